```python
import jax, jax.numpy as jnp
from jax import lax
import numpy as np

D_MODEL = 1024
BATCH = 4
SEQ = 8192
DEPTH = 2

HEAD_DIM = 64
ATTN_WIDTH = D_MODEL // 2
N_Q_HEADS = ATTN_WIDTH // HEAD_DIM
N_KV_HEADS = 2
Q_PER_KV = N_Q_HEADS // N_KV_HEADS
KV_WIDTH = N_KV_HEADS * HEAD_DIM
WINDOW = 128
BLOCK = 128
CONV_WIDTH = D_MODEL // 4
CONV_KERNEL = 31
LRU_WIDTH = D_MODEL // 4
LRU_HEADS = 4
LRU_HEAD_DIM = LRU_WIDTH // LRU_HEADS
LRU_CONV_KERNEL = 4
LRU_C = 8.0
MIX_WIDTH = ATTN_WIDTH + CONV_WIDTH + LRU_WIDTH
IN_SPLIT_SIZES = (ATTN_WIDTH, KV_WIDTH, KV_WIDTH, CONV_WIDTH, CONV_WIDTH, LRU_WIDTH, LRU_WIDTH)
IN_WIDTH = sum(IN_SPLIT_SIZES)
IN_SPLIT_IDX = [int(v) for v in np.cumsum(IN_SPLIT_SIZES)[:-1]]
D_FF = 4 * D_MODEL
RMS_EPS = 1e-6
LN_EPS = 1e-5
MASK_VALUE = -1e30

kernel_name = "hymba_style_swa_conformer_rglru_hybrid"


def rms_norm(x, g):
    xf = x.astype(jnp.float32)
    y = xf * lax.rsqrt(jnp.mean(xf * xf, axis=-1, keepdims=True) + RMS_EPS)
    return (y * g.astype(jnp.float32)).astype(x.dtype)


def layer_norm(x, g, b):
    xf = x.astype(jnp.float32)
    mu = jnp.mean(xf, axis=-1, keepdims=True)
    xc = xf - mu
    y = xc * lax.rsqrt(jnp.mean(xc * xc, axis=-1, keepdims=True) + LN_EPS)
    return (y * g.astype(jnp.float32) + b.astype(jnp.float32)).astype(x.dtype)


def causal_depthwise_conv(x, w, b):
    k_width, c = w.shape
    out = lax.conv_general_dilated(
        x, w[:, None, :].astype(x.dtype), window_strides=(1,), padding=[(k_width - 1, 0)],
        dimension_numbers=("NWC", "WIO", "NWC"), feature_group_count=c)
    return out + b.astype(x.dtype)


def sliding_window_attention(q, k, v, sinks):
    b, s, _ = q.shape
    nb = s // BLOCK
    q = q.reshape(b, nb, BLOCK, N_KV_HEADS, Q_PER_KV, HEAD_DIM)
    k = k.reshape(b, nb, BLOCK, N_KV_HEADS, HEAD_DIM)
    v = v.reshape(b, nb, BLOCK, N_KV_HEADS, HEAD_DIM)
    k_band = jnp.concatenate([jnp.concatenate([jnp.zeros_like(k[:, :1]), k[:, :-1]], axis=1), k], axis=2)
    v_band = jnp.concatenate([jnp.concatenate([jnp.zeros_like(v[:, :1]), v[:, :-1]], axis=1), v], axis=2)
    scores = jnp.einsum("bnqhgd,bnkhd->bnhgqk", q, k_band).astype(jnp.float32) * (HEAD_DIM ** -0.5)
    blk = jnp.arange(nb)[:, None]
    q_pos = blk * BLOCK + jnp.arange(BLOCK)[None, :]
    k_pos = (blk - 1) * BLOCK + jnp.arange(2 * BLOCK)[None, :]
    diff = q_pos[:, :, None] - k_pos[:, None, :]
    mask = (diff >= 0) & (diff < WINDOW) & (k_pos[:, None, :] >= 0)
    scores = jnp.where(mask[None, :, None, None], scores, MASK_VALUE)
    sink = sinks.astype(jnp.float32).reshape(N_KV_HEADS, Q_PER_KV)[None, None, :, :, None, None]
    m = jnp.maximum(jnp.max(scores, axis=-1, keepdims=True), sink)
    p = jnp.exp(scores - m)
    probs = p / (jnp.sum(p, axis=-1, keepdims=True) + jnp.exp(sink - m))
    out = jnp.einsum("bnhgqk,bnkhd->bnqhgd", probs.astype(v.dtype), v_band)
    return out.reshape(b, s, ATTN_WIDTH)


def conformer_conv(u_val, u_gate, dw_w, dw_b, ln_g, ln_b):
    u = u_val * jax.nn.sigmoid(u_gate)
    u = causal_depthwise_conv(u, dw_w, dw_b)
    u = layer_norm(u, ln_g, ln_b)
    return jax.nn.silu(u)


def _linear_recurrence_combine(c1, c2):
    a1, b1 = c1
    a2, b2 = c2
    return a1 * a2, a2 * b1 + b2


def rglru_branch(u_x, u_gate, conv_w, conv_b, wa, ba, wx, bx, lam):
    xc = causal_depthwise_conv(u_x, conv_w, conv_b)
    b, s, _ = xc.shape
    xh = xc.reshape(b, s, LRU_HEADS, LRU_HEAD_DIM)
    r = jax.nn.sigmoid(jnp.einsum("bshi,hij->bshj", xh, wa) + ba).reshape(b, s, LRU_WIDTH)
    i = jax.nn.sigmoid(jnp.einsum("bshi,hij->bshj", xh, wx) + bx).reshape(b, s, LRU_WIDTH)
    log_a = (-LRU_C * r.astype(jnp.float32)) * jax.nn.softplus(-lam.astype(jnp.float32))
    a = jnp.exp(log_a)
    gated_x = jnp.sqrt(-jnp.expm1(2.0 * log_a)) * (i * xc).astype(jnp.float32)
    _, h = lax.associative_scan(_linear_recurrence_combine, (a, gated_x), axis=1)
    return h.astype(u_x.dtype) * jax.nn.gelu(u_gate)


def setup_inputs(seed: int = 0) -> dict:
    key = jax.random.key(seed)
    ks = jax.random.split(key, 24)
    f32 = jnp.float32

    def nrm(k, shape, scale):
        return jax.random.normal(k, shape, f32) * scale

    def gain(k, shape):
        return 1.0 + 0.02 * jax.random.normal(k, shape, f32)

    a0 = jax.random.uniform(ks[14], (DEPTH, LRU_WIDTH), f32, 0.9, 0.999)
    s0 = a0 ** (1.0 / LRU_C)
    lru_lambda = jnp.log(s0) - jnp.log1p(-s0)
    return {
        "x": jax.random.normal(ks[0], (BATCH, SEQ, D_MODEL), f32),
        "norm1": gain(ks[1], (DEPTH, D_MODEL)),
        "w_in": nrm(ks[2], (DEPTH, D_MODEL, IN_WIDTH), D_MODEL ** -0.5),
        "attn_sinks": nrm(ks[3], (DEPTH, N_Q_HEADS), 0.5),
        "conv_dw_w": nrm(ks[4], (DEPTH, CONV_KERNEL, CONV_WIDTH), CONV_KERNEL ** -0.5),
        "conv_dw_b": nrm(ks[5], (DEPTH, CONV_WIDTH), 0.01),
        "conv_ln_g": gain(ks[6], (DEPTH, CONV_WIDTH)),
        "conv_ln_b": nrm(ks[7], (DEPTH, CONV_WIDTH), 0.01),
        "lru_conv_w": nrm(ks[8], (DEPTH, LRU_CONV_KERNEL, LRU_WIDTH), LRU_CONV_KERNEL ** -0.5),
        "lru_conv_b": nrm(ks[9], (DEPTH, LRU_WIDTH), 0.01),
        "lru_wa": nrm(ks[10], (DEPTH, LRU_HEADS, LRU_HEAD_DIM, LRU_HEAD_DIM), LRU_HEAD_DIM ** -0.5),
        "lru_ba": nrm(ks[11], (DEPTH, LRU_HEADS, LRU_HEAD_DIM), 0.01),
        "lru_wx": nrm(ks[12], (DEPTH, LRU_HEADS, LRU_HEAD_DIM, LRU_HEAD_DIM), LRU_HEAD_DIM ** -0.5),
        "lru_bx": nrm(ks[13], (DEPTH, LRU_HEADS, LRU_HEAD_DIM), 0.01),
        "lru_lambda": lru_lambda,
        "mix_norm": gain(ks[15], (DEPTH, MIX_WIDTH)),
        "w_out": nrm(ks[16], (DEPTH, MIX_WIDTH, D_MODEL), MIX_WIDTH ** -0.5),
        "norm2": gain(ks[17], (DEPTH, D_MODEL)),
        "w_up": nrm(ks[18], (DEPTH, D_MODEL, D_FF), D_MODEL ** -0.5),
        "w_down": nrm(ks[19], (DEPTH, D_FF, D_MODEL), D_FF ** -0.5),
        "final_norm": gain(ks[20], (D_MODEL,)),
    }


def reference(x, norm1, w_in, attn_sinks, conv_dw_w, conv_dw_b, conv_ln_g, conv_ln_b,
              lru_conv_w, lru_conv_b, lru_wa, lru_ba, lru_wx, lru_bx, lru_lambda,
              mix_norm, w_out, norm2, w_up, w_down, final_norm):
    h = x
    a_end = ATTN_WIDTH
    c_end = ATTN_WIDTH + CONV_WIDTH
    for l in range(DEPTH):
        hn = rms_norm(h, norm1[l])
        z = hn @ w_in[l]
        q, k, v, c_val, c_gate, r_x, r_gate = jnp.split(z, IN_SPLIT_IDX, axis=-1)
        y_attn = sliding_window_attention(q, k, v, attn_sinks[l])
        y_conv = conformer_conv(c_val, c_gate, conv_dw_w[l], conv_dw_b[l], conv_ln_g[l], conv_ln_b[l])
        y_lru = rglru_branch(r_x, r_gate, lru_conv_w[l], lru_conv_b[l], lru_wa[l], lru_ba[l],
                             lru_wx[l], lru_bx[l], lru_lambda[l])
        g = mix_norm[l]
        y = jnp.concatenate([rms_norm(y_attn, g[:a_end]),
                             rms_norm(y_conv, g[a_end:c_end]),
                             rms_norm(y_lru, g[c_end:])], axis=-1)
        h = h + y @ w_out[l]
        hn = rms_norm(h, norm2[l])
        h = h + jnp.square(jax.nn.relu(hn @ w_up[l])) @ w_down[l]
    return rms_norm(h, final_norm)
```

```python
import functools

import jax
import jax.numpy as jnp
from jax import lax
from jax.experimental import pallas as pl
from jax.experimental.pallas import tpu as pltpu

D_MODEL = 1024
HEAD_DIM = 64
ATTN_WIDTH = 512
N_Q_HEADS = 8
N_KV_HEADS = 2
KV_WIDTH = 128
WINDOW = 128
CONV_WIDTH = 256
CONV_KERNEL = 31
LRU_WIDTH = 256
LRU_HEADS = 4
LRU_HEAD_DIM = 64
LRU_CONV_KERNEL = 4
LRU_C = 8.0
MIX_WIDTH = 1024
IN_WIDTH = 1792
D_FF = 4096
RMS_EPS = 1e-6
LN_EPS = 1e-5
MASK_VALUE = -1e30

Q_OFF, K_OFF, V_OFF, CV_OFF, CG_OFF, RX_OFF, RG_OFF = 0, 512, 640, 768, 1024, 1280, 1536

LANES = 128
SUBLANES = 8

MIX_T = 512
FFN_T = 512
FF_CHUNK = 1024
CONV_HALO = 32
LRU_HALO = 8
CONV_ROWS = 64
VMEM_LIMIT = 56 * 1024 * 1024

F32 = jnp.float32
BF16 = jnp.bfloat16


def _rms(x, g):
    return x * lax.rsqrt(jnp.mean(x * x, axis=-1, keepdims=True) + RMS_EPS) * g


def _mixer_kernel(sinks_ref, x_ref, n1_ref, win_ref, cw_ref, cb_ref, lng_ref, lnb_ref,
                  lcw_ref, lcb_ref, wg_ref, bg_ref, lam_ref, mixg_ref, wout_ref,
                  o_ref,
                  kt_ext, v_ext, vr_ext, u_ext, xl_ext, hcar, bias_s, ya_s, yc_s, a_s, b_s):
    T = MIX_T
    s = pl.program_id(1)

    @pl.when(s == 0)
    def _():
        kt_ext[:, 0:WINDOW] = jnp.zeros((KV_WIDTH, WINDOW), BF16)
        v_ext[0:WINDOW, :] = jnp.zeros((WINDOW, KV_WIDTH), BF16)
        vr_ext[0:WINDOW, :] = jnp.zeros((WINDOW, KV_WIDTH), BF16)
        u_ext[0:CONV_HALO, :] = jnp.zeros((CONV_HALO, CONV_WIDTH), F32)
        xl_ext[0:LRU_HALO, :] = jnp.zeros((LRU_HALO, LRU_WIDTH), F32)
        hcar[...] = jnp.zeros((SUBLANES, LRU_WIDTH), F32)

    qi = lax.broadcasted_iota(jnp.int32, (WINDOW, 2 * WINDOW), 0)
    kc = lax.broadcasted_iota(jnp.int32, (WINDOW, 2 * WINDOW), 1)
    band = (kc > qi) & (kc <= qi + WINDOW)
    bias_s[0] = jnp.where(band, 0.0, MASK_VALUE).astype(F32)
    bias_s[1] = jnp.where(band & (kc >= WINDOW), 0.0, MASK_VALUE).astype(F32)

    x = x_ref[...]
    hn = _rms(x, n1_ref[...]).astype(BF16)
    z = jnp.dot(hn, win_ref[...], preferred_element_type=F32)

    q = z[:, Q_OFF:Q_OFF + ATTN_WIDTH].astype(BF16)
    kf = z[:, K_OFF:K_OFF + KV_WIDTH]
    vf = z[:, V_OFF:V_OFF + KV_WIDTH]
    kt_ext[:, WINDOW:] = kf.T.astype(BF16)
    v_ext[WINDOW:, :] = vf.astype(BF16)
    vr_ext[WINDOW:, :] = pltpu.roll(vf, HEAD_DIM, axis=1).astype(BF16)

    lane = lax.broadcasted_iota(jnp.int32, (2 * WINDOW, KV_WIDTH), 1)
    low = lane < HEAD_DIM
    lane_o = lax.broadcasted_iota(jnp.int32, (WINDOW, LANES), 1)
    zk = jnp.zeros((HEAD_DIM, 2 * WINDOW), BF16)

    for j in range(T // WINDOW):
        r0 = j * WINDOW
        kt_band = kt_ext[:, r0:r0 + 2 * WINDOW]
        v_band = v_ext[r0:r0 + 2 * WINDOW, :]
        vr_band = vr_ext[r0:r0 + 2 * WINDOW, :]
        bias = bias_s[jnp.where(s == 0, 1, 0)] if j == 0 else bias_s[0]
        for h in range(N_KV_HEADS):
            kh = kt_band[h * HEAD_DIM:(h + 1) * HEAD_DIM, :]
            kt2 = jnp.concatenate(
                [jnp.concatenate([kh, zk], axis=0), jnp.concatenate([zk, kh], axis=0)], axis=1)
            if h == 0:
                va = jnp.where(low, v_band, jnp.zeros_like(v_band))
                vb = jnp.where(low, jnp.zeros_like(vr_band), vr_band)
            else:
                va = jnp.where(low, vr_band, jnp.zeros_like(vr_band))
                vb = jnp.where(low, jnp.zeros_like(v_band), v_band)
            v2 = jnp.concatenate([va, vb], axis=0)
            c0 = h * 2 * LANES
            qs = jnp.concatenate([q[r0:r0 + WINDOW, c0:c0 + LANES],
                                  q[r0:r0 + WINDOW, c0 + LANES:c0 + 2 * LANES]], axis=0)
            sc = jnp.dot(qs, kt2, preferred_element_type=F32)
            p_rows = []
            dens = []
            for r in range(2):
                p_cols = []
                den_r = []
                for e in range(2):
                    sink = sinks_ref[4 * h + 2 * r + e]
                    t = sc[r * WINDOW:(r + 1) * WINDOW, e * 2 * WINDOW:(e + 1) * 2 * WINDOW] + bias
                    m = jnp.maximum(jnp.max(t, axis=-1, keepdims=True), sink)
                    p = jnp.exp(t - m)
                    den_r.append(jnp.sum(p, axis=-1, keepdims=True) + jnp.exp(sink - m))
                    p_cols.append(p.astype(BF16))
                p_rows.append(jnp.concatenate(p_cols, axis=1))
                dens.append(den_r)
            pm = jnp.concatenate(p_rows, axis=0)
            o = jnp.dot(pm, v2, preferred_element_type=F32)
            for r in range(2):
                den = jnp.where(lane_o < HEAD_DIM, dens[r][0], dens[r][1])
                col = (2 * h + r) * LANES
                ya_s[r0:r0 + WINDOW, col:col + LANES] = o[r * WINDOW:(r + 1) * WINDOW, :] / den

    kt_ext[:, 0:WINDOW] = kt_ext[:, T:T + WINDOW]
    v_ext[0:WINDOW, :] = v_ext[T:T + WINDOW, :]
    vr_ext[0:WINDOW, :] = vr_ext[T:T + WINDOW, :]

    cval = z[:, CV_OFF:CV_OFF + CONV_WIDTH]
    cgate = z[:, CG_OFF:CG_OFF + CONV_WIDTH]
    u_ext[CONV_HALO:, :] = cval * jax.nn.sigmoid(cgate)
    base = CONV_HALO - (CONV_KERNEL - 1)
    cb = cb_ref[...]
    for c in range(T // CONV_ROWS):
        acc = jnp.broadcast_to(cb, (CONV_ROWS, CONV_WIDTH))
        for k in range(CONV_KERNEL):
            acc = acc + cw_ref[k:k + 1, :] * u_ext[c * CONV_ROWS + base + k:c * CONV_ROWS + base + k + CONV_ROWS, :]
        yc_s[c * CONV_ROWS:(c + 1) * CONV_ROWS, :] = acc
    u_ext[0:CONV_HALO, :] = u_ext[T:T + CONV_HALO, :]
    uc = yc_s[...]
    mu = jnp.mean(uc, axis=-1, keepdims=True)
    xc_ = uc - mu
    ln = xc_ * lax.rsqrt(jnp.mean(xc_ * xc_, axis=-1, keepdims=True) + LN_EPS) * lng_ref[...] + lnb_ref[...]
    y_conv = ln * jax.nn.sigmoid(ln)

    rx = z[:, RX_OFF:RX_OFF + LRU_WIDTH]
    rg = z[:, RG_OFF:RG_OFF + LRU_WIDTH]
    xl_ext[LRU_HALO:, :] = rx
    lbase = LRU_HALO - (LRU_CONV_KERNEL - 1)
    xc = jnp.broadcast_to(lcb_ref[...], (T, LRU_WIDTH))
    for k in range(LRU_CONV_KERNEL):
        xc = xc + lcw_ref[k:k + 1, :] * xl_ext[lbase + k:lbase + k + T, :]
    xl_ext[0:LRU_HALO, :] = xl_ext[T:T + LRU_HALO, :]
    gates = jnp.dot(xc.astype(BF16), wg_ref[...], preferred_element_type=F32) + bg_ref[...]
    rgate = jax.nn.sigmoid(gates[:, :LRU_WIDTH])
    igate = jax.nn.sigmoid(gates[:, LRU_WIDTH:])
    log_a = (-LRU_C * rgate) * jax.nn.softplus(-lam_ref[...])
    a_s[...] = jnp.exp(log_a)
    th = jnp.tanh(log_a)
    b_s[...] = jnp.sqrt(2.0 * th / (th - 1.0)) * (igate * xc)

    row = lax.broadcasted_iota(jnp.int32, (SUBLANES, LRU_WIDTH), 0)

    def scan_body(g, hprev):
        g0 = pl.multiple_of(g * SUBLANES, SUBLANES)
        a = a_s[pl.ds(g0, SUBLANES), :]
        b = b_s[pl.ds(g0, SUBLANES), :]
        for d in (1, 2, 4):
            a_sh = jnp.where(row >= d, pltpu.roll(a, d, axis=0), 1.0)
            b_sh = jnp.where(row >= d, pltpu.roll(b, d, axis=0), 0.0)
            b = a * b_sh + b
            a = a * a_sh
        hh = a * hprev + b
        b_s[pl.ds(g0, SUBLANES), :] = hh
        return jnp.broadcast_to(hh[SUBLANES - 1:SUBLANES, :], (SUBLANES, LRU_WIDTH))

    hcar[...] = lax.fori_loop(0, T // SUBLANES, scan_body, hcar[...], unroll=4)
    y_lru = b_s[...] * jax.nn.gelu(rg)

    mg = mixg_ref[...]
    y = jnp.concatenate([
        _rms(ya_s[...], mg[:, :ATTN_WIDTH]).astype(BF16),
        _rms(y_conv, mg[:, ATTN_WIDTH:ATTN_WIDTH + CONV_WIDTH]).astype(BF16),
        _rms(y_lru, mg[:, ATTN_WIDTH + CONV_WIDTH:]).astype(BF16)], axis=-1)
    o_ref[...] = x + jnp.dot(y, wout_ref[...], preferred_element_type=F32)


def _ffn_kernel(h_ref, n2_ref, wup_ref, wdn_ref, fn_ref, o_ref, *, final):
    h = h_ref[...]
    hn = _rms(h, n2_ref[...]).astype(BF16)
    acc = h
    for c in range(D_FF // FF_CHUNK):
        u = jnp.dot(hn, wup_ref[:, c * FF_CHUNK:(c + 1) * FF_CHUNK], preferred_element_type=F32)
        u = jnp.square(jnp.maximum(u, 0.0)).astype(BF16)
        acc = acc + jnp.dot(u, wdn_ref[c * FF_CHUNK:(c + 1) * FF_CHUNK, :], preferred_element_type=F32)
    if final:
        acc = _rms(acc, fn_ref[...])
    o_ref[...] = acc


def _const_spec(shape):
    nd = len(shape)
    return pl.BlockSpec(shape, lambda *_: (0,) * nd, pipeline_mode=pl.Buffered(1))


def _mixer_call(x, sinks, n1, win, cw, cb, lng, lnb, lcw, lcb, wg, bg, lam, mixg, wout):
    B, S, D = x.shape
    T = MIX_T
    consts = (n1, win, cw, cb, lng, lnb, lcw, lcb, wg, bg, lam, mixg, wout)
    return pl.pallas_call(
        _mixer_kernel,
        out_shape=jax.ShapeDtypeStruct((B, S, D), F32),
        grid=(B, S // T),
        in_specs=[pl.BlockSpec(memory_space=pltpu.SMEM),
                  pl.BlockSpec((None, T, D), lambda b, s: (b, s, 0))]
                 + [_const_spec(c.shape) for c in consts],
        out_specs=pl.BlockSpec((None, T, D), lambda b, s: (b, s, 0)),
        scratch_shapes=[
            pltpu.VMEM((KV_WIDTH, WINDOW + T), BF16),
            pltpu.VMEM((WINDOW + T, KV_WIDTH), BF16),
            pltpu.VMEM((WINDOW + T, KV_WIDTH), BF16),
            pltpu.VMEM((CONV_HALO + T, CONV_WIDTH), F32),
            pltpu.VMEM((LRU_HALO + T, LRU_WIDTH), F32),
            pltpu.VMEM((SUBLANES, LRU_WIDTH), F32),
            pltpu.VMEM((2, WINDOW, 2 * WINDOW), F32),
            pltpu.VMEM((T, ATTN_WIDTH), F32),
            pltpu.VMEM((T, CONV_WIDTH), F32),
            pltpu.VMEM((T, LRU_WIDTH), F32),
            pltpu.VMEM((T, LRU_WIDTH), F32),
        ],
        compiler_params=pltpu.CompilerParams(
            dimension_semantics=("arbitrary", "arbitrary"), vmem_limit_bytes=VMEM_LIMIT),
        name="mixer",
    )(sinks, x, *consts)


def _ffn_call(h, n2, wup, wdn, fn, final):
    M, D = h.shape
    T = FFN_T
    return pl.pallas_call(
        functools.partial(_ffn_kernel, final=final),
        out_shape=jax.ShapeDtypeStruct((M, D), F32),
        grid=(M // T,),
        in_specs=[pl.BlockSpec((T, D), lambda i: (i, 0)),
                  _const_spec(n2.shape), _const_spec(wup.shape), _const_spec(wdn.shape), _const_spec(fn.shape)],
        out_specs=pl.BlockSpec((T, D), lambda i: (i, 0)),
        compiler_params=pltpu.CompilerParams(
            dimension_semantics=("arbitrary",), vmem_limit_bytes=VMEM_LIMIT),
        name="ffn_final" if final else "ffn",
    )(h, n2, wup, wdn, fn)


def _block_diag(w):
    hN, di, dj = w.shape
    eye = jnp.eye(hN, dtype=w.dtype)
    return (eye[:, None, :, None] * w[:, :, None, :]).reshape(hN * di, hN * dj)


def kernel(x, norm1, w_in, attn_sinks, conv_dw_w, conv_dw_b, conv_ln_g, conv_ln_b, lru_conv_w, lru_conv_b,
           lru_wa, lru_ba, lru_wx, lru_bx, lru_lambda, mix_norm, w_out, norm2, w_up, w_down, final_norm):
    B, S, D = x.shape
    depth = w_in.shape[0]
    qscale = jnp.concatenate([jnp.full((ATTN_WIDTH,), HEAD_DIM ** -0.5, F32),
                              jnp.ones((IN_WIDTH - ATTN_WIDTH,), F32)])
    h = x
    for l in range(depth):
        win = (w_in[l] * qscale).astype(BF16)
        wg = jnp.concatenate([_block_diag(lru_wa[l]), _block_diag(lru_wx[l])], axis=1).astype(BF16)
        bg = jnp.concatenate([lru_ba[l].reshape(1, -1), lru_bx[l].reshape(1, -1)], axis=1)
        h = _mixer_call(
            h, attn_sinks[l], norm1[l][None], win, conv_dw_w[l], conv_dw_b[l][None],
            conv_ln_g[l][None], conv_ln_b[l][None], lru_conv_w[l], lru_conv_b[l][None],
            wg, bg, lru_lambda[l][None], mix_norm[l][None], w_out[l].astype(BF16))
        h = _ffn_call(h.reshape(B * S, D), norm2[l][None], w_up[l].astype(BF16), w_down[l].astype(BF16),
                      final_norm[None], final=(l == depth - 1)).reshape(B, S, D)
    return h
```

```python
import functools

import jax
import jax.numpy as jnp
from jax import lax
from jax.experimental import pallas as pl
from jax.experimental.pallas import tpu as pltpu

D_MODEL = 1024
HEAD_DIM = 64
ATTN_WIDTH = 512
N_Q_HEADS = 8
N_KV_HEADS = 2
KV_WIDTH = 128
WINDOW = 128
CONV_WIDTH = 256
CONV_KERNEL = 31
LRU_WIDTH = 256
LRU_HEADS = 4
LRU_HEAD_DIM = 64
LRU_CONV_KERNEL = 4
LRU_C = 8.0
MIX_WIDTH = 1024
IN_WIDTH = 1792
D_FF = 4096
RMS_EPS = 1e-6
LN_EPS = 1e-5
MASK_VALUE = -1e30

Q_OFF, K_OFF, V_OFF, CV_OFF, CG_OFF, RX_OFF, RG_OFF = 0, 512, 640, 768, 1024, 1280, 1536

LANES = 128
SUBLANES = 8

MIX_T = 512
FFN_T = 512
FF_CHUNK = 1024
CONV_HALO = 32
LRU_HALO = 8
CONV_ROWS = 128
SCAN_LEN = MIX_T // SUBLANES
SCAN_PITCH = SCAN_LEN + SUBLANES
VMEM_LIMIT = 56 * 1024 * 1024

F32 = jnp.float32
BF16 = jnp.bfloat16


def _sigmoid(x):
    return 0.5 * jnp.tanh(0.5 * x) + 0.5


def _rms(x, g):
    return x * lax.rsqrt(jnp.mean(x * x, axis=-1, keepdims=True) + RMS_EPS) * g


def _mixer_kernel(sinks_ref, x_ref, n1_ref, win_ref, cw_ref, cb_ref, lng_ref, lnb_ref,
                  lcw_ref, lcb_ref, wg_ref, bg_ref, lam_ref, mixg_ref, wout_ref,
                  o_ref,
                  kt_ext, v_ext, vr_ext, u_ext, xl_ext, hcar, bias_s, ya_s, yc_s, a_s, b_s, yl_s):
    T = MIX_T
    s = pl.program_id(1)

    @pl.when(s == 0)
    def _():
        kt_ext[:, 0:WINDOW] = jnp.zeros((KV_WIDTH, WINDOW), BF16)
        v_ext[0:WINDOW, :] = jnp.zeros((WINDOW, KV_WIDTH), BF16)
        vr_ext[0:WINDOW, :] = jnp.zeros((WINDOW, KV_WIDTH), BF16)
        u_ext[0:CONV_HALO, :] = jnp.zeros((CONV_HALO, CONV_WIDTH), F32)
        xl_ext[0:LRU_HALO, :] = jnp.zeros((LRU_HALO, LRU_WIDTH), F32)
        hcar[...] = jnp.zeros((SUBLANES, LRU_WIDTH), F32)

    qi = lax.broadcasted_iota(jnp.int32, (WINDOW, 2 * WINDOW), 0)
    kc = lax.broadcasted_iota(jnp.int32, (WINDOW, 2 * WINDOW), 1)
    band = (kc > qi) & (kc <= qi + WINDOW)
    bias_s[0] = jnp.where(band, 0.0, MASK_VALUE).astype(F32)
    bias_s[1] = jnp.where(band & (kc >= WINDOW), 0.0, MASK_VALUE).astype(F32)

    x = x_ref[...]
    hn = _rms(x, n1_ref[...]).astype(BF16)
    z = jnp.dot(hn, win_ref[...], preferred_element_type=F32)

    q = z[:, Q_OFF:Q_OFF + ATTN_WIDTH].astype(BF16)
    kf = z[:, K_OFF:K_OFF + KV_WIDTH]
    vf = z[:, V_OFF:V_OFF + KV_WIDTH]
    kt_ext[:, WINDOW:] = kf.T.astype(BF16)
    v_ext[WINDOW:, :] = vf.astype(BF16)
    vr_ext[WINDOW:, :] = pltpu.roll(vf, HEAD_DIM, axis=1).astype(BF16)

    lane = lax.broadcasted_iota(jnp.int32, (2 * WINDOW, KV_WIDTH), 1)
    low = lane < HEAD_DIM
    lane_o = lax.broadcasted_iota(jnp.int32, (WINDOW, LANES), 1)
    zk = jnp.zeros((HEAD_DIM, 2 * WINDOW), BF16)

    for j in range(T // WINDOW):
        r0 = j * WINDOW
        kt_band = kt_ext[:, r0:r0 + 2 * WINDOW]
        v_band = v_ext[r0:r0 + 2 * WINDOW, :]
        vr_band = vr_ext[r0:r0 + 2 * WINDOW, :]
        bias = bias_s[jnp.where(s == 0, 1, 0)] if j == 0 else bias_s[0]
        for h in range(N_KV_HEADS):
            kh = kt_band[h * HEAD_DIM:(h + 1) * HEAD_DIM, :]
            kt2 = jnp.concatenate(
                [jnp.concatenate([kh, zk], axis=0), jnp.concatenate([zk, kh], axis=0)], axis=1)
            if h == 0:
                va = jnp.where(low, v_band, jnp.zeros_like(v_band))
                vb = jnp.where(low, jnp.zeros_like(vr_band), vr_band)
            else:
                va = jnp.where(low, vr_band, jnp.zeros_like(vr_band))
                vb = jnp.where(low, jnp.zeros_like(v_band), v_band)
            v2 = jnp.concatenate([va, vb], axis=0)
            c0 = h * 2 * LANES
            qs = jnp.concatenate([q[r0:r0 + WINDOW, c0:c0 + LANES],
                                  q[r0:r0 + WINDOW, c0 + LANES:c0 + 2 * LANES]], axis=0)
            sc = jnp.dot(qs, kt2, preferred_element_type=F32)
            p_rows = []
            dens = []
            for r in range(2):
                p_cols = []
                den_r = []
                for e in range(2):
                    sink = sinks_ref[4 * h + 2 * r + e]
                    t = sc[r * WINDOW:(r + 1) * WINDOW, e * 2 * WINDOW:(e + 1) * 2 * WINDOW] + bias
                    m = jnp.maximum(jnp.max(t, axis=-1, keepdims=True), sink)
                    p = jnp.exp(t - m)
                    den_r.append(jnp.sum(p, axis=-1, keepdims=True) + jnp.exp(sink - m))
                    p_cols.append(p.astype(BF16))
                p_rows.append(jnp.concatenate(p_cols, axis=1))
                dens.append(den_r)
            pm = jnp.concatenate(p_rows, axis=0)
            o = jnp.dot(pm, v2, preferred_element_type=F32)
            for r in range(2):
                den = jnp.where(lane_o < HEAD_DIM, dens[r][0], dens[r][1])
                col = (2 * h + r) * LANES
                ya_s[r0:r0 + WINDOW, col:col + LANES] = o[r * WINDOW:(r + 1) * WINDOW, :] / den

    kt_ext[:, 0:WINDOW] = kt_ext[:, T:T + WINDOW]
    v_ext[0:WINDOW, :] = v_ext[T:T + WINDOW, :]
    vr_ext[0:WINDOW, :] = vr_ext[T:T + WINDOW, :]

    cval = z[:, CV_OFF:CV_OFF + CONV_WIDTH]
    cgate = z[:, CG_OFF:CG_OFF + CONV_WIDTH]
    u_ext[CONV_HALO:, :] = cval * _sigmoid(cgate)
    base = CONV_HALO - (CONV_KERNEL - 1)
    for lt in range(CONV_WIDTH // LANES):
        cl = slice(lt * LANES, (lt + 1) * LANES)
        for c in range(T // CONV_ROWS):
            r0 = c * CONV_ROWS
            acc = jnp.broadcast_to(cb_ref[:, cl], (CONV_ROWS, LANES))
            for sh in range(SUBLANES):
                part = None
                nrows = CONV_ROWS + (SUBLANES if sh else 0)
                for k in range(CONV_KERNEL):
                    if (base + k) % SUBLANES != sh:
                        continue
                    al = r0 + base + k - sh
                    term = cw_ref[k:k + 1, cl] * u_ext[al:al + nrows, cl]
                    part = term if part is None else part + term
                acc = acc + part[sh:sh + CONV_ROWS, :]
            yc_s[r0:r0 + CONV_ROWS, cl] = acc
    u_ext[0:CONV_HALO, :] = u_ext[T:T + CONV_HALO, :]
    uc = yc_s[...]
    mu = jnp.mean(uc, axis=-1, keepdims=True)
    xc_ = uc - mu
    ln = xc_ * lax.rsqrt(jnp.mean(xc_ * xc_, axis=-1, keepdims=True) + LN_EPS) * lng_ref[...] + lnb_ref[...]
    y_conv = ln * _sigmoid(ln)

    rx = z[:, RX_OFF:RX_OFF + LRU_WIDTH]
    rg = z[:, RG_OFF:RG_OFF + LRU_WIDTH]
    xl_ext[LRU_HALO:, :] = rx
    lbase = LRU_HALO - (LRU_CONV_KERNEL - 1)
    xc = jnp.broadcast_to(lcb_ref[...], (T, LRU_WIDTH))
    for k in range(LRU_CONV_KERNEL):
        xc = xc + lcw_ref[k:k + 1, :] * xl_ext[lbase + k:lbase + k + T, :]
    xl_ext[0:LRU_HALO, :] = xl_ext[T:T + LRU_HALO, :]
    gates = jnp.dot(xc.astype(BF16), wg_ref[...], preferred_element_type=F32) + bg_ref[...]
    rgate = _sigmoid(gates[:, :LRU_WIDTH])
    igate = _sigmoid(gates[:, LRU_WIDTH:])
    log_a = (-LRU_C * rgate) * jax.nn.softplus(-lam_ref[...])
    a_full = jnp.exp(log_a)
    th = jnp.tanh(log_a)
    b_full = jnp.sqrt(2.0 * th / (th - 1.0)) * (igate * xc)

    L, P = SCAN_LEN, SCAN_PITCH
    row = lax.broadcasted_iota(jnp.int32, (SUBLANES, LANES), 0)
    for lt in range(LRU_WIDTH // LANES):
        cl = slice(lt * LANES, (lt + 1) * LANES)
        for c in range(SUBLANES):
            a_s[lt, c * P:c * P + L, :] = a_full[c * L:(c + 1) * L, cl]
            b_s[lt, c * P:c * P + L, :] = b_full[c * L:(c + 1) * L, cl]
        hloc = jnp.zeros((SUBLANES, LANES), F32)
        cum = jnp.ones((SUBLANES, LANES), F32)
        for m in range(L):
            am = a_s[lt, pl.ds(m, SUBLANES, stride=P), :]
            bm = b_s[lt, pl.ds(m, SUBLANES, stride=P), :]
            hloc = am * hloc + bm
            cum = am * cum
            b_s[lt, pl.ds(m, SUBLANES, stride=P), :] = hloc
            a_s[lt, pl.ds(m, SUBLANES, stride=P), :] = cum
        ca, cbv = cum, hloc
        for d in (1, 2, 4):
            a_sh = jnp.where(row >= d, pltpu.roll(ca, d, axis=0), 1.0)
            b_sh = jnp.where(row >= d, pltpu.roll(cbv, d, axis=0), 0.0)
            cbv = ca * b_sh + cbv
            ca = ca * a_sh
        hprev = hcar[:, cl]
        ends = ca * hprev + cbv
        carry_in = jnp.where(row == 0, hprev, pltpu.roll(ends, 1, axis=0))
        hcar[:, cl] = jnp.broadcast_to(ends[SUBLANES - 1:SUBLANES, :], (SUBLANES, LANES))
        for c in range(SUBLANES):
            g = jnp.broadcast_to(carry_in[c:c + 1, :], (L, LANES))
            yl_s[c * L:(c + 1) * L, cl] = b_s[lt, c * P:c * P + L, :] + a_s[lt, c * P:c * P + L, :] * g
    y_lru = yl_s[...] * jax.nn.gelu(rg)

    mg = mixg_ref[...]
    y = jnp.concatenate([
        _rms(ya_s[...], mg[:, :ATTN_WIDTH]).astype(BF16),
        _rms(y_conv, mg[:, ATTN_WIDTH:ATTN_WIDTH + CONV_WIDTH]).astype(BF16),
        _rms(y_lru, mg[:, ATTN_WIDTH + CONV_WIDTH:]).astype(BF16)], axis=-1)
    o_ref[...] = x + jnp.dot(y, wout_ref[...], preferred_element_type=F32)


def _ffn_kernel(h_ref, n2_ref, wup_ref, wdn_ref, fn_ref, o_ref, *, final):
    h = h_ref[...]
    hn = _rms(h, n2_ref[...]).astype(BF16)
    acc = h
    for c in range(D_FF // FF_CHUNK):
        u = jnp.dot(hn, wup_ref[:, c * FF_CHUNK:(c + 1) * FF_CHUNK], preferred_element_type=F32)
        u = jnp.square(jnp.maximum(u, 0.0)).astype(BF16)
        acc = acc + jnp.dot(u, wdn_ref[c * FF_CHUNK:(c + 1) * FF_CHUNK, :], preferred_element_type=F32)
    if final:
        acc = _rms(acc, fn_ref[...])
    o_ref[...] = acc


def _const_spec(shape):
    nd = len(shape)
    return pl.BlockSpec(shape, lambda *_: (0,) * nd, pipeline_mode=pl.Buffered(1))


def _mixer_call(x, sinks, n1, win, cw, cb, lng, lnb, lcw, lcb, wg, bg, lam, mixg, wout):
    B, S, D = x.shape
    T = MIX_T
    consts = (n1, win, cw, cb, lng, lnb, lcw, lcb, wg, bg, lam, mixg, wout)
    return pl.pallas_call(
        _mixer_kernel,
        out_shape=jax.ShapeDtypeStruct((B, S, D), F32),
        grid=(B, S // T),
        in_specs=[pl.BlockSpec(memory_space=pltpu.SMEM),
                  pl.BlockSpec((None, T, D), lambda b, s: (b, s, 0))]
                 + [_const_spec(c.shape) for c in consts],
        out_specs=pl.BlockSpec((None, T, D), lambda b, s: (b, s, 0)),
        scratch_shapes=[
            pltpu.VMEM((KV_WIDTH, WINDOW + T), BF16),
            pltpu.VMEM((WINDOW + T, KV_WIDTH), BF16),
            pltpu.VMEM((WINDOW + T, KV_WIDTH), BF16),
            pltpu.VMEM((CONV_HALO + T, CONV_WIDTH), F32),
            pltpu.VMEM((LRU_HALO + T, LRU_WIDTH), F32),
            pltpu.VMEM((SUBLANES, LRU_WIDTH), F32),
            pltpu.VMEM((2, WINDOW, 2 * WINDOW), F32),
            pltpu.VMEM((T, ATTN_WIDTH), F32),
            pltpu.VMEM((T, CONV_WIDTH), F32),
            pltpu.VMEM((LRU_WIDTH // LANES, SUBLANES * SCAN_PITCH, LANES), F32),
            pltpu.VMEM((LRU_WIDTH // LANES, SUBLANES * SCAN_PITCH, LANES), F32),
            pltpu.VMEM((T, LRU_WIDTH), F32),
        ],
        compiler_params=pltpu.CompilerParams(
            dimension_semantics=("arbitrary", "arbitrary"), vmem_limit_bytes=VMEM_LIMIT),
        name="mixer",
    )(sinks, x, *consts)


def _ffn_call(h, n2, wup, wdn, fn, final):
    M, D = h.shape
    T = FFN_T
    return pl.pallas_call(
        functools.partial(_ffn_kernel, final=final),
        out_shape=jax.ShapeDtypeStruct((M, D), F32),
        grid=(M // T,),
        in_specs=[pl.BlockSpec((T, D), lambda i: (i, 0)),
                  _const_spec(n2.shape), _const_spec(wup.shape), _const_spec(wdn.shape), _const_spec(fn.shape)],
        out_specs=pl.BlockSpec((T, D), lambda i: (i, 0)),
        compiler_params=pltpu.CompilerParams(
            dimension_semantics=("arbitrary",), vmem_limit_bytes=VMEM_LIMIT),
        name="ffn_final" if final else "ffn",
    )(h, n2, wup, wdn, fn)


def _block_diag(w):
    hN, di, dj = w.shape
    eye = jnp.eye(hN, dtype=w.dtype)
    return (eye[:, None, :, None] * w[:, :, None, :]).reshape(hN * di, hN * dj)


def kernel(x, norm1, w_in, attn_sinks, conv_dw_w, conv_dw_b, conv_ln_g, conv_ln_b, lru_conv_w, lru_conv_b,
           lru_wa, lru_ba, lru_wx, lru_bx, lru_lambda, mix_norm, w_out, norm2, w_up, w_down, final_norm):
    B, S, D = x.shape
    depth = w_in.shape[0]
    qscale = jnp.concatenate([jnp.full((ATTN_WIDTH,), HEAD_DIM ** -0.5, F32),
                              jnp.ones((IN_WIDTH - ATTN_WIDTH,), F32)])
    h = x
    for l in range(depth):
        win = (w_in[l] * qscale).astype(BF16)
        wg = jnp.concatenate([_block_diag(lru_wa[l]), _block_diag(lru_wx[l])], axis=1).astype(BF16)
        bg = jnp.concatenate([lru_ba[l].reshape(1, -1), lru_bx[l].reshape(1, -1)], axis=1)
        h = _mixer_call(
            h, attn_sinks[l], norm1[l][None], win, conv_dw_w[l], conv_dw_b[l][None],
            conv_ln_g[l][None], conv_ln_b[l][None], lru_conv_w[l], lru_conv_b[l][None],
            wg, bg, lru_lambda[l][None], mix_norm[l][None], w_out[l].astype(BF16))
        h = _ffn_call(h.reshape(B * S, D), norm2[l][None], w_up[l].astype(BF16), w_down[l].astype(BF16),
                      final_norm[None], final=(l == depth - 1)).reshape(B, S, D)
    return h
```

```python
import functools

import jax
import jax.numpy as jnp
from jax import lax
from jax.experimental import pallas as pl
from jax.experimental.pallas import tpu as pltpu

D_MODEL = 1024
HEAD_DIM = 64
ATTN_WIDTH = 512
N_Q_HEADS = 8
N_KV_HEADS = 2
KV_WIDTH = 128
WINDOW = 128
CONV_WIDTH = 256
CONV_KERNEL = 31
LRU_WIDTH = 256
LRU_HEADS = 4
LRU_HEAD_DIM = 64
LRU_CONV_KERNEL = 4
LRU_C = 8.0
MIX_WIDTH = 1024
IN_WIDTH = 1792
D_FF = 4096
RMS_EPS = 1e-6
LN_EPS = 1e-5
MASK_VALUE = -1e30

Q_OFF, K_OFF, V_OFF, CV_OFF, CG_OFF, RX_OFF, RG_OFF = 0, 512, 640, 768, 1024, 1280, 1536
YA_OFF, YC_OFF, YL_OFF = 0, ATTN_WIDTH, ATTN_WIDTH + CONV_WIDTH

LANES = 128
SUBLANES = 8
MXU_COLS = 256

MIX_T = 512
SUB = WINDOW
N_SUB = MIX_T // SUB
FF_CHUNK = D_FF // N_SUB
FF_PIECES = FF_CHUNK // MXU_COLS
DN_PIECES = D_MODEL // MXU_COLS
CONV_HALO = 32
LRU_HALO = 8
SCAN_LEN = SUB // SUBLANES
SCAN_PITCH = SCAN_LEN + SUBLANES
VMEM_LIMIT = 56 * 1024 * 1024

F32 = jnp.float32
BF16 = jnp.bfloat16


def _sigmoid(x):
    return 0.5 * jnp.tanh(0.5 * x) + 0.5


def _rms(x, g):
    return x * lax.rsqrt(jnp.mean(x * x, axis=-1, keepdims=True) + RMS_EPS) * g


def _layer_kernel(sinks_ref, x_ref, n1_ref, win_ref, cw_ref, cb_ref, lng_ref, lnb_ref,
                  lcw_ref, lcb_ref, wg_ref, bg_ref, lam_ref, mixg_ref, wout_ref,
                  n2_ref, wup_ref, wdn_ref, fn_ref,
                  o_ref,
                  kt_ext, v_ext, vr_ext, u_ext, xl_ext, hcar, bias_s, a_s, b_s,
                  z_s, hmid, fhn_s, fu_s,
                  *, final, blocks_per_seq):
    T = MIX_T
    i = pl.program_id(0)
    s = lax.rem(i, blocks_per_seq)

    @pl.when(i == 0)
    def _():
        hmid[...] = jnp.zeros((T, D_MODEL), F32)

    @pl.when(s == 0)
    def _():
        kt_ext[:, 0:WINDOW] = jnp.zeros((KV_WIDTH, WINDOW), BF16)
        v_ext[0:WINDOW, :] = jnp.zeros((WINDOW, KV_WIDTH), BF16)
        vr_ext[0:WINDOW, :] = jnp.zeros((WINDOW, KV_WIDTH), BF16)
        u_ext[0:CONV_HALO, :] = jnp.zeros((CONV_HALO, CONV_WIDTH), F32)
        xl_ext[0:LRU_HALO, :] = jnp.zeros((LRU_HALO, LRU_WIDTH), F32)
        hcar[...] = jnp.zeros((SUBLANES, LRU_WIDTH), F32)

    qi = lax.broadcasted_iota(jnp.int32, (WINDOW, 2 * WINDOW), 0)
    kc = lax.broadcasted_iota(jnp.int32, (WINDOW, 2 * WINDOW), 1)
    band = (kc > qi) & (kc <= qi + WINDOW)
    bias_s[0] = jnp.where(band, 0.0, MASK_VALUE).astype(F32)
    bias_s[1] = jnp.where(band & (kc >= WINDOW), 0.0, MASK_VALUE).astype(F32)

    mg = mixg_ref[...]
    lane = lax.broadcasted_iota(jnp.int32, (2 * WINDOW, KV_WIDTH), 1)
    low = lane < HEAD_DIM
    lane_o = lax.broadcasted_iota(jnp.int32, (WINDOW, LANES), 1)
    zk = jnp.zeros((HEAD_DIM, 2 * WINDOW), BF16)
    row8 = lax.broadcasted_iota(jnp.int32, (SUBLANES, LANES), 0)
    L, P = SCAN_LEN, SCAN_PITCH

    def ffn_prep():
        h = hmid[...]
        fhn_s[...] = _rms(h, n2_ref[...]).astype(BF16)
        o_ref[...] = h

    def ffn_up(c, p):
        rows = slice((p // 2) * (T // 2), (p // 2 + 1) * (T // 2))
        n0 = (p % 2) * 2 * MXU_COLS
        u = jnp.dot(fhn_s[rows, :], wup_ref[:, c * FF_CHUNK + n0:c * FF_CHUNK + n0 + 2 * MXU_COLS],
                    preferred_element_type=F32)
        fu_s[rows, n0:n0 + 2 * MXU_COLS] = jnp.square(jnp.maximum(u, 0.0)).astype(BF16)

    def ffn_down(c, p):
        rows = slice((p // 2) * (T // 2), (p // 2 + 1) * (T // 2))
        cols = slice((p % 2) * 2 * MXU_COLS, (p % 2 + 1) * 2 * MXU_COLS)
        o_ref[rows, cols] += jnp.dot(fu_s[rows, :], wdn_ref[c * FF_CHUNK:(c + 1) * FF_CHUNK, cols],
                                     preferred_element_type=F32)

    def inproj(r):
        r0 = r * SUB
        hn = _rms(x_ref[r0:r0 + SUB, :], n1_ref[...]).astype(BF16)
        z_s[r0:r0 + SUB, :] = jnp.dot(hn, win_ref[...], preferred_element_type=F32)

    def attn_qk(r, st):
        r0 = r * SUB
        q = z_s[r0:r0 + SUB, Q_OFF:Q_OFF + ATTN_WIDTH].astype(BF16)
        kf = z_s[r0:r0 + SUB, K_OFF:K_OFF + KV_WIDTH]
        vf = z_s[r0:r0 + SUB, V_OFF:V_OFF + KV_WIDTH]
        kt_ext[:, WINDOW + r0:WINDOW + r0 + SUB] = kf.T.astype(BF16)
        v_ext[WINDOW + r0:WINDOW + r0 + SUB, :] = vf.astype(BF16)
        vr_ext[WINDOW + r0:WINDOW + r0 + SUB, :] = pltpu.roll(vf, HEAD_DIM, axis=1).astype(BF16)
        kt_band = kt_ext[:, r0:r0 + 2 * WINDOW]
        st["sc"] = []
        for h in range(N_KV_HEADS):
            kh = kt_band[h * HEAD_DIM:(h + 1) * HEAD_DIM, :]
            kt2 = jnp.concatenate(
                [jnp.concatenate([kh, zk], axis=0), jnp.concatenate([zk, kh], axis=0)], axis=1)
            c0 = h * 2 * LANES
            qs = jnp.concatenate([q[:, c0:c0 + LANES], q[:, c0 + LANES:c0 + 2 * LANES]], axis=0)
            st["sc"].append(jnp.dot(qs, kt2, preferred_element_type=F32))

    def attn_softmax(r, h, st):
        bias = bias_s[jnp.where(s == 0, 1, 0)] if r == 0 else bias_s[0]
        sc = st["sc"][h]
        p_rows = []
        dens = []
        for rr in range(2):
            p_cols = []
            den_r = []
            for e in range(2):
                sink = sinks_ref[4 * h + 2 * rr + e]
                t = sc[rr * WINDOW:(rr + 1) * WINDOW, e * 2 * WINDOW:(e + 1) * 2 * WINDOW] + bias
                m = jnp.maximum(jnp.max(t, axis=-1, keepdims=True), sink)
                p = jnp.exp(t - m)
                den_r.append(jnp.sum(p, axis=-1, keepdims=True) + jnp.exp(sink - m))
                p_cols.append(p.astype(BF16))
            p_rows.append(jnp.concatenate(p_cols, axis=1))
            dens.append(den_r)
        st[("pm", h)] = jnp.concatenate(p_rows, axis=0)
        st[("den", h)] = dens

    def attn_pv(r, st):
        r0 = r * SUB
        v_band = v_ext[r0:r0 + 2 * WINDOW, :]
        vr_band = vr_ext[r0:r0 + 2 * WINDOW, :]
        st["o"] = []
        for h in range(N_KV_HEADS):
            if h == 0:
                va = jnp.where(low, v_band, jnp.zeros_like(v_band))
                vb = jnp.where(low, jnp.zeros_like(vr_band), vr_band)
            else:
                va = jnp.where(low, vr_band, jnp.zeros_like(vr_band))
                vb = jnp.where(low, jnp.zeros_like(v_band), v_band)
            v2 = jnp.concatenate([va, vb], axis=0)
            st["o"].append(jnp.dot(st[("pm", h)], v2, preferred_element_type=F32))

    def attn_out(r, st):
        r0 = r * SUB
        ya_cols = []
        for h in range(N_KV_HEADS):
            for rr in range(2):
                dens = st[("den", h)]
                den = jnp.where(lane_o < HEAD_DIM, dens[rr][0], dens[rr][1])
                ya_cols.append(st["o"][h][rr * WINDOW:(rr + 1) * WINDOW, :] / den)
        y_attn = jnp.concatenate(ya_cols, axis=1)
        ya = _rms(y_attn, mg[:, YA_OFF:YA_OFF + ATTN_WIDTH]).astype(BF16)
        st["hm"] = x_ref[r0:r0 + SUB, :] + jnp.dot(ya, wout_ref[YA_OFF:YA_OFF + ATTN_WIDTH, :],
                                                   preferred_element_type=F32)

    def conv_unit(r, lt, st):
        r0 = r * SUB
        cl = slice(lt * LANES, (lt + 1) * LANES)
        cval = z_s[r0:r0 + SUB, CV_OFF + lt * LANES:CV_OFF + (lt + 1) * LANES]
        cgate = z_s[r0:r0 + SUB, CG_OFF + lt * LANES:CG_OFF + (lt + 1) * LANES]
        u_ext[CONV_HALO + r0:CONV_HALO + r0 + SUB, cl] = cval * _sigmoid(cgate)
        base = CONV_HALO - (CONV_KERNEL - 1)
        acc = jnp.broadcast_to(cb_ref[:, cl], (SUB, LANES))
        for sh in range(SUBLANES):
            part = None
            nrows = SUB + (SUBLANES if sh else 0)
            for k in range(CONV_KERNEL):
                if (base + k) % SUBLANES != sh:
                    continue
                al = r0 + base + k - sh
                term = cw_ref[k:k + 1, cl] * u_ext[al:al + nrows, cl]
                part = term if part is None else part + term
            acc = acc + part[sh:sh + SUB, :]
        st[("yc", lt)] = acc

    def conv_post(r, st):
        uc = jnp.concatenate([st[("yc", lt)] for lt in range(CONV_WIDTH // LANES)], axis=1)
        mu = jnp.mean(uc, axis=-1, keepdims=True)
        xc_ = uc - mu
        ln = xc_ * lax.rsqrt(jnp.mean(xc_ * xc_, axis=-1, keepdims=True) + LN_EPS) * lng_ref[...] + lnb_ref[...]
        y_conv = ln * _sigmoid(ln)
        yc = _rms(y_conv, mg[:, YC_OFF:YC_OFF + CONV_WIDTH]).astype(BF16)
        st["hm"] = st["hm"] + jnp.dot(yc, wout_ref[YC_OFF:YC_OFF + CONV_WIDTH, :], preferred_element_type=F32)

    def lru_gates(r, st):
        r0 = r * SUB
        xl_ext[LRU_HALO + r0:LRU_HALO + r0 + SUB, :] = z_s[r0:r0 + SUB, RX_OFF:RX_OFF + LRU_WIDTH]
        lbase = LRU_HALO - (LRU_CONV_KERNEL - 1)
        xc = jnp.broadcast_to(lcb_ref[...], (SUB, LRU_WIDTH))
        for k in range(LRU_CONV_KERNEL):
            xc = xc + lcw_ref[k:k + 1, :] * xl_ext[r0 + lbase + k:r0 + lbase + k + SUB, :]
        st["xc"] = xc
        st["gates"] = jnp.dot(xc.astype(BF16), wg_ref[...], preferred_element_type=F32) + bg_ref[...]

    def lru_scan(r, st):
        r0 = r * SUB
        xc, gates = st["xc"], st["gates"]
        rgate = _sigmoid(gates[:, :LRU_WIDTH])
        igate = _sigmoid(gates[:, LRU_WIDTH:])
        log_a = (-LRU_C * rgate) * jax.nn.softplus(-lam_ref[...])
        a_full = jnp.exp(log_a)
        th = jnp.tanh(log_a)
        b_full = jnp.sqrt(2.0 * th / (th - 1.0)) * (igate * xc)
        yl_cols = []
        for lt in range(LRU_WIDTH // LANES):
            cl = slice(lt * LANES, (lt + 1) * LANES)
            for c in range(SUBLANES):
                a_s[lt, c * P:c * P + L, :] = a_full[c * L:(c + 1) * L, cl]
                b_s[lt, c * P:c * P + L, :] = b_full[c * L:(c + 1) * L, cl]
            hloc = jnp.zeros((SUBLANES, LANES), F32)
            cum = jnp.ones((SUBLANES, LANES), F32)
            for m in range(L):
                am = a_s[lt, pl.ds(m, SUBLANES, stride=P), :]
                bm = b_s[lt, pl.ds(m, SUBLANES, stride=P), :]
                hloc = am * hloc + bm
                cum = am * cum
                b_s[lt, pl.ds(m, SUBLANES, stride=P), :] = hloc
                a_s[lt, pl.ds(m, SUBLANES, stride=P), :] = cum
            ca, cbv = cum, hloc
            for d in (1, 2, 4):
                a_sh = jnp.where(row8 >= d, pltpu.roll(ca, d, axis=0), 1.0)
                b_sh = jnp.where(row8 >= d, pltpu.roll(cbv, d, axis=0), 0.0)
                cbv = ca * b_sh + cbv
                ca = ca * a_sh
            hprev = hcar[:, cl]
            ends = ca * hprev + cbv
            carry_in = jnp.where(row8 == 0, hprev, pltpu.roll(ends, 1, axis=0))
            hcar[:, cl] = jnp.broadcast_to(ends[SUBLANES - 1:SUBLANES, :], (SUBLANES, LANES))
            parts = []
            for c in range(SUBLANES):
                g = jnp.broadcast_to(carry_in[c:c + 1, :], (L, LANES))
                parts.append(b_s[lt, c * P:c * P + L, :] + a_s[lt, c * P:c * P + L, :] * g)
            yl_cols.append(jnp.concatenate(parts, axis=0))
        y_lru = jnp.concatenate(yl_cols, axis=1) * jax.nn.gelu(z_s[r0:r0 + SUB, RG_OFF:RG_OFF + LRU_WIDTH])
        yl = _rms(y_lru, mg[:, YL_OFF:YL_OFF + LRU_WIDTH]).astype(BF16)
        hmid[r0:r0 + SUB, :] = st["hm"] + jnp.dot(yl, wout_ref[YL_OFF:YL_OFF + LRU_WIDTH, :],
                                                  preferred_element_type=F32)

    ffn_prep()
    inproj(0)
    for r in range(N_SUB):
        st = {}
        attn_qk(r, st)
        ffn_up(r, 0)
        attn_softmax(r, 0, st)
        ffn_up(r, 1)
        attn_softmax(r, 1, st)
        attn_pv(r, st)
        ffn_up(r, 2)
        attn_out(r, st)
        ffn_up(r, 3)
        conv_unit(r, 0, st)
        ffn_down(r, 0)
        conv_unit(r, 1, st)
        ffn_down(r, 1)
        conv_post(r, st)
        if r + 1 < N_SUB:
            inproj(r + 1)
        lru_gates(r, st)
        ffn_down(r, 2)
        lru_scan(r, st)
        ffn_down(r, 3)

    if final:
        o_ref[...] = _rms(o_ref[...], fn_ref[...])

    kt_ext[:, 0:WINDOW] = kt_ext[:, T:T + WINDOW]
    v_ext[0:WINDOW, :] = v_ext[T:T + WINDOW, :]
    vr_ext[0:WINDOW, :] = vr_ext[T:T + WINDOW, :]
    u_ext[0:CONV_HALO, :] = u_ext[T:T + CONV_HALO, :]
    xl_ext[0:LRU_HALO, :] = xl_ext[T:T + LRU_HALO, :]


def _const_spec(shape):
    nd = len(shape)
    return pl.BlockSpec(shape, lambda *_: (0,) * nd, pipeline_mode=pl.Buffered(1))


def _layer_call(x, sinks, consts, seq_len, final):
    M, D = x.shape
    T = MIX_T
    nblk = M // T
    return pl.pallas_call(
        functools.partial(_layer_kernel, final=final, blocks_per_seq=seq_len // T),
        out_shape=jax.ShapeDtypeStruct((M, D), F32),
        grid=(nblk + 1,),
        in_specs=[pl.BlockSpec(memory_space=pltpu.SMEM),
                  pl.BlockSpec((T, D), lambda i: (jnp.minimum(i, nblk - 1), 0))]
                 + [_const_spec(c.shape) for c in consts],
        out_specs=pl.BlockSpec((T, D), lambda i: (jnp.maximum(i - 1, 0), 0)),
        scratch_shapes=[
            pltpu.VMEM((KV_WIDTH, WINDOW + T), BF16),
            pltpu.VMEM((WINDOW + T, KV_WIDTH), BF16),
            pltpu.VMEM((WINDOW + T, KV_WIDTH), BF16),
            pltpu.VMEM((CONV_HALO + T, CONV_WIDTH), F32),
            pltpu.VMEM((LRU_HALO + T, LRU_WIDTH), F32),
            pltpu.VMEM((SUBLANES, LRU_WIDTH), F32),
            pltpu.VMEM((2, WINDOW, 2 * WINDOW), F32),
            pltpu.VMEM((LRU_WIDTH // LANES, SUBLANES * SCAN_PITCH, LANES), F32),
            pltpu.VMEM((LRU_WIDTH // LANES, SUBLANES * SCAN_PITCH, LANES), F32),
            pltpu.VMEM((T, IN_WIDTH), F32),
            pltpu.VMEM((T, D), F32),
            pltpu.VMEM((T, D), BF16),
            pltpu.VMEM((T, FF_CHUNK), BF16),
        ],
        compiler_params=pltpu.CompilerParams(
            dimension_semantics=("arbitrary",), vmem_limit_bytes=VMEM_LIMIT),
        name="layer_final" if final else "layer",
    )(sinks, x, *consts)


def _block_diag(w):
    hN, di, dj = w.shape
    eye = jnp.eye(hN, dtype=w.dtype)
    return (eye[:, None, :, None] * w[:, :, None, :]).reshape(hN * di, hN * dj)


def kernel(x, norm1, w_in, attn_sinks, conv_dw_w, conv_dw_b, conv_ln_g, conv_ln_b, lru_conv_w, lru_conv_b,
           lru_wa, lru_ba, lru_wx, lru_bx, lru_lambda, mix_norm, w_out, norm2, w_up, w_down, final_norm):
    B, S, D = x.shape
    depth = w_in.shape[0]
    qscale = jnp.concatenate([jnp.full((ATTN_WIDTH,), HEAD_DIM ** -0.5, F32),
                              jnp.ones((IN_WIDTH - ATTN_WIDTH,), F32)])
    h = x.reshape(B * S, D)
    for l in range(depth):
        win = (w_in[l] * qscale).astype(BF16)
        wg = jnp.concatenate([_block_diag(lru_wa[l]), _block_diag(lru_wx[l])], axis=1).astype(BF16)
        bg = jnp.concatenate([lru_ba[l].reshape(1, -1), lru_bx[l].reshape(1, -1)], axis=1)
        consts = (norm1[l][None], win, conv_dw_w[l], conv_dw_b[l][None],
                  conv_ln_g[l][None], conv_ln_b[l][None], lru_conv_w[l], lru_conv_b[l][None],
                  wg, bg, lru_lambda[l][None], mix_norm[l][None], w_out[l].astype(BF16),
                  norm2[l][None], w_up[l].astype(BF16), w_down[l].astype(BF16), final_norm[None])
        h = _layer_call(h, attn_sinks[l], consts, S, final=(l == depth - 1))
    return h.reshape(B, S, D)
```

```python
import functools

import jax
import jax.numpy as jnp
from jax import lax
from jax.experimental import pallas as pl
from jax.experimental.pallas import tpu as pltpu

D_MODEL = 1024
HEAD_DIM = 64
ATTN_WIDTH = 512
N_Q_HEADS = 8
N_KV_HEADS = 2
KV_WIDTH = 128
WINDOW = 128
CONV_WIDTH = 256
CONV_KERNEL = 31
LRU_WIDTH = 256
LRU_HEADS = 4
LRU_HEAD_DIM = 64
LRU_CONV_KERNEL = 4
LRU_C = 8.0
MIX_WIDTH = 1024
IN_WIDTH = 1792
D_FF = 4096
RMS_EPS = 1e-6
LN_EPS = 1e-5
MASK_VALUE = -1e30

Q_OFF, K_OFF, V_OFF, CV_OFF, CG_OFF, RX_OFF, RG_OFF = 0, 512, 640, 768, 1024, 1280, 1536
YA_OFF, YC_OFF, YL_OFF = 0, ATTN_WIDTH, ATTN_WIDTH + CONV_WIDTH

LANES = 128
SUBLANES = 8
MXU_COLS = 256

MIX_T = 512
SUB = WINDOW
N_SUB = MIX_T // SUB
FF_CHUNK = D_FF // N_SUB
FF_PIECES = FF_CHUNK // MXU_COLS
DN_PIECES = D_MODEL // MXU_COLS
CONV_HALO = 32
LRU_HALO = 8
SCAN_LEN = SUB // SUBLANES
SCAN_PITCH = SCAN_LEN + SUBLANES
VMEM_LIMIT = 56 * 1024 * 1024

F32 = jnp.float32
BF16 = jnp.bfloat16


def _sigmoid(x):
    return 0.5 * jnp.tanh(0.5 * x) + 0.5


def _rms(x, g):
    return x * lax.rsqrt(jnp.mean(x * x, axis=-1, keepdims=True) + RMS_EPS) * g


def _layer_kernel(sinks_ref, x_ref, n1_ref, win_ref, cw_ref, cb_ref, lng_ref, lnb_ref,
                  lcw_ref, lcb_ref, wg_ref, bg_ref, lam_ref, mixg_ref, wout_ref,
                  n2_ref, wup_ref, wdn_ref, fn_ref,
                  o_ref,
                  kt_ext, v_ext, vr_ext, u_ext, xl_ext, hcar, bias_s, a_s, b_s,
                  z_s, hmid, fhn_s, fres_s, fu_s,
                  *, final, blocks_per_seq):
    T = MIX_T
    i = pl.program_id(0)
    s = lax.rem(i, blocks_per_seq)

    @pl.when(i == 0)
    def _():
        fhn_s[...] = jnp.zeros((T, D_MODEL), BF16)
        fres_s[...] = jnp.zeros((T, D_MODEL), F32)

    @pl.when(s == 0)
    def _():
        kt_ext[:, 0:WINDOW] = jnp.zeros((KV_WIDTH, WINDOW), BF16)
        v_ext[0:WINDOW, :] = jnp.zeros((WINDOW, KV_WIDTH), BF16)
        vr_ext[0:WINDOW, :] = jnp.zeros((WINDOW, KV_WIDTH), BF16)
        u_ext[0:CONV_HALO, :] = jnp.zeros((CONV_HALO, CONV_WIDTH), F32)
        xl_ext[0:LRU_HALO, :] = jnp.zeros((LRU_HALO, LRU_WIDTH), F32)
        hcar[...] = jnp.zeros((SUBLANES, LRU_WIDTH), F32)

    qi = lax.broadcasted_iota(jnp.int32, (WINDOW, 2 * WINDOW), 0)
    kc = lax.broadcasted_iota(jnp.int32, (WINDOW, 2 * WINDOW), 1)
    band = (kc > qi) & (kc <= qi + WINDOW)
    bias_s[0] = jnp.where(band, 0.0, MASK_VALUE).astype(F32)
    bias_s[1] = jnp.where(band & (kc >= WINDOW), 0.0, MASK_VALUE).astype(F32)

    mg = mixg_ref[...]
    lane = lax.broadcasted_iota(jnp.int32, (2 * WINDOW, KV_WIDTH), 1)
    low = lane < HEAD_DIM
    lane_o = lax.broadcasted_iota(jnp.int32, (WINDOW, LANES), 1)
    zk = jnp.zeros((HEAD_DIM, 2 * WINDOW), BF16)
    row8 = lax.broadcasted_iota(jnp.int32, (SUBLANES, LANES), 0)
    L, P = SCAN_LEN, SCAN_PITCH

    def ffn_prep(r):
        rows = slice(r * SUB, (r + 1) * SUB)
        h = hmid[rows, :]
        fhn_s[rows, :] = _rms(h, n2_ref[...]).astype(BF16)
        fres_s[rows, :] = h

    def ffn_up(c, p):
        rows = slice((p // 2) * (T // 2), (p // 2 + 1) * (T // 2))
        n0 = (p % 2) * 2 * MXU_COLS
        u = jnp.dot(fhn_s[rows, :], wup_ref[:, c * FF_CHUNK + n0:c * FF_CHUNK + n0 + 2 * MXU_COLS],
                    preferred_element_type=F32).astype(BF16)
        u = jnp.maximum(u, 0.0)
        fu_s[rows, n0:n0 + 2 * MXU_COLS] = u * u

    def ffn_down(c, p):
        rows = slice((p // 2) * (T // 2), (p // 2 + 1) * (T // 2))
        cols = slice((p % 2) * 2 * MXU_COLS, (p % 2 + 1) * 2 * MXU_COLS)
        d = jnp.dot(fu_s[rows, :], wdn_ref[c * FF_CHUNK:(c + 1) * FF_CHUNK, cols], preferred_element_type=F32)
        if c == 0:
            o_ref[rows, cols] = fres_s[rows, cols] + d
        else:
            o_ref[rows, cols] += d

    def inproj(r):
        r0 = r * SUB
        hn = _rms(x_ref[r0:r0 + SUB, :], n1_ref[...]).astype(BF16)
        z_s[r0:r0 + SUB, :] = jnp.dot(hn, win_ref[...], preferred_element_type=F32)

    def attn_qk(r, st):
        r0 = r * SUB
        q = z_s[r0:r0 + SUB, Q_OFF:Q_OFF + ATTN_WIDTH].astype(BF16)
        kf = z_s[r0:r0 + SUB, K_OFF:K_OFF + KV_WIDTH]
        vf = z_s[r0:r0 + SUB, V_OFF:V_OFF + KV_WIDTH]
        kt_ext[:, WINDOW + r0:WINDOW + r0 + SUB] = kf.T.astype(BF16)
        v_ext[WINDOW + r0:WINDOW + r0 + SUB, :] = vf.astype(BF16)
        vr_ext[WINDOW + r0:WINDOW + r0 + SUB, :] = pltpu.roll(vf, HEAD_DIM, axis=1).astype(BF16)
        kt_band = kt_ext[:, r0:r0 + 2 * WINDOW]
        st["sc"] = []
        for h in range(N_KV_HEADS):
            kh = kt_band[h * HEAD_DIM:(h + 1) * HEAD_DIM, :]
            kt2 = jnp.concatenate(
                [jnp.concatenate([kh, zk], axis=0), jnp.concatenate([zk, kh], axis=0)], axis=1)
            c0 = h * 2 * LANES
            qs = jnp.concatenate([q[:, c0:c0 + LANES], q[:, c0 + LANES:c0 + 2 * LANES]], axis=0)
            st["sc"].append(jnp.dot(qs, kt2, preferred_element_type=F32))

    def attn_softmax(r, h, st):
        bias = bias_s[jnp.where(s == 0, 1, 0)] if r == 0 else bias_s[0]
        sc = st["sc"][h]
        p_rows = []
        dens = []
        for rr in range(2):
            p_cols = []
            den_r = []
            for e in range(2):
                sink = sinks_ref[4 * h + 2 * rr + e]
                t = sc[rr * WINDOW:(rr + 1) * WINDOW, e * 2 * WINDOW:(e + 1) * 2 * WINDOW] + bias
                m = jnp.maximum(jnp.max(t, axis=-1, keepdims=True), sink)
                p = jnp.exp(t - m)
                den_r.append(jnp.sum(p, axis=-1, keepdims=True) + jnp.exp(sink - m))
                p_cols.append(p.astype(BF16))
            p_rows.append(jnp.concatenate(p_cols, axis=1))
            dens.append(den_r)
        st[("pm", h)] = jnp.concatenate(p_rows, axis=0)
        st[("den", h)] = dens

    def attn_pv(r, h, st):
        r0 = r * SUB
        v_band = v_ext[r0:r0 + 2 * WINDOW, :]
        vr_band = vr_ext[r0:r0 + 2 * WINDOW, :]
        if h == 0:
            va = jnp.where(low, v_band, jnp.zeros_like(v_band))
            vb = jnp.where(low, jnp.zeros_like(vr_band), vr_band)
        else:
            va = jnp.where(low, vr_band, jnp.zeros_like(vr_band))
            vb = jnp.where(low, jnp.zeros_like(v_band), v_band)
        v2 = jnp.concatenate([va, vb], axis=0)
        o = jnp.dot(st[("pm", h)], v2, preferred_element_type=F32)
        dens = st[("den", h)]
        for rr in range(2):
            den = jnp.where(lane_o < HEAD_DIM, dens[rr][0], dens[rr][1])
            st[("ya", 2 * h + rr)] = o[rr * WINDOW:(rr + 1) * WINDOW, :] / den

    def attn_out(r, st):
        rows = slice(r * SUB, (r + 1) * SUB)
        y_attn = jnp.concatenate([st[("ya", c)] for c in range(ATTN_WIDTH // LANES)], axis=1)
        ya = _rms(y_attn, mg[:, YA_OFF:YA_OFF + ATTN_WIDTH]).astype(BF16)
        hmid[rows, :] = x_ref[rows, :] + jnp.dot(ya, wout_ref[YA_OFF:YA_OFF + ATTN_WIDTH, :],
                                                 preferred_element_type=F32)

    def conv_unit(r, lt, st):
        r0 = r * SUB
        cl = slice(lt * LANES, (lt + 1) * LANES)
        cval = z_s[r0:r0 + SUB, CV_OFF + lt * LANES:CV_OFF + (lt + 1) * LANES]
        cgate = z_s[r0:r0 + SUB, CG_OFF + lt * LANES:CG_OFF + (lt + 1) * LANES]
        u_ext[CONV_HALO + r0:CONV_HALO + r0 + SUB, cl] = cval * _sigmoid(cgate)
        base = CONV_HALO - (CONV_KERNEL - 1)
        acc = jnp.broadcast_to(cb_ref[:, cl], (SUB, LANES))
        for sh in range(SUBLANES):
            part = None
            nrows = SUB + (SUBLANES if sh else 0)
            for k in range(CONV_KERNEL):
                if (base + k) % SUBLANES != sh:
                    continue
                al = r0 + base + k - sh
                term = cw_ref[k:k + 1, cl] * u_ext[al:al + nrows, cl]
                part = term if part is None else part + term
            acc = acc + part[sh:sh + SUB, :]
        st[("yc", lt)] = acc

    def conv_post(r, st):
        uc = jnp.concatenate([st[("yc", lt)] for lt in range(CONV_WIDTH // LANES)], axis=1)
        mu = jnp.mean(uc, axis=-1, keepdims=True)
        xc_ = uc - mu
        ln = xc_ * lax.rsqrt(jnp.mean(xc_ * xc_, axis=-1, keepdims=True) + LN_EPS) * lng_ref[...] + lnb_ref[...]
        y_conv = ln * _sigmoid(ln)
        yc = _rms(y_conv, mg[:, YC_OFF:YC_OFF + CONV_WIDTH]).astype(BF16)
        hmid[r * SUB:(r + 1) * SUB, :] += jnp.dot(yc, wout_ref[YC_OFF:YC_OFF + CONV_WIDTH, :],
                                                  preferred_element_type=F32)

    def lru_gates(r, st):
        r0 = r * SUB
        xl_ext[LRU_HALO + r0:LRU_HALO + r0 + SUB, :] = z_s[r0:r0 + SUB, RX_OFF:RX_OFF + LRU_WIDTH]
        lbase = LRU_HALO - (LRU_CONV_KERNEL - 1)
        xc = jnp.broadcast_to(lcb_ref[...], (SUB, LRU_WIDTH))
        for k in range(LRU_CONV_KERNEL):
            xc = xc + lcw_ref[k:k + 1, :] * xl_ext[r0 + lbase + k:r0 + lbase + k + SUB, :]
        st["xc"] = xc
        st["gates"] = jnp.dot(xc.astype(BF16), wg_ref[...], preferred_element_type=F32) + bg_ref[...]

    def lru_scan(r, st):
        r0 = r * SUB
        xc, gates = st["xc"], st["gates"]
        rgate = _sigmoid(gates[:, :LRU_WIDTH])
        igate = _sigmoid(gates[:, LRU_WIDTH:])
        log_a = (-LRU_C * rgate) * jax.nn.softplus(-lam_ref[...])
        a_full = jnp.exp(log_a)
        th = jnp.tanh(log_a)
        b_full = jnp.sqrt(2.0 * th / (th - 1.0)) * (igate * xc)
        yl_cols = []
        for lt in range(LRU_WIDTH // LANES):
            cl = slice(lt * LANES, (lt + 1) * LANES)
            for c in range(SUBLANES):
                a_s[lt, c * P:c * P + L, :] = a_full[c * L:(c + 1) * L, cl]
                b_s[lt, c * P:c * P + L, :] = b_full[c * L:(c + 1) * L, cl]
            hloc = jnp.zeros((SUBLANES, LANES), F32)
            cum = jnp.ones((SUBLANES, LANES), F32)
            for m in range(L):
                am = a_s[lt, pl.ds(m, SUBLANES, stride=P), :]
                bm = b_s[lt, pl.ds(m, SUBLANES, stride=P), :]
                hloc = am * hloc + bm
                cum = am * cum
                b_s[lt, pl.ds(m, SUBLANES, stride=P), :] = hloc
                a_s[lt, pl.ds(m, SUBLANES, stride=P), :] = cum
            ca, cbv = cum, hloc
            for d in (1, 2, 4):
                a_sh = jnp.where(row8 >= d, pltpu.roll(ca, d, axis=0), 1.0)
                b_sh = jnp.where(row8 >= d, pltpu.roll(cbv, d, axis=0), 0.0)
                cbv = ca * b_sh + cbv
                ca = ca * a_sh
            hprev = hcar[:, cl]
            ends = ca * hprev + cbv
            carry_in = jnp.where(row8 == 0, hprev, pltpu.roll(ends, 1, axis=0))
            hcar[:, cl] = jnp.broadcast_to(ends[SUBLANES - 1:SUBLANES, :], (SUBLANES, LANES))
            parts = []
            for c in range(SUBLANES):
                g = jnp.broadcast_to(carry_in[c:c + 1, :], (L, LANES))
                parts.append(b_s[lt, c * P:c * P + L, :] + a_s[lt, c * P:c * P + L, :] * g)
            yl_cols.append(jnp.concatenate(parts, axis=0))
        y_lru = jnp.concatenate(yl_cols, axis=1) * jax.nn.gelu(z_s[r0:r0 + SUB, RG_OFF:RG_OFF + LRU_WIDTH])
        yl = _rms(y_lru, mg[:, YL_OFF:YL_OFF + LRU_WIDTH]).astype(BF16)
        hmid[r0:r0 + SUB, :] += jnp.dot(yl, wout_ref[YL_OFF:YL_OFF + LRU_WIDTH, :], preferred_element_type=F32)

    inproj(0)
    for r in range(N_SUB):
        st = {}
        last = r == N_SUB - 1
        attn_qk(r, st)
        ffn_up(r, 0)
        attn_softmax(r, 0, st)
        attn_pv(r, 0, st)
        ffn_up(r, 1)
        attn_softmax(r, 1, st)
        attn_pv(r, 1, st)
        ffn_up(r, 2)
        attn_out(r, st)
        ffn_up(r, 3)
        conv_unit(r, 0, st)
        ffn_down(r, 0)
        if last:
            ffn_prep(0)
        conv_unit(r, 1, st)
        ffn_down(r, 1)
        if last:
            ffn_prep(1)
        conv_post(r, st)
        if not last:
            inproj(r + 1)
        lru_gates(r, st)
        ffn_down(r, 2)
        if last:
            ffn_prep(2)
        lru_scan(r, st)
        ffn_down(r, 3)
    ffn_prep(N_SUB - 1)

    if final:
        o_ref[...] = _rms(o_ref[...], fn_ref[...])

    kt_ext[:, 0:WINDOW] = kt_ext[:, T:T + WINDOW]
    v_ext[0:WINDOW, :] = v_ext[T:T + WINDOW, :]
    vr_ext[0:WINDOW, :] = vr_ext[T:T + WINDOW, :]
    u_ext[0:CONV_HALO, :] = u_ext[T:T + CONV_HALO, :]
    xl_ext[0:LRU_HALO, :] = xl_ext[T:T + LRU_HALO, :]


def _const_spec(shape):
    nd = len(shape)
    return pl.BlockSpec(shape, lambda *_: (0,) * nd, pipeline_mode=pl.Buffered(1))


def _layer_call(x, sinks, consts, seq_len, final):
    M, D = x.shape
    T = MIX_T
    nblk = M // T
    return pl.pallas_call(
        functools.partial(_layer_kernel, final=final, blocks_per_seq=seq_len // T),
        out_shape=jax.ShapeDtypeStruct((M, D), F32),
        grid=(nblk + 1,),
        in_specs=[pl.BlockSpec(memory_space=pltpu.SMEM),
                  pl.BlockSpec((T, D), lambda i: (jnp.minimum(i, nblk - 1), 0))]
                 + [_const_spec(c.shape) for c in consts],
        out_specs=pl.BlockSpec((T, D), lambda i: (jnp.maximum(i - 1, 0), 0)),
        scratch_shapes=[
            pltpu.VMEM((KV_WIDTH, WINDOW + T), BF16),
            pltpu.VMEM((WINDOW + T, KV_WIDTH), BF16),
            pltpu.VMEM((WINDOW + T, KV_WIDTH), BF16),
            pltpu.VMEM((CONV_HALO + T, CONV_WIDTH), F32),
            pltpu.VMEM((LRU_HALO + T, LRU_WIDTH), F32),
            pltpu.VMEM((SUBLANES, LRU_WIDTH), F32),
            pltpu.VMEM((2, WINDOW, 2 * WINDOW), F32),
            pltpu.VMEM((LRU_WIDTH // LANES, SUBLANES * SCAN_PITCH, LANES), F32),
            pltpu.VMEM((LRU_WIDTH // LANES, SUBLANES * SCAN_PITCH, LANES), F32),
            pltpu.VMEM((T, IN_WIDTH), F32),
            pltpu.VMEM((T, D), F32),
            pltpu.VMEM((T, D), BF16),
            pltpu.VMEM((T, D), F32),
            pltpu.VMEM((T, FF_CHUNK), BF16),
        ],
        compiler_params=pltpu.CompilerParams(
            dimension_semantics=("arbitrary",), vmem_limit_bytes=VMEM_LIMIT),
        name="layer_final" if final else "layer",
    )(sinks, x, *consts)


def _block_diag(w):
    hN, di, dj = w.shape
    eye = jnp.eye(hN, dtype=w.dtype)
    return (eye[:, None, :, None] * w[:, :, None, :]).reshape(hN * di, hN * dj)


def kernel(x, norm1, w_in, attn_sinks, conv_dw_w, conv_dw_b, conv_ln_g, conv_ln_b, lru_conv_w, lru_conv_b,
           lru_wa, lru_ba, lru_wx, lru_bx, lru_lambda, mix_norm, w_out, norm2, w_up, w_down, final_norm):
    B, S, D = x.shape
    depth = w_in.shape[0]
    qscale = jnp.concatenate([jnp.full((ATTN_WIDTH,), HEAD_DIM ** -0.5, F32),
                              jnp.ones((IN_WIDTH - ATTN_WIDTH,), F32)])
    h = x.reshape(B * S, D)
    for l in range(depth):
        win = (w_in[l] * qscale).astype(BF16)
        wg = jnp.concatenate([_block_diag(lru_wa[l]), _block_diag(lru_wx[l])], axis=1).astype(BF16)
        bg = jnp.concatenate([lru_ba[l].reshape(1, -1), lru_bx[l].reshape(1, -1)], axis=1)
        consts = (norm1[l][None], win, conv_dw_w[l], conv_dw_b[l][None],
                  conv_ln_g[l][None], conv_ln_b[l][None], lru_conv_w[l], lru_conv_b[l][None],
                  wg, bg, lru_lambda[l][None], mix_norm[l][None], w_out[l].astype(BF16),
                  norm2[l][None], w_up[l].astype(BF16), w_down[l].astype(BF16), final_norm[None])
        h = _layer_call(h, attn_sinks[l], consts, S, final=(l == depth - 1))
    return h.reshape(B, S, D)
```

```python
import functools

import jax
import jax.numpy as jnp
from jax import lax
from jax.experimental import pallas as pl
from jax.experimental.pallas import tpu as pltpu

D_MODEL = 1024
HEAD_DIM = 64
ATTN_WIDTH = 512
N_Q_HEADS = 8
N_KV_HEADS = 2
KV_WIDTH = 128
WINDOW = 128
CONV_WIDTH = 256
CONV_KERNEL = 31
LRU_WIDTH = 256
LRU_HEADS = 4
LRU_HEAD_DIM = 64
LRU_CONV_KERNEL = 4
LRU_C = 8.0
MIX_WIDTH = 1024
IN_WIDTH = 1792
D_FF = 4096
RMS_EPS = 1e-6
LN_EPS = 1e-5
MASK_VALUE = -1e30

Q_OFF, K_OFF, V_OFF, CV_OFF, CG_OFF, RX_OFF, RG_OFF = 0, 512, 640, 768, 1024, 1280, 1536
YA_OFF, YC_OFF, YL_OFF = 0, ATTN_WIDTH, ATTN_WIDTH + CONV_WIDTH

LANES = 128
SUBLANES = 8
MXU_COLS = 256

MIX_T = 512
SUB = WINDOW
N_SUB = MIX_T // SUB
FF_CHUNK = D_FF // N_SUB
FF_PIECES = FF_CHUNK // MXU_COLS
DN_PIECES = D_MODEL // MXU_COLS
CONV_HALO = 32
LRU_HALO = 8
SCAN_LEN = SUB // SUBLANES
SCAN_PITCH = SCAN_LEN + SUBLANES
VMEM_LIMIT = 56 * 1024 * 1024

F32 = jnp.float32
BF16 = jnp.bfloat16


def _sigmoid(x):
    return 0.5 * jnp.tanh(0.5 * x) + 0.5


def _rms(x, g):
    return x * lax.rsqrt(jnp.mean(x * x, axis=-1, keepdims=True) + RMS_EPS) * g


def _layer_kernel(sinks_ref, x_ref, n1_ref, win_ref, cw_ref, cb_ref, lng_ref, lnb_ref,
                  lcw_ref, lcb_ref, wg_ref, bg_ref, lam_ref, mixg_ref, wout_ref,
                  n2_ref, wup_ref, wdn_ref, fn_ref,
                  o_ref,
                  kt_ext, v_ext, vr_ext, u_ext, xl_ext, hcar, bias_s, a_s, b_s,
                  z_s, hmid, fhn_s, fres_s, fu_s,
                  *, layer, final, blocks_per_seq):
    T = MIX_T
    i = pl.program_id(0)
    s = lax.rem(i, blocks_per_seq)

    @pl.when(i == 0)
    def _():
        fhn_s[...] = jnp.zeros((T, D_MODEL), BF16)
        fres_s[...] = jnp.zeros((T, D_MODEL), F32)

    @pl.when(s == 0)
    def _():
        kt_ext[:, 0:WINDOW] = jnp.zeros((KV_WIDTH, WINDOW), BF16)
        v_ext[0:WINDOW, :] = jnp.zeros((WINDOW, KV_WIDTH), BF16)
        vr_ext[0:WINDOW, :] = jnp.zeros((WINDOW, KV_WIDTH), BF16)
        u_ext[0:CONV_HALO, :] = jnp.zeros((CONV_HALO, CONV_WIDTH), F32)
        xl_ext[0:LRU_HALO, :] = jnp.zeros((LRU_HALO, LRU_WIDTH), F32)
        hcar[...] = jnp.zeros((SUBLANES, LRU_WIDTH), F32)

    qi = lax.broadcasted_iota(jnp.int32, (WINDOW, 2 * WINDOW), 0)
    kc = lax.broadcasted_iota(jnp.int32, (WINDOW, 2 * WINDOW), 1)
    band = (kc > qi) & (kc <= qi + WINDOW)
    bias_s[0] = jnp.where(band, 0.0, MASK_VALUE).astype(F32)
    bias_s[1] = jnp.where(band & (kc >= WINDOW), 0.0, MASK_VALUE).astype(F32)

    mg = mixg_ref[...]
    lane = lax.broadcasted_iota(jnp.int32, (2 * WINDOW, KV_WIDTH), 1)
    band_row = lax.broadcasted_iota(jnp.int32, (2 * WINDOW, KV_WIDTH), 0)
    keep_lo = (lane < HEAD_DIM) & (band_row != 0)
    drop_hi = (lane < HEAD_DIM) | (band_row == 0)
    lane_o = lax.broadcasted_iota(jnp.int32, (WINDOW, LANES), 1)
    zk = jnp.zeros((HEAD_DIM, 2 * WINDOW), BF16)
    row8 = lax.broadcasted_iota(jnp.int32, (SUBLANES, LANES), 0)
    L, P = SCAN_LEN, SCAN_PITCH

    def ffn_prep(r):
        rows = slice(r * SUB, (r + 1) * SUB)
        h = hmid[rows, :]
        fhn_s[rows, :] = _rms(h, n2_ref[...]).astype(BF16)
        fres_s[rows, :] = h

    def ffn_up(c, p):
        rows = slice((p // 2) * (T // 2), (p // 2 + 1) * (T // 2))
        n0 = (p % 2) * 2 * MXU_COLS
        u = jnp.dot(fhn_s[rows, :], wup_ref[:, c * FF_CHUNK + n0:c * FF_CHUNK + n0 + 2 * MXU_COLS],
                    preferred_element_type=F32).astype(BF16)
        u = jnp.maximum(u, 0.0)
        fu_s[rows, n0:n0 + 2 * MXU_COLS] = u * u

    def ffn_down(c, p):
        rows = slice((p // 2) * (T // 2), (p // 2 + 1) * (T // 2))
        cols = slice((p % 2) * 2 * MXU_COLS, (p % 2 + 1) * 2 * MXU_COLS)
        d = jnp.dot(fu_s[rows, :], wdn_ref[c * FF_CHUNK:(c + 1) * FF_CHUNK, cols], preferred_element_type=F32)
        if c == 0:
            o_ref[rows, cols] = fres_s[rows, cols] + d
        else:
            o_ref[rows, cols] += d

    def ffn_finish(half):
        if final:
            rows = slice(half * (T // 2), (half + 1) * (T // 2))
            o_ref[rows, :] = _rms(o_ref[rows, :], fn_ref[...])

    def inproj(r):
        r0 = r * SUB
        hn = _rms(x_ref[r0:r0 + SUB, :], n1_ref[...]).astype(BF16)
        z_s[r0:r0 + SUB, :] = jnp.dot(hn, win_ref[...], preferred_element_type=F32)

    def attn_qk(r, st):
        r0 = r * SUB
        q = z_s[r0:r0 + SUB, Q_OFF:Q_OFF + ATTN_WIDTH].astype(BF16)
        kf = z_s[r0:r0 + SUB, K_OFF:K_OFF + KV_WIDTH]
        vf = z_s[r0:r0 + SUB, V_OFF:V_OFF + KV_WIDTH]
        kt_ext[:, WINDOW + r0:WINDOW + r0 + SUB] = kf.T.astype(BF16)
        v_ext[WINDOW + r0:WINDOW + r0 + SUB, :] = vf.astype(BF16)
        vr_ext[WINDOW + r0:WINDOW + r0 + SUB, :] = pltpu.roll(vf, HEAD_DIM, axis=1).astype(BF16)
        kt_band = kt_ext[:, r0:r0 + 2 * WINDOW]
        st["sc"] = []
        for h in range(N_KV_HEADS):
            kh = kt_band[h * HEAD_DIM:(h + 1) * HEAD_DIM, :]
            kt2 = jnp.concatenate(
                [jnp.concatenate([kh, zk], axis=0), jnp.concatenate([zk, kh], axis=0)], axis=1)
            c0 = h * 2 * LANES
            qs = jnp.concatenate([q[:, c0:c0 + LANES], q[:, c0 + LANES:c0 + 2 * LANES]], axis=0)
            st["sc"].append(jnp.dot(qs, kt2, preferred_element_type=F32))

    def attn_softmax(r, h, st):
        bias = bias_s[jnp.where(s == 0, 1, 0)] if r == 0 else bias_s[0]
        sc = st["sc"][h]
        p_rows = []
        dens = []
        for rr in range(2):
            p_cols = []
            den_r = []
            for e in range(2):
                sink = sinks_ref[layer, 4 * h + 2 * rr + e]
                rws = slice(rr * WINDOW, (rr + 1) * WINDOW)
                c0 = e * 2 * WINDOW
                t = jnp.concatenate(
                    [jnp.where(lane_o == 0, sink, sc[rws, c0:c0 + LANES] + bias[:, :LANES]),
                     sc[rws, c0 + LANES:c0 + 2 * LANES] + bias[:, LANES:]], axis=1)
                p = jnp.exp(t - jnp.max(t, axis=-1, keepdims=True))
                den_r.append(jnp.sum(p, axis=-1, keepdims=True))
                p_cols.append(p.astype(BF16))
            p_rows.append(jnp.concatenate(p_cols, axis=1))
            dens.append(den_r)
        st[("pm", h)] = jnp.concatenate(p_rows, axis=0)
        st[("den", h)] = dens

    def attn_pv(r, h, st):
        r0 = r * SUB
        v_band = v_ext[r0:r0 + 2 * WINDOW, :]
        vr_band = vr_ext[r0:r0 + 2 * WINDOW, :]
        if h == 0:
            va = jnp.where(keep_lo, v_band, jnp.zeros_like(v_band))
            vb = jnp.where(drop_hi, jnp.zeros_like(vr_band), vr_band)
        else:
            va = jnp.where(keep_lo, vr_band, jnp.zeros_like(vr_band))
            vb = jnp.where(drop_hi, jnp.zeros_like(v_band), v_band)
        v2 = jnp.concatenate([va, vb], axis=0)
        o = jnp.dot(st[("pm", h)], v2, preferred_element_type=F32)
        dens = st[("den", h)]
        for rr in range(2):
            den = jnp.where(lane_o < HEAD_DIM, dens[rr][0], dens[rr][1])
            st[("ya", 2 * h + rr)] = o[rr * WINDOW:(rr + 1) * WINDOW, :] / den

    def attn_out(r, st):
        rows = slice(r * SUB, (r + 1) * SUB)
        y_attn = jnp.concatenate([st[("ya", c)] for c in range(ATTN_WIDTH // LANES)], axis=1)
        ya = _rms(y_attn, mg[:, YA_OFF:YA_OFF + ATTN_WIDTH]).astype(BF16)
        hmid[rows, :] = x_ref[rows, :] + jnp.dot(ya, wout_ref[YA_OFF:YA_OFF + ATTN_WIDTH, :],
                                                 preferred_element_type=F32)

    def conv_unit(r, lt, st):
        r0 = r * SUB
        cl = slice(lt * LANES, (lt + 1) * LANES)
        cval = z_s[r0:r0 + SUB, CV_OFF + lt * LANES:CV_OFF + (lt + 1) * LANES]
        cgate = z_s[r0:r0 + SUB, CG_OFF + lt * LANES:CG_OFF + (lt + 1) * LANES]
        u_ext[CONV_HALO + r0:CONV_HALO + r0 + SUB, cl] = cval * _sigmoid(cgate)
        base = CONV_HALO - (CONV_KERNEL - 1)
        acc = jnp.broadcast_to(cb_ref[:, cl], (SUB, LANES))
        for sh in range(SUBLANES):
            part = None
            nrows = SUB + (SUBLANES if sh else 0)
            for k in range(CONV_KERNEL):
                if (base + k) % SUBLANES != sh:
                    continue
                al = r0 + base + k - sh
                term = cw_ref[k:k + 1, cl] * u_ext[al:al + nrows, cl]
                part = term if part is None else part + term
            acc = acc + part[sh:sh + SUB, :]
        st[("yc", lt)] = acc

    def conv_post(r, st):
        uc = jnp.concatenate([st[("yc", lt)] for lt in range(CONV_WIDTH // LANES)], axis=1)
        mu = jnp.mean(uc, axis=-1, keepdims=True)
        xc_ = uc - mu
        ln = xc_ * lax.rsqrt(jnp.mean(xc_ * xc_, axis=-1, keepdims=True) + LN_EPS) * lng_ref[...] + lnb_ref[...]
        y_conv = ln * _sigmoid(ln)
        yc = _rms(y_conv, mg[:, YC_OFF:YC_OFF + CONV_WIDTH]).astype(BF16)
        hmid[r * SUB:(r + 1) * SUB, :] += jnp.dot(yc, wout_ref[YC_OFF:YC_OFF + CONV_WIDTH, :],
                                                  preferred_element_type=F32)

    def lru_gates(r, st):
        r0 = r * SUB
        xl_ext[LRU_HALO + r0:LRU_HALO + r0 + SUB, :] = z_s[r0:r0 + SUB, RX_OFF:RX_OFF + LRU_WIDTH]
        lbase = LRU_HALO - (LRU_CONV_KERNEL - 1)
        xc = jnp.broadcast_to(lcb_ref[...], (SUB, LRU_WIDTH))
        for k in range(LRU_CONV_KERNEL):
            xc = xc + lcw_ref[k:k + 1, :] * xl_ext[r0 + lbase + k:r0 + lbase + k + SUB, :]
        st["xc"] = xc
        st["gates"] = jnp.dot(xc.astype(BF16), wg_ref[...], preferred_element_type=F32) + bg_ref[...]

    def lru_scan(r, st):
        r0 = r * SUB
        xc, gates = st["xc"], st["gates"]
        rgate = _sigmoid(gates[:, :LRU_WIDTH])
        igate = _sigmoid(gates[:, LRU_WIDTH:])
        log_a = (-LRU_C * rgate) * jax.nn.softplus(-lam_ref[...])
        a_full = jnp.exp(log_a)
        th = jnp.tanh(log_a)
        b_full = jnp.sqrt(2.0 * th / (th - 1.0)) * (igate * xc)
        yl_cols = []
        for lt in range(LRU_WIDTH // LANES):
            cl = slice(lt * LANES, (lt + 1) * LANES)
            for c in range(SUBLANES):
                a_s[lt, c * P:c * P + L, :] = a_full[c * L:(c + 1) * L, cl]
                b_s[lt, c * P:c * P + L, :] = b_full[c * L:(c + 1) * L, cl]
            hloc = jnp.zeros((SUBLANES, LANES), F32)
            cum = jnp.ones((SUBLANES, LANES), F32)
            for m in range(L):
                am = a_s[lt, pl.ds(m, SUBLANES, stride=P), :]
                bm = b_s[lt, pl.ds(m, SUBLANES, stride=P), :]
                hloc = am * hloc + bm
                cum = am * cum
                b_s[lt, pl.ds(m, SUBLANES, stride=P), :] = hloc
                a_s[lt, pl.ds(m, SUBLANES, stride=P), :] = cum
            ca, cbv = cum, hloc
            for d in (1, 2, 4):
                a_sh = jnp.where(row8 >= d, pltpu.roll(ca, d, axis=0), 1.0)
                b_sh = jnp.where(row8 >= d, pltpu.roll(cbv, d, axis=0), 0.0)
                cbv = ca * b_sh + cbv
                ca = ca * a_sh
            hprev = hcar[:, cl]
            ends = ca * hprev + cbv
            carry_in = jnp.where(row8 == 0, hprev, pltpu.roll(ends, 1, axis=0))
            hcar[:, cl] = jnp.broadcast_to(ends[SUBLANES - 1:SUBLANES, :], (SUBLANES, LANES))
            parts = []
            for c in range(SUBLANES):
                g = jnp.broadcast_to(carry_in[c:c + 1, :], (L, LANES))
                parts.append(b_s[lt, c * P:c * P + L, :] + a_s[lt, c * P:c * P + L, :] * g)
            yl_cols.append(jnp.concatenate(parts, axis=0))
        y_lru = jnp.concatenate(yl_cols, axis=1) * jax.nn.gelu(z_s[r0:r0 + SUB, RG_OFF:RG_OFF + LRU_WIDTH])
        yl = _rms(y_lru, mg[:, YL_OFF:YL_OFF + LRU_WIDTH]).astype(BF16)
        hmid[r0:r0 + SUB, :] += jnp.dot(yl, wout_ref[YL_OFF:YL_OFF + LRU_WIDTH, :], preferred_element_type=F32)

    inproj(0)
    for r in range(N_SUB):
        st = {}
        last = r == N_SUB - 1
        attn_qk(r, st)
        ffn_up(r, 0)
        attn_softmax(r, 0, st)
        attn_pv(r, 0, st)
        ffn_up(r, 1)
        attn_softmax(r, 1, st)
        attn_pv(r, 1, st)
        ffn_up(r, 2)
        attn_out(r, st)
        ffn_up(r, 3)
        conv_unit(r, 0, st)
        ffn_down(r, 0)
        if last:
            ffn_prep(0)
        conv_unit(r, 1, st)
        ffn_down(r, 1)
        if last:
            ffn_prep(1)
            ffn_finish(0)
        conv_post(r, st)
        if not last:
            inproj(r + 1)
        lru_gates(r, st)
        ffn_down(r, 2)
        if last:
            ffn_prep(2)
        lru_scan(r, st)
        ffn_down(r, 3)
    ffn_prep(N_SUB - 1)
    ffn_finish(1)

    kt_ext[:, 0:WINDOW] = kt_ext[:, T:T + WINDOW]
    v_ext[0:WINDOW, :] = v_ext[T:T + WINDOW, :]
    vr_ext[0:WINDOW, :] = vr_ext[T:T + WINDOW, :]
    u_ext[0:CONV_HALO, :] = u_ext[T:T + CONV_HALO, :]
    xl_ext[0:LRU_HALO, :] = xl_ext[T:T + LRU_HALO, :]


def _layer_spec(arr, layer):
    nd = arr.ndim
    return pl.BlockSpec((None,) + arr.shape[1:], lambda i: (layer,) + (0,) * (nd - 1),
                        pipeline_mode=pl.Buffered(1))


def _layer_call(x, sinks, consts, seq_len, layer, final):
    M, D = x.shape
    T = MIX_T
    nblk = M // T
    return pl.pallas_call(
        functools.partial(_layer_kernel, layer=layer, final=final, blocks_per_seq=seq_len // T),
        out_shape=jax.ShapeDtypeStruct((M, D), F32),
        grid=(nblk + 1,),
        in_specs=[pl.BlockSpec(memory_space=pltpu.SMEM),
                  pl.BlockSpec((T, D), lambda i: (jnp.minimum(i, nblk - 1), 0))]
                 + [_layer_spec(c, layer) for c in consts],
        out_specs=pl.BlockSpec((T, D), lambda i: (jnp.maximum(i - 1, 0), 0)),
        scratch_shapes=[
            pltpu.VMEM((KV_WIDTH, WINDOW + T), BF16),
            pltpu.VMEM((WINDOW + T, KV_WIDTH), BF16),
            pltpu.VMEM((WINDOW + T, KV_WIDTH), BF16),
            pltpu.VMEM((CONV_HALO + T, CONV_WIDTH), F32),
            pltpu.VMEM((LRU_HALO + T, LRU_WIDTH), F32),
            pltpu.VMEM((SUBLANES, LRU_WIDTH), F32),
            pltpu.VMEM((2, WINDOW, 2 * WINDOW), F32),
            pltpu.VMEM((LRU_WIDTH // LANES, SUBLANES * SCAN_PITCH, LANES), F32),
            pltpu.VMEM((LRU_WIDTH // LANES, SUBLANES * SCAN_PITCH, LANES), F32),
            pltpu.VMEM((T, IN_WIDTH), F32),
            pltpu.VMEM((T, D), F32),
            pltpu.VMEM((T, D), BF16),
            pltpu.VMEM((T, D), F32),
            pltpu.VMEM((T, FF_CHUNK), BF16),
        ],
        compiler_params=pltpu.CompilerParams(
            dimension_semantics=("arbitrary",), vmem_limit_bytes=VMEM_LIMIT),
        name="layer_final" if final else "layer",
    )(sinks, x, *consts)


def _block_diag(w):
    dp, hN, di, dj = w.shape
    eye = jnp.eye(hN, dtype=w.dtype)
    return (eye[None, :, None, :, None] * w[:, :, :, None, :]).reshape(dp, hN * di, hN * dj)


def kernel(x, norm1, w_in, attn_sinks, conv_dw_w, conv_dw_b, conv_ln_g, conv_ln_b, lru_conv_w, lru_conv_b,
           lru_wa, lru_ba, lru_wx, lru_bx, lru_lambda, mix_norm, w_out, norm2, w_up, w_down, final_norm):
    B, S, D = x.shape
    depth = w_in.shape[0]
    qscale = jnp.concatenate([jnp.full((ATTN_WIDTH,), HEAD_DIM ** -0.5, F32),
                              jnp.ones((IN_WIDTH - ATTN_WIDTH,), F32)])
    win = (w_in * qscale).astype(BF16)
    wg = jnp.concatenate([_block_diag(lru_wa), _block_diag(lru_wx)], axis=2).astype(BF16)
    bg = jnp.concatenate([lru_ba.reshape(depth, 1, -1), lru_bx.reshape(depth, 1, -1)], axis=2)
    consts = (norm1[:, None], win, conv_dw_w, conv_dw_b[:, None], conv_ln_g[:, None], conv_ln_b[:, None],
              lru_conv_w, lru_conv_b[:, None], wg, bg, lru_lambda[:, None], mix_norm[:, None],
              w_out.astype(BF16), norm2[:, None], w_up.astype(BF16), w_down.astype(BF16),
              jnp.broadcast_to(final_norm[None, None], (depth, 1, D)))
    h = x.reshape(B * S, D)
    for l in range(depth):
        h = _layer_call(h, attn_sinks, consts, S, layer=l, final=(l == depth - 1))
    return h.reshape(B, S, D)
```

```python
import functools

import jax
import jax.numpy as jnp
from jax import lax
from jax.experimental import pallas as pl
from jax.experimental.pallas import tpu as pltpu

D_MODEL = 1024
HEAD_DIM = 64
ATTN_WIDTH = 512
N_Q_HEADS = 8
N_KV_HEADS = 2
KV_WIDTH = 128
WINDOW = 128
CONV_WIDTH = 256
CONV_KERNEL = 31
LRU_WIDTH = 256
LRU_HEADS = 4
LRU_HEAD_DIM = 64
LRU_CONV_KERNEL = 4
LRU_C = 8.0
MIX_WIDTH = 1024
IN_WIDTH = 1792
D_FF = 4096
RMS_EPS = 1e-6
LN_EPS = 1e-5
MASK_VALUE = -1e30

Q_OFF, K_OFF, V_OFF, CV_OFF, CG_OFF, RX_OFF, RG_OFF = 0, 512, 640, 768, 1024, 1280, 1536
YA_OFF, YC_OFF, YL_OFF = 0, ATTN_WIDTH, ATTN_WIDTH + CONV_WIDTH

VEC_NAMES = ("norm1", "conv_b", "ln_g", "ln_b", "lru_conv_b", "gate_b", "lambda", "mix_norm", "norm2", "final_norm")
VEC_WIDTHS = dict(zip(VEC_NAMES, (D_MODEL, CONV_WIDTH, CONV_WIDTH, CONV_WIDTH, LRU_WIDTH, 2 * LRU_WIDTH,
                                  LRU_WIDTH, MIX_WIDTH, D_MODEL, D_MODEL)))
VEC_OFF = {name: sum(VEC_WIDTHS[n] for n in VEC_NAMES[:k]) for k, name in enumerate(VEC_NAMES)}

LANES = 128
SUBLANES = 8
MXU_COLS = 256

MIX_T = 512
SUB = WINDOW
N_SUB = MIX_T // SUB
FF_CHUNK = D_FF // N_SUB
FF_PIECES = FF_CHUNK // MXU_COLS
DN_PIECES = D_MODEL // MXU_COLS
CONV_HALO = 32
LRU_HALO = 8
SCAN_LEN = SUB // SUBLANES
SCAN_PITCH = SCAN_LEN + SUBLANES
VMEM_LIMIT = 56 * 1024 * 1024

F32 = jnp.float32
BF16 = jnp.bfloat16


def _sigmoid(x):
    return 0.5 * jnp.tanh(0.5 * x) + 0.5


def _rms(x, g):
    return x * lax.rsqrt(jnp.mean(x * x, axis=-1, keepdims=True) + RMS_EPS) * g


class _Vec:
    def __init__(self, ref, name):
        self.ref, self.off, self.width = ref, VEC_OFF[name], VEC_WIDTHS[name]

    def __getitem__(self, idx):
        if idx is Ellipsis:
            return self.ref[:, self.off:self.off + self.width]
        rows, cols = idx
        return self.ref[rows, self.off + cols.start:self.off + cols.stop]


def _layer_kernel(sinks_ref, x_ref, vec_ref, win_ref, cw_ref, lcw_ref, wg_ref, wout_ref, wup_ref, wdn_ref,
                  o_ref,
                  kt_ext, v_ext, vr_ext, u_ext, xl_ext, hcar, bias_s, a_s, b_s,
                  z_s, hmid, fhn_s, fres_s, fu_s,
                  *, layer, final, blocks_per_seq):
    T = MIX_T
    i = pl.program_id(0)
    s = lax.rem(i, blocks_per_seq)
    n1_ref, cb_ref, lng_ref, lnb_ref, lcb_ref, bg_ref, lam_ref, mixg_ref, n2_ref, fn_ref = (
        _Vec(vec_ref, name) for name in VEC_NAMES)

    @pl.when(i == 0)
    def _():
        fhn_s[...] = jnp.zeros((T, D_MODEL), BF16)
        fres_s[...] = jnp.zeros((T, D_MODEL), F32)

    @pl.when(s == 0)
    def _():
        kt_ext[:, 0:WINDOW] = jnp.zeros((KV_WIDTH, WINDOW), BF16)
        v_ext[0:WINDOW, :] = jnp.zeros((WINDOW, KV_WIDTH), BF16)
        vr_ext[0:WINDOW, :] = jnp.zeros((WINDOW, KV_WIDTH), BF16)
        u_ext[0:CONV_HALO, :] = jnp.zeros((CONV_HALO, CONV_WIDTH), F32)
        xl_ext[0:LRU_HALO, :] = jnp.zeros((LRU_HALO, LRU_WIDTH), F32)
        hcar[...] = jnp.zeros((SUBLANES, LRU_WIDTH), F32)

    qi = lax.broadcasted_iota(jnp.int32, (WINDOW, 2 * WINDOW), 0)
    kc = lax.broadcasted_iota(jnp.int32, (WINDOW, 2 * WINDOW), 1)
    band = (kc > qi) & (kc <= qi + WINDOW)
    bias_s[0] = jnp.where(band, 0.0, MASK_VALUE).astype(F32)
    bias_s[1] = jnp.where(band & (kc >= WINDOW), 0.0, MASK_VALUE).astype(F32)

    mg = mixg_ref[...]
    lane = lax.broadcasted_iota(jnp.int32, (2 * WINDOW, KV_WIDTH), 1)
    band_row = lax.broadcasted_iota(jnp.int32, (2 * WINDOW, KV_WIDTH), 0)
    keep_lo = (lane < HEAD_DIM) & (band_row != 0)
    drop_hi = (lane < HEAD_DIM) | (band_row == 0)
    lane_o = lax.broadcasted_iota(jnp.int32, (WINDOW, LANES), 1)
    zk = jnp.zeros((HEAD_DIM, 2 * WINDOW), BF16)
    row8 = lax.broadcasted_iota(jnp.int32, (SUBLANES, LANES), 0)
    L, P = SCAN_LEN, SCAN_PITCH

    def ffn_prep(r):
        rows = slice(r * SUB, (r + 1) * SUB)
        h = hmid[rows, :]
        fhn_s[rows, :] = _rms(h, n2_ref[...]).astype(BF16)
        fres_s[rows, :] = h

    def ffn_up(c, p):
        rows = slice((p // 2) * (T // 2), (p // 2 + 1) * (T // 2))
        n0 = (p % 2) * 2 * MXU_COLS
        u = jnp.dot(fhn_s[rows, :], wup_ref[:, c * FF_CHUNK + n0:c * FF_CHUNK + n0 + 2 * MXU_COLS],
                    preferred_element_type=F32).astype(BF16)
        u = jnp.maximum(u, 0.0)
        fu_s[rows, n0:n0 + 2 * MXU_COLS] = u * u

    def ffn_down(c, p):
        rows = slice((p // 2) * (T // 2), (p // 2 + 1) * (T // 2))
        cols = slice((p % 2) * 2 * MXU_COLS, (p % 2 + 1) * 2 * MXU_COLS)
        d = jnp.dot(fu_s[rows, :], wdn_ref[c * FF_CHUNK:(c + 1) * FF_CHUNK, cols], preferred_element_type=F32)
        if c == 0:
            o_ref[rows, cols] = fres_s[rows, cols] + d
        else:
            o_ref[rows, cols] += d

    def ffn_finish(half):
        if final:
            rows = slice(half * (T // 2), (half + 1) * (T // 2))
            o_ref[rows, :] = _rms(o_ref[rows, :], fn_ref[...])

    def inproj(r):
        r0 = r * SUB
        hn = _rms(x_ref[r0:r0 + SUB, :], n1_ref[...]).astype(BF16)
        z_s[r0:r0 + SUB, :] = jnp.dot(hn, win_ref[...], preferred_element_type=F32)

    def attn_qk(r, st):
        r0 = r * SUB
        q = z_s[r0:r0 + SUB, Q_OFF:Q_OFF + ATTN_WIDTH].astype(BF16)
        kf = z_s[r0:r0 + SUB, K_OFF:K_OFF + KV_WIDTH]
        vf = z_s[r0:r0 + SUB, V_OFF:V_OFF + KV_WIDTH]
        kt_ext[:, WINDOW + r0:WINDOW + r0 + SUB] = kf.T.astype(BF16)
        v_ext[WINDOW + r0:WINDOW + r0 + SUB, :] = vf.astype(BF16)
        vr_ext[WINDOW + r0:WINDOW + r0 + SUB, :] = pltpu.roll(vf, HEAD_DIM, axis=1).astype(BF16)
        kt_band = kt_ext[:, r0:r0 + 2 * WINDOW]
        st["sc"] = []
        for h in range(N_KV_HEADS):
            kh = kt_band[h * HEAD_DIM:(h + 1) * HEAD_DIM, :]
            kt2 = jnp.concatenate(
                [jnp.concatenate([kh, zk], axis=0), jnp.concatenate([zk, kh], axis=0)], axis=1)
            c0 = h * 2 * LANES
            qs = jnp.concatenate([q[:, c0:c0 + LANES], q[:, c0 + LANES:c0 + 2 * LANES]], axis=0)
            st["sc"].append(jnp.dot(qs, kt2, preferred_element_type=F32))

    def attn_softmax(r, h, st):
        bias = bias_s[jnp.where(s == 0, 1, 0)] if r == 0 else bias_s[0]
        sc = st["sc"][h]
        p_rows = []
        dens = []
        for rr in range(2):
            p_cols = []
            den_r = []
            for e in range(2):
                sink = sinks_ref[layer, 4 * h + 2 * rr + e]
                rws = slice(rr * WINDOW, (rr + 1) * WINDOW)
                c0 = e * 2 * WINDOW
                t = jnp.concatenate(
                    [jnp.where(lane_o == 0, sink, sc[rws, c0:c0 + LANES] + bias[:, :LANES]),
                     sc[rws, c0 + LANES:c0 + 2 * LANES] + bias[:, LANES:]], axis=1)
                p = jnp.exp(t - jnp.max(t, axis=-1, keepdims=True))
                den_r.append(jnp.sum(p, axis=-1, keepdims=True))
                p_cols.append(p.astype(BF16))
            p_rows.append(jnp.concatenate(p_cols, axis=1))
            dens.append(den_r)
        st[("pm", h)] = jnp.concatenate(p_rows, axis=0)
        st[("den", h)] = dens

    def attn_pv(r, h, st):
        r0 = r * SUB
        v_band = v_ext[r0:r0 + 2 * WINDOW, :]
        vr_band = vr_ext[r0:r0 + 2 * WINDOW, :]
        if h == 0:
            va = jnp.where(keep_lo, v_band, jnp.zeros_like(v_band))
            vb = jnp.where(drop_hi, jnp.zeros_like(vr_band), vr_band)
        else:
            va = jnp.where(keep_lo, vr_band, jnp.zeros_like(vr_band))
            vb = jnp.where(drop_hi, jnp.zeros_like(v_band), v_band)
        v2 = jnp.concatenate([va, vb], axis=0)
        o = jnp.dot(st[("pm", h)], v2, preferred_element_type=F32)
        dens = st[("den", h)]
        for rr in range(2):
            den = jnp.where(lane_o < HEAD_DIM, dens[rr][0], dens[rr][1])
            st[("ya", 2 * h + rr)] = o[rr * WINDOW:(rr + 1) * WINDOW, :] / den

    def attn_out(r, st):
        rows = slice(r * SUB, (r + 1) * SUB)
        y_attn = jnp.concatenate([st[("ya", c)] for c in range(ATTN_WIDTH // LANES)], axis=1)
        ya = _rms(y_attn, mg[:, YA_OFF:YA_OFF + ATTN_WIDTH]).astype(BF16)
        hmid[rows, :] = x_ref[rows, :] + jnp.dot(ya, wout_ref[YA_OFF:YA_OFF + ATTN_WIDTH, :],
                                                 preferred_element_type=F32)

    def conv_unit(r, lt, st):
        r0 = r * SUB
        cl = slice(lt * LANES, (lt + 1) * LANES)
        cval = z_s[r0:r0 + SUB, CV_OFF + lt * LANES:CV_OFF + (lt + 1) * LANES]
        cgate = z_s[r0:r0 + SUB, CG_OFF + lt * LANES:CG_OFF + (lt + 1) * LANES]
        u_ext[CONV_HALO + r0:CONV_HALO + r0 + SUB, cl] = cval * _sigmoid(cgate)
        base = CONV_HALO - (CONV_KERNEL - 1)
        acc = jnp.broadcast_to(cb_ref[:, cl], (SUB, LANES))
        for sh in range(SUBLANES):
            part = None
            nrows = SUB + (SUBLANES if sh else 0)
            for k in range(CONV_KERNEL):
                if (base + k) % SUBLANES != sh:
                    continue
                al = r0 + base + k - sh
                term = cw_ref[k:k + 1, cl] * u_ext[al:al + nrows, cl]
                part = term if part is None else part + term
            acc = acc + part[sh:sh + SUB, :]
        st[("yc", lt)] = acc

    def conv_post(r, st):
        uc = jnp.concatenate([st[("yc", lt)] for lt in range(CONV_WIDTH // LANES)], axis=1)
        mu = jnp.mean(uc, axis=-1, keepdims=True)
        xc_ = uc - mu
        ln = xc_ * lax.rsqrt(jnp.mean(xc_ * xc_, axis=-1, keepdims=True) + LN_EPS) * lng_ref[...] + lnb_ref[...]
        y_conv = ln * _sigmoid(ln)
        yc = _rms(y_conv, mg[:, YC_OFF:YC_OFF + CONV_WIDTH]).astype(BF16)
        hmid[r * SUB:(r + 1) * SUB, :] += jnp.dot(yc, wout_ref[YC_OFF:YC_OFF + CONV_WIDTH, :],
                                                  preferred_element_type=F32)

    def lru_gates(r, st):
        r0 = r * SUB
        xl_ext[LRU_HALO + r0:LRU_HALO + r0 + SUB, :] = z_s[r0:r0 + SUB, RX_OFF:RX_OFF + LRU_WIDTH]
        lbase = LRU_HALO - (LRU_CONV_KERNEL - 1)
        xc = jnp.broadcast_to(lcb_ref[...], (SUB, LRU_WIDTH))
        for k in range(LRU_CONV_KERNEL):
            xc = xc + lcw_ref[k:k + 1, :] * xl_ext[r0 + lbase + k:r0 + lbase + k + SUB, :]
        st["xc"] = xc
        st["gates"] = jnp.dot(xc.astype(BF16), wg_ref[...], preferred_element_type=F32) + bg_ref[...]

    def lru_scan(r, st):
        r0 = r * SUB
        xc, gates = st["xc"], st["gates"]
        rgate = _sigmoid(gates[:, :LRU_WIDTH])
        igate = _sigmoid(gates[:, LRU_WIDTH:])
        log_a = (-LRU_C * rgate) * jax.nn.softplus(-lam_ref[...])
        a_full = jnp.exp(log_a)
        th = jnp.tanh(log_a)
        b_full = jnp.sqrt(2.0 * th / (th - 1.0)) * (igate * xc)
        yl_cols = []
        for lt in range(LRU_WIDTH // LANES):
            cl = slice(lt * LANES, (lt + 1) * LANES)
            for c in range(SUBLANES):
                a_s[lt, c * P:c * P + L, :] = a_full[c * L:(c + 1) * L, cl]
                b_s[lt, c * P:c * P + L, :] = b_full[c * L:(c + 1) * L, cl]
            hloc = jnp.zeros((SUBLANES, LANES), F32)
            cum = jnp.ones((SUBLANES, LANES), F32)
            for m in range(L):
                am = a_s[lt, pl.ds(m, SUBLANES, stride=P), :]
                bm = b_s[lt, pl.ds(m, SUBLANES, stride=P), :]
                hloc = am * hloc + bm
                cum = am * cum
                b_s[lt, pl.ds(m, SUBLANES, stride=P), :] = hloc
                a_s[lt, pl.ds(m, SUBLANES, stride=P), :] = cum
            ca, cbv = cum, hloc
            for d in (1, 2, 4):
                a_sh = jnp.where(row8 >= d, pltpu.roll(ca, d, axis=0), 1.0)
                b_sh = jnp.where(row8 >= d, pltpu.roll(cbv, d, axis=0), 0.0)
                cbv = ca * b_sh + cbv
                ca = ca * a_sh
            hprev = hcar[:, cl]
            ends = ca * hprev + cbv
            carry_in = jnp.where(row8 == 0, hprev, pltpu.roll(ends, 1, axis=0))
            hcar[:, cl] = jnp.broadcast_to(ends[SUBLANES - 1:SUBLANES, :], (SUBLANES, LANES))
            parts = []
            for c in range(SUBLANES):
                g = jnp.broadcast_to(carry_in[c:c + 1, :], (L, LANES))
                parts.append(b_s[lt, c * P:c * P + L, :] + a_s[lt, c * P:c * P + L, :] * g)
            yl_cols.append(jnp.concatenate(parts, axis=0))
        y_lru = jnp.concatenate(yl_cols, axis=1) * jax.nn.gelu(z_s[r0:r0 + SUB, RG_OFF:RG_OFF + LRU_WIDTH])
        yl = _rms(y_lru, mg[:, YL_OFF:YL_OFF + LRU_WIDTH]).astype(BF16)
        hmid[r0:r0 + SUB, :] += jnp.dot(yl, wout_ref[YL_OFF:YL_OFF + LRU_WIDTH, :], preferred_element_type=F32)

    pieces = []
    for c in range(N_SUB):
        pieces += [functools.partial(ffn_up, c, p) for p in range(4)]
        pieces += [functools.partial(ffn_down, c, p) for p in range(4)]
    n_prepped = [0]

    def next_ffn(mixer_rows_done):
        k = len(pieces_done)
        pieces[k]()
        pieces_done.append(k)
        if k >= N_SUB * 8 - 5:
            while n_prepped[0] < mixer_rows_done:
                ffn_prep(n_prepped[0])
                n_prepped[0] += 1
        if k == N_SUB * 8 - 3:
            ffn_finish(0)
        if k == N_SUB * 8 - 1:
            ffn_finish(1)

    pieces_done = []
    next_ffn(0)
    inproj(0)
    for r in range(N_SUB):
        st = {}
        attn_qk(r, st)
        next_ffn(r)
        attn_softmax(r, 0, st)
        attn_pv(r, 0, st)
        next_ffn(r)
        attn_softmax(r, 1, st)
        attn_pv(r, 1, st)
        next_ffn(r)
        attn_out(r, st)
        next_ffn(r)
        conv_unit(r, 0, st)
        next_ffn(r)
        conv_unit(r, 1, st)
        next_ffn(r)
        conv_post(r, st)
        if r + 1 < N_SUB:
            inproj(r + 1)
        lru_gates(r, st)
        next_ffn(r)
        lru_scan(r, st)
        if r + 1 < N_SUB:
            next_ffn(r + 1)
    while n_prepped[0] < N_SUB:
        ffn_prep(n_prepped[0])
        n_prepped[0] += 1

    kt_ext[:, 0:WINDOW] = kt_ext[:, T:T + WINDOW]
    v_ext[0:WINDOW, :] = v_ext[T:T + WINDOW, :]
    vr_ext[0:WINDOW, :] = vr_ext[T:T + WINDOW, :]
    u_ext[0:CONV_HALO, :] = u_ext[T:T + CONV_HALO, :]
    xl_ext[0:LRU_HALO, :] = xl_ext[T:T + LRU_HALO, :]


def _layer_spec(arr, layer):
    nd = arr.ndim
    return pl.BlockSpec((None,) + arr.shape[1:], lambda i: (layer,) + (0,) * (nd - 1),
                        pipeline_mode=pl.Buffered(1))


def _layer_call(x, sinks, consts, seq_len, layer, final):
    M, D = x.shape
    T = MIX_T
    nblk = M // T
    return pl.pallas_call(
        functools.partial(_layer_kernel, layer=layer, final=final, blocks_per_seq=seq_len // T),
        out_shape=jax.ShapeDtypeStruct((M, D), F32),
        grid=(nblk + 1,),
        in_specs=[pl.BlockSpec(memory_space=pltpu.SMEM),
                  pl.BlockSpec((T, D), lambda i: (jnp.minimum(i, nblk - 1), 0))]
                 + [_layer_spec(c, layer) for c in consts],
        out_specs=pl.BlockSpec((T, D), lambda i: (jnp.maximum(i - 1, 0), 0)),
        scratch_shapes=[
            pltpu.VMEM((KV_WIDTH, WINDOW + T), BF16),
            pltpu.VMEM((WINDOW + T, KV_WIDTH), BF16),
            pltpu.VMEM((WINDOW + T, KV_WIDTH), BF16),
            pltpu.VMEM((CONV_HALO + T, CONV_WIDTH), F32),
            pltpu.VMEM((LRU_HALO + T, LRU_WIDTH), F32),
            pltpu.VMEM((SUBLANES, LRU_WIDTH), F32),
            pltpu.VMEM((2, WINDOW, 2 * WINDOW), F32),
            pltpu.VMEM((LRU_WIDTH // LANES, SUBLANES * SCAN_PITCH, LANES), F32),
            pltpu.VMEM((LRU_WIDTH // LANES, SUBLANES * SCAN_PITCH, LANES), F32),
            pltpu.VMEM((T, IN_WIDTH), F32),
            pltpu.VMEM((T, D), F32),
            pltpu.VMEM((T, D), BF16),
            pltpu.VMEM((T, D), F32),
            pltpu.VMEM((T, FF_CHUNK), BF16),
        ],
        compiler_params=pltpu.CompilerParams(
            dimension_semantics=("arbitrary",), vmem_limit_bytes=VMEM_LIMIT),
        name="layer_final" if final else "layer",
    )(sinks, x, *consts)


def _block_diag(w):
    dp, hN, di, dj = w.shape
    eye = jnp.eye(hN, dtype=w.dtype)
    return (eye[None, :, None, :, None] * w[:, :, :, None, :]).reshape(dp, hN * di, hN * dj)


def kernel(x, norm1, w_in, attn_sinks, conv_dw_w, conv_dw_b, conv_ln_g, conv_ln_b, lru_conv_w, lru_conv_b,
           lru_wa, lru_ba, lru_wx, lru_bx, lru_lambda, mix_norm, w_out, norm2, w_up, w_down, final_norm):
    B, S, D = x.shape
    depth = w_in.shape[0]
    qscale = jnp.concatenate([jnp.full((ATTN_WIDTH,), HEAD_DIM ** -0.5, F32),
                              jnp.ones((IN_WIDTH - ATTN_WIDTH,), F32)])
    win = (w_in * qscale).astype(BF16)
    wg = jnp.concatenate([_block_diag(lru_wa), _block_diag(lru_wx)], axis=2).astype(BF16)
    vec = jnp.concatenate(
        [norm1, conv_dw_b, conv_ln_g, conv_ln_b, lru_conv_b, lru_ba.reshape(depth, -1), lru_bx.reshape(depth, -1),
         lru_lambda, mix_norm, norm2, jnp.broadcast_to(final_norm[None], (depth, D))], axis=1)[:, None, :]
    consts = (vec, win, conv_dw_w, lru_conv_w, wg, w_out.astype(BF16), w_up.astype(BF16), w_down.astype(BF16))
    h = x.reshape(B * S, D)
    for l in range(depth):
        h = _layer_call(h, attn_sinks, consts, S, layer=l, final=(l == depth - 1))
    return h.reshape(B, S, D)
```

```python
import functools

import jax
import jax.numpy as jnp
from jax import lax
from jax.experimental import pallas as pl
from jax.experimental.pallas import tpu as pltpu

D_MODEL = 1024
HEAD_DIM = 64
ATTN_WIDTH = 512
N_Q_HEADS = 8
N_KV_HEADS = 2
KV_WIDTH = 128
WINDOW = 128
CONV_WIDTH = 256
CONV_KERNEL = 31
LRU_WIDTH = 256
LRU_HEADS = 4
LRU_HEAD_DIM = 64
LRU_CONV_KERNEL = 4
LRU_C = 8.0
MIX_WIDTH = 1024
IN_WIDTH = 1792
D_FF = 4096
RMS_EPS = 1e-6
LN_EPS = 1e-5
MASK_VALUE = -1e30

Q_OFF, K_OFF, V_OFF, CV_OFF, CG_OFF, RX_OFF, RG_OFF = 0, 512, 640, 768, 1024, 1280, 1536
YA_OFF, YC_OFF, YL_OFF = 0, ATTN_WIDTH, ATTN_WIDTH + CONV_WIDTH

VEC_NAMES = ("norm1", "conv_b", "ln_g", "ln_b", "lru_conv_b", "gate_b", "lambda", "mix_norm", "norm2", "final_norm")
VEC_WIDTHS = dict(zip(VEC_NAMES, (D_MODEL, CONV_WIDTH, CONV_WIDTH, CONV_WIDTH, LRU_WIDTH, 2 * LRU_WIDTH,
                                  LRU_WIDTH, MIX_WIDTH, D_MODEL, D_MODEL)))
VEC_OFF = {name: sum(VEC_WIDTHS[n] for n in VEC_NAMES[:k]) for k, name in enumerate(VEC_NAMES)}

LANES = 128
SUBLANES = 8
MXU_COLS = 256

MIX_T = 512
SUB = WINDOW
N_SUB = MIX_T // SUB
FF_CHUNK = D_FF // N_SUB
FF_PIECES = FF_CHUNK // MXU_COLS
DN_PIECES = D_MODEL // MXU_COLS
FFN_SPLIT = 1
CONV_HALO = 32
LRU_HALO = 8
SCAN_LEN = SUB // SUBLANES
SCAN_PITCH = SCAN_LEN + SUBLANES
VMEM_LIMIT = 56 * 1024 * 1024

F32 = jnp.float32
BF16 = jnp.bfloat16


def _sigmoid(x):
    return 0.5 * jnp.tanh(0.5 * x) + 0.5


def _rms(x, g):
    return x * lax.rsqrt(jnp.mean(x * x, axis=-1, keepdims=True) + RMS_EPS) * g


class _Vec:
    def __init__(self, ref, name):
        self.ref, self.off, self.width = ref, VEC_OFF[name], VEC_WIDTHS[name]

    def __getitem__(self, idx):
        if idx is Ellipsis:
            return self.ref[:, self.off:self.off + self.width]
        rows, cols = idx
        return self.ref[rows, self.off + cols.start:self.off + cols.stop]


def _layer_kernel(sinks_ref, x_ref, vec_ref, win_ref, cw_ref, lcw_ref, wg_ref, wout_ref, wup_ref, wdn_ref,
                  o_ref,
                  kt_ext, v_ext, vr_ext, u_ext, xl_ext, hcar, bias_s, a_s, b_s,
                  z_s, hmid, fhn_s, fres_s, fu_s,
                  *, layer, final, blocks_per_seq):
    T = MIX_T
    i = pl.program_id(0)
    s = lax.rem(i, blocks_per_seq)
    n1_ref, cb_ref, lng_ref, lnb_ref, lcb_ref, bg_ref, lam_ref, mixg_ref, n2_ref, fn_ref = (
        _Vec(vec_ref, name) for name in VEC_NAMES)

    @pl.when(i == 0)
    def _():
        fhn_s[...] = jnp.zeros((T, D_MODEL), BF16)
        fres_s[...] = jnp.zeros((T, D_MODEL), F32)

    @pl.when(s == 0)
    def _():
        kt_ext[:, 0:WINDOW] = jnp.zeros((KV_WIDTH, WINDOW), BF16)
        v_ext[0:WINDOW, :] = jnp.zeros((WINDOW, KV_WIDTH), BF16)
        vr_ext[0:WINDOW, :] = jnp.zeros((WINDOW, KV_WIDTH), BF16)
        u_ext[0:CONV_HALO, :] = jnp.zeros((CONV_HALO, CONV_WIDTH), F32)
        xl_ext[0:LRU_HALO, :] = jnp.zeros((LRU_HALO, LRU_WIDTH), F32)
        hcar[...] = jnp.zeros((SUBLANES, LRU_WIDTH), F32)

    qi = lax.broadcasted_iota(jnp.int32, (WINDOW, 2 * WINDOW), 0)
    kc = lax.broadcasted_iota(jnp.int32, (WINDOW, 2 * WINDOW), 1)
    band = (kc > qi) & (kc <= qi + WINDOW)
    bias_s[0] = jnp.where(band, 0.0, MASK_VALUE).astype(F32)
    bias_s[1] = jnp.where(band & (kc >= WINDOW), 0.0, MASK_VALUE).astype(F32)

    mg = mixg_ref[...]
    lane = lax.broadcasted_iota(jnp.int32, (2 * WINDOW, KV_WIDTH), 1)
    band_row = lax.broadcasted_iota(jnp.int32, (2 * WINDOW, KV_WIDTH), 0)
    keep_lo = (lane < HEAD_DIM) & (band_row != 0)
    drop_hi = (lane < HEAD_DIM) | (band_row == 0)
    lane_o = lax.broadcasted_iota(jnp.int32, (WINDOW, LANES), 1)
    zk = jnp.zeros((HEAD_DIM, 2 * WINDOW), BF16)
    row8 = lax.broadcasted_iota(jnp.int32, (SUBLANES, LANES), 0)
    L, P = SCAN_LEN, SCAN_PITCH

    def ffn_prep(r):
        rows = slice(r * SUB, (r + 1) * SUB)
        h = hmid[rows, :]
        fhn_s[rows, :] = _rms(h, n2_ref[...]).astype(BF16)
        fres_s[rows, :] = h

    def ffn_up(c, p):
        n0 = (p % 2) * 2 * MXU_COLS
        for q in range(FFN_SPLIT):
            r0 = (p // 2) * (T // 2) + q * (T // 2 // FFN_SPLIT)
            rows = slice(r0, r0 + T // 2 // FFN_SPLIT)
            u = jnp.dot(fhn_s[rows, :], wup_ref[:, c * FF_CHUNK + n0:c * FF_CHUNK + n0 + 2 * MXU_COLS],
                        preferred_element_type=F32).astype(BF16)
            u = jnp.maximum(u, 0.0)
            fu_s[c % 2, rows, n0:n0 + 2 * MXU_COLS] = u * u

    def ffn_down(c, p):
        cols = slice((p % 2) * 2 * MXU_COLS, (p % 2 + 1) * 2 * MXU_COLS)
        for q in range(FFN_SPLIT):
            r0 = (p // 2) * (T // 2) + q * (T // 2 // FFN_SPLIT)
            rows = slice(r0, r0 + T // 2 // FFN_SPLIT)
            d = jnp.dot(fu_s[c % 2, rows, :], wdn_ref[c * FF_CHUNK:(c + 1) * FF_CHUNK, cols],
                        preferred_element_type=F32)
            if c == 0:
                o_ref[rows, cols] = fres_s[rows, cols] + d
            else:
                o_ref[rows, cols] += d

    def ffn_finish(half):
        if final:
            rows = slice(half * (T // 2), (half + 1) * (T // 2))
            o_ref[rows, :] = _rms(o_ref[rows, :], fn_ref[...])

    def inproj(r):
        r0 = r * SUB
        hn = _rms(x_ref[r0:r0 + SUB, :], n1_ref[...]).astype(BF16)
        z_s[r0:r0 + SUB, :] = jnp.dot(hn, win_ref[...], preferred_element_type=F32)

    def attn_qk(r, st):
        r0 = r * SUB
        q = z_s[r0:r0 + SUB, Q_OFF:Q_OFF + ATTN_WIDTH].astype(BF16)
        kf = z_s[r0:r0 + SUB, K_OFF:K_OFF + KV_WIDTH]
        vf = z_s[r0:r0 + SUB, V_OFF:V_OFF + KV_WIDTH]
        kt_ext[:, WINDOW + r0:WINDOW + r0 + SUB] = kf.T.astype(BF16)
        v_ext[WINDOW + r0:WINDOW + r0 + SUB, :] = vf.astype(BF16)
        vr_ext[WINDOW + r0:WINDOW + r0 + SUB, :] = pltpu.roll(vf, HEAD_DIM, axis=1).astype(BF16)
        kt_band = kt_ext[:, r0:r0 + 2 * WINDOW]
        st["sc"] = []
        for h in range(N_KV_HEADS):
            kh = kt_band[h * HEAD_DIM:(h + 1) * HEAD_DIM, :]
            kt2 = jnp.concatenate(
                [jnp.concatenate([kh, zk], axis=0), jnp.concatenate([zk, kh], axis=0)], axis=1)
            c0 = h * 2 * LANES
            qs = jnp.concatenate([q[:, c0:c0 + LANES], q[:, c0 + LANES:c0 + 2 * LANES]], axis=0)
            st["sc"].append(jnp.dot(qs, kt2, preferred_element_type=F32))

    def attn_softmax(r, h, st):
        bias = bias_s[jnp.where(s == 0, 1, 0)] if r == 0 else bias_s[0]
        sc = st["sc"][h]
        p_rows = []
        dens = []
        for rr in range(2):
            p_cols = []
            den_r = []
            for e in range(2):
                sink = sinks_ref[layer, 4 * h + 2 * rr + e]
                rws = slice(rr * WINDOW, (rr + 1) * WINDOW)
                c0 = e * 2 * WINDOW
                t = jnp.concatenate(
                    [jnp.where(lane_o == 0, sink, sc[rws, c0:c0 + LANES] + bias[:, :LANES]),
                     sc[rws, c0 + LANES:c0 + 2 * LANES] + bias[:, LANES:]], axis=1)
                p = jnp.exp(t - jnp.max(t, axis=-1, keepdims=True))
                den_r.append(jnp.sum(p, axis=-1, keepdims=True))
                p_cols.append(p.astype(BF16))
            p_rows.append(jnp.concatenate(p_cols, axis=1))
            dens.append(den_r)
        st[("pm", h)] = jnp.concatenate(p_rows, axis=0)
        st[("den", h)] = dens

    def attn_pv(r, h, st):
        r0 = r * SUB
        v_band = v_ext[r0:r0 + 2 * WINDOW, :]
        vr_band = vr_ext[r0:r0 + 2 * WINDOW, :]
        if h == 0:
            va = jnp.where(keep_lo, v_band, jnp.zeros_like(v_band))
            vb = jnp.where(drop_hi, jnp.zeros_like(vr_band), vr_band)
        else:
            va = jnp.where(keep_lo, vr_band, jnp.zeros_like(vr_band))
            vb = jnp.where(drop_hi, jnp.zeros_like(v_band), v_band)
        v2 = jnp.concatenate([va, vb], axis=0)
        o = jnp.dot(st[("pm", h)], v2, preferred_element_type=F32)
        dens = st[("den", h)]
        for rr in range(2):
            den = jnp.where(lane_o < HEAD_DIM, dens[rr][0], dens[rr][1])
            st[("ya", 2 * h + rr)] = o[rr * WINDOW:(rr + 1) * WINDOW, :] / den

    def attn_norm(r, st):
        y_attn = jnp.concatenate([st[("ya", c)] for c in range(ATTN_WIDTH // LANES)], axis=1)
        st["ya_n"] = _rms(y_attn, mg[:, YA_OFF:YA_OFF + ATTN_WIDTH]).astype(BF16)

    def attn_dot(r, st):
        rows = slice(r * SUB, (r + 1) * SUB)
        hmid[rows, :] = x_ref[rows, :] + jnp.dot(st["ya_n"], wout_ref[YA_OFF:YA_OFF + ATTN_WIDTH, :],
                                                 preferred_element_type=F32)

    def conv_unit(r, lt, st):
        r0 = r * SUB
        cl = slice(lt * LANES, (lt + 1) * LANES)
        cval = z_s[r0:r0 + SUB, CV_OFF + lt * LANES:CV_OFF + (lt + 1) * LANES]
        cgate = z_s[r0:r0 + SUB, CG_OFF + lt * LANES:CG_OFF + (lt + 1) * LANES]
        u_ext[CONV_HALO + r0:CONV_HALO + r0 + SUB, cl] = cval * _sigmoid(cgate)
        base = CONV_HALO - (CONV_KERNEL - 1)
        acc = jnp.broadcast_to(cb_ref[:, cl], (SUB, LANES))
        for sh in range(SUBLANES):
            part = None
            nrows = SUB + (SUBLANES if sh else 0)
            for k in range(CONV_KERNEL):
                if (base + k) % SUBLANES != sh:
                    continue
                al = r0 + base + k - sh
                term = cw_ref[k:k + 1, cl] * u_ext[al:al + nrows, cl]
                part = term if part is None else part + term
            acc = acc + part[sh:sh + SUB, :]
        st[("yc", lt)] = acc

    def conv_post(r, st):
        uc = jnp.concatenate([st[("yc", lt)] for lt in range(CONV_WIDTH // LANES)], axis=1)
        mu = jnp.mean(uc, axis=-1, keepdims=True)
        xc_ = uc - mu
        ln = xc_ * lax.rsqrt(jnp.mean(xc_ * xc_, axis=-1, keepdims=True) + LN_EPS) * lng_ref[...] + lnb_ref[...]
        y_conv = ln * _sigmoid(ln)
        st["yc_n"] = _rms(y_conv, mg[:, YC_OFF:YC_OFF + CONV_WIDTH]).astype(BF16)

    def conv_dot(r, st):
        hmid[r * SUB:(r + 1) * SUB, :] += jnp.dot(st["yc_n"], wout_ref[YC_OFF:YC_OFF + CONV_WIDTH, :],
                                                  preferred_element_type=F32)

    def lru_pre(r, st):
        r0 = r * SUB
        xl_ext[LRU_HALO + r0:LRU_HALO + r0 + SUB, :] = z_s[r0:r0 + SUB, RX_OFF:RX_OFF + LRU_WIDTH]
        lbase = LRU_HALO - (LRU_CONV_KERNEL - 1)
        xc = jnp.broadcast_to(lcb_ref[...], (SUB, LRU_WIDTH))
        for k in range(LRU_CONV_KERNEL):
            xc = xc + lcw_ref[k:k + 1, :] * xl_ext[r0 + lbase + k:r0 + lbase + k + SUB, :]
        st["xc"] = xc
        st["xc_b"] = xc.astype(BF16)

    def lru_gate_dot(r, st):
        st["gates"] = jnp.dot(st["xc_b"], wg_ref[...], preferred_element_type=F32) + bg_ref[...]

    def lru_scan(r, st):
        r0 = r * SUB
        xc, gates = st["xc"], st["gates"]
        rgate = _sigmoid(gates[:, :LRU_WIDTH])
        igate = _sigmoid(gates[:, LRU_WIDTH:])
        log_a = (-LRU_C * rgate) * jax.nn.softplus(-lam_ref[...])
        a_full = jnp.exp(log_a)
        th = jnp.tanh(log_a)
        b_full = jnp.sqrt(2.0 * th / (th - 1.0)) * (igate * xc)
        yl_cols = []
        for lt in range(LRU_WIDTH // LANES):
            cl = slice(lt * LANES, (lt + 1) * LANES)
            for c in range(SUBLANES):
                a_s[lt, c * P:c * P + L, :] = a_full[c * L:(c + 1) * L, cl]
                b_s[lt, c * P:c * P + L, :] = b_full[c * L:(c + 1) * L, cl]
            hloc = jnp.zeros((SUBLANES, LANES), F32)
            cum = jnp.ones((SUBLANES, LANES), F32)
            for m in range(L):
                am = a_s[lt, pl.ds(m, SUBLANES, stride=P), :]
                bm = b_s[lt, pl.ds(m, SUBLANES, stride=P), :]
                hloc = am * hloc + bm
                cum = am * cum
                b_s[lt, pl.ds(m, SUBLANES, stride=P), :] = hloc
                a_s[lt, pl.ds(m, SUBLANES, stride=P), :] = cum
            ca, cbv = cum, hloc
            for d in (1, 2, 4):
                a_sh = jnp.where(row8 >= d, pltpu.roll(ca, d, axis=0), 1.0)
                b_sh = jnp.where(row8 >= d, pltpu.roll(cbv, d, axis=0), 0.0)
                cbv = ca * b_sh + cbv
                ca = ca * a_sh
            hprev = hcar[:, cl]
            ends = ca * hprev + cbv
            carry_in = jnp.where(row8 == 0, hprev, pltpu.roll(ends, 1, axis=0))
            hcar[:, cl] = jnp.broadcast_to(ends[SUBLANES - 1:SUBLANES, :], (SUBLANES, LANES))
            parts = []
            for c in range(SUBLANES):
                g = jnp.broadcast_to(carry_in[c:c + 1, :], (L, LANES))
                parts.append(b_s[lt, c * P:c * P + L, :] + a_s[lt, c * P:c * P + L, :] * g)
            yl_cols.append(jnp.concatenate(parts, axis=0))
        y_lru = jnp.concatenate(yl_cols, axis=1) * jax.nn.gelu(z_s[r0:r0 + SUB, RG_OFF:RG_OFF + LRU_WIDTH])
        st["yl_n"] = _rms(y_lru, mg[:, YL_OFF:YL_OFF + LRU_WIDTH]).astype(BF16)

    def lru_dot(r, st):
        hmid[r * SUB:(r + 1) * SUB, :] += jnp.dot(st["yl_n"], wout_ref[YL_OFF:YL_OFF + LRU_WIDTH, :],
                                                  preferred_element_type=F32)

    pieces = []
    for c in range(N_SUB):
        pieces += [functools.partial(ffn_up, c, p) for p in range(4)]
        pieces += [functools.partial(ffn_down, c, p) for p in range(4)]
    n_prepped = [0]

    def next_ffn(mixer_rows_done):
        k = len(pieces_done)
        pieces[k]()
        pieces_done.append(k)
        if k >= N_SUB * 8 - 5:
            while n_prepped[0] < mixer_rows_done:
                ffn_prep(n_prepped[0])
                n_prepped[0] += 1
        if k == N_SUB * 8 - 3:
            ffn_finish(0)
        if k == N_SUB * 8 - 1:
            ffn_finish(1)

    pieces_done = []
    next_ffn(0)
    inproj(0)
    for r in range(N_SUB):
        st = {}
        attn_qk(r, st)
        next_ffn(r)
        attn_softmax(r, 0, st)
        attn_softmax(r, 1, st)
        next_ffn(r)
        attn_pv(r, 0, st)
        next_ffn(r)
        attn_pv(r, 1, st)
        attn_norm(r, st)
        conv_unit(r, 0, st)
        next_ffn(r)
        attn_dot(r, st)
        conv_unit(r, 1, st)
        next_ffn(r)
        conv_post(r, st)
        lru_pre(r, st)
        next_ffn(r)
        conv_dot(r, st)
        lru_gate_dot(r, st)
        if r + 1 < N_SUB:
            inproj(r + 1)
        lru_scan(r, st)
        next_ffn(r)
        lru_dot(r, st)
        if r + 1 < N_SUB:
            next_ffn(r + 1)
    while n_prepped[0] < N_SUB:
        ffn_prep(n_prepped[0])
        n_prepped[0] += 1

    kt_ext[:, 0:WINDOW] = kt_ext[:, T:T + WINDOW]
    v_ext[0:WINDOW, :] = v_ext[T:T + WINDOW, :]
    vr_ext[0:WINDOW, :] = vr_ext[T:T + WINDOW, :]
    u_ext[0:CONV_HALO, :] = u_ext[T:T + CONV_HALO, :]
    xl_ext[0:LRU_HALO, :] = xl_ext[T:T + LRU_HALO, :]


def _layer_spec(arr, layer):
    nd = arr.ndim
    return pl.BlockSpec((None,) + arr.shape[1:], lambda i: (layer,) + (0,) * (nd - 1),
                        pipeline_mode=pl.Buffered(1))


def _layer_call(x, sinks, consts, seq_len, layer, final):
    M, D = x.shape
    T = MIX_T
    nblk = M // T
    return pl.pallas_call(
        functools.partial(_layer_kernel, layer=layer, final=final, blocks_per_seq=seq_len // T),
        out_shape=jax.ShapeDtypeStruct((M, D), F32),
        grid=(nblk + 1,),
        in_specs=[pl.BlockSpec(memory_space=pltpu.SMEM),
                  pl.BlockSpec((T, D), lambda i: (jnp.minimum(i, nblk - 1), 0))]
                 + [_layer_spec(c, layer) for c in consts],
        out_specs=pl.BlockSpec((T, D), lambda i: (jnp.maximum(i - 1, 0), 0)),
        scratch_shapes=[
            pltpu.VMEM((KV_WIDTH, WINDOW + T), BF16),
            pltpu.VMEM((WINDOW + T, KV_WIDTH), BF16),
            pltpu.VMEM((WINDOW + T, KV_WIDTH), BF16),
            pltpu.VMEM((CONV_HALO + T, CONV_WIDTH), F32),
            pltpu.VMEM((LRU_HALO + T, LRU_WIDTH), F32),
            pltpu.VMEM((SUBLANES, LRU_WIDTH), F32),
            pltpu.VMEM((2, WINDOW, 2 * WINDOW), F32),
            pltpu.VMEM((LRU_WIDTH // LANES, SUBLANES * SCAN_PITCH, LANES), F32),
            pltpu.VMEM((LRU_WIDTH // LANES, SUBLANES * SCAN_PITCH, LANES), F32),
            pltpu.VMEM((T, IN_WIDTH), F32),
            pltpu.VMEM((T, D), F32),
            pltpu.VMEM((T, D), BF16),
            pltpu.VMEM((T, D), F32),
            pltpu.VMEM((2, T, FF_CHUNK), BF16),
        ],
        compiler_params=pltpu.CompilerParams(
            dimension_semantics=("arbitrary",), vmem_limit_bytes=VMEM_LIMIT),
        name="layer_final" if final else "layer",
    )(sinks, x, *consts)


def _block_diag(w):
    dp, hN, di, dj = w.shape
    eye = jnp.eye(hN, dtype=w.dtype)
    return (eye[None, :, None, :, None] * w[:, :, :, None, :]).reshape(dp, hN * di, hN * dj)


def kernel(x, norm1, w_in, attn_sinks, conv_dw_w, conv_dw_b, conv_ln_g, conv_ln_b, lru_conv_w, lru_conv_b,
           lru_wa, lru_ba, lru_wx, lru_bx, lru_lambda, mix_norm, w_out, norm2, w_up, w_down, final_norm):
    B, S, D = x.shape
    depth = w_in.shape[0]
    qscale = jnp.concatenate([jnp.full((ATTN_WIDTH,), HEAD_DIM ** -0.5, F32),
                              jnp.ones((IN_WIDTH - ATTN_WIDTH,), F32)])
    win = (w_in * qscale).astype(BF16)
    wg = jnp.concatenate([_block_diag(lru_wa), _block_diag(lru_wx)], axis=2).astype(BF16)
    vec = jnp.concatenate(
        [norm1, conv_dw_b, conv_ln_g, conv_ln_b, lru_conv_b, lru_ba.reshape(depth, -1), lru_bx.reshape(depth, -1),
         lru_lambda, mix_norm, norm2, jnp.broadcast_to(final_norm[None], (depth, D))], axis=1)[:, None, :]
    consts = (vec, win, conv_dw_w, lru_conv_w, wg, w_out.astype(BF16), w_up.astype(BF16), w_down.astype(BF16))
    h = x.reshape(B * S, D)
    for l in range(depth):
        h = _layer_call(h, attn_sinks, consts, S, layer=l, final=(l == depth - 1))
    return h.reshape(B, S, D)
```

```python
import functools

import jax
import jax.numpy as jnp
from jax import lax
from jax.experimental import pallas as pl
from jax.experimental.pallas import tpu as pltpu

D_MODEL = 1024
HEAD_DIM = 64
ATTN_WIDTH = 512
N_Q_HEADS = 8
N_KV_HEADS = 2
KV_WIDTH = 128
WINDOW = 128
CONV_WIDTH = 256
CONV_KERNEL = 31
LRU_WIDTH = 256
LRU_HEADS = 4
LRU_HEAD_DIM = 64
LRU_CONV_KERNEL = 4
LRU_C = 8.0
MIX_WIDTH = 1024
IN_WIDTH = 1792
D_FF = 4096
RMS_EPS = 1e-6
LN_EPS = 1e-5
MASK_VALUE = -1e30

Q_OFF, K_OFF, V_OFF, CV_OFF, CG_OFF, RX_OFF, RG_OFF = 0, 512, 640, 768, 1024, 1280, 1536
YA_OFF, YC_OFF, YL_OFF = 0, ATTN_WIDTH, ATTN_WIDTH + CONV_WIDTH

VEC_NAMES = ("norm1", "conv_b", "ln_g", "ln_b", "lru_conv_b", "gate_b", "lambda", "mix_norm", "norm2", "final_norm")
VEC_WIDTHS = dict(zip(VEC_NAMES, (D_MODEL, CONV_WIDTH, CONV_WIDTH, CONV_WIDTH, LRU_WIDTH, 2 * LRU_WIDTH,
                                  LRU_WIDTH, MIX_WIDTH, D_MODEL, D_MODEL)))
VEC_OFF = {name: sum(VEC_WIDTHS[n] for n in VEC_NAMES[:k]) for k, name in enumerate(VEC_NAMES)}

LANES = 128
SUBLANES = 8
MXU_COLS = 256

MIX_T = 512
SUB = WINDOW
N_SUB = MIX_T // SUB
FF_CHUNK = D_FF // N_SUB
FF_PIECES = FF_CHUNK // MXU_COLS
DN_PIECES = D_MODEL // MXU_COLS
FFN_SPLIT = 1
CONV_HALO = 32
LRU_HALO = 8
SCAN_LEN = SUB // SUBLANES
SCAN_PITCH = SCAN_LEN + SUBLANES
VMEM_LIMIT = 56 * 1024 * 1024

F32 = jnp.float32
BF16 = jnp.bfloat16


def _sigmoid(x):
    return 0.5 * jnp.tanh(0.5 * x) + 0.5


def _rms(x, g):
    return x * lax.rsqrt(jnp.mean(x * x, axis=-1, keepdims=True) + RMS_EPS) * g


class _Vec:
    def __init__(self, ref, name):
        self.ref, self.off, self.width = ref, VEC_OFF[name], VEC_WIDTHS[name]

    def __getitem__(self, idx):
        if idx is Ellipsis:
            return self.ref[:, self.off:self.off + self.width]
        rows, cols = idx
        return self.ref[rows, self.off + cols.start:self.off + cols.stop]


def _layer_kernel(sinks_ref, x_ref, vec_ref, win_ref, cw_ref, lcw_ref, wg_ref, wout_ref, wup_ref, wdn_ref,
                  o_ref,
                  kt_ext, v_ext, vr_ext, u_ext, xl_ext, hcar, bias_s, a_s, b_s,
                  z_s, hmid, fhn_s, fres_s, fu_s,
                  *, layer, final, blocks_per_seq, nblk):
    T = MIX_T
    i = pl.program_id(0)
    s = lax.rem(i, blocks_per_seq)
    n1_ref, cb_ref, lng_ref, lnb_ref, lcb_ref, bg_ref, lam_ref, mixg_ref, n2_ref, fn_ref = (
        _Vec(vec_ref, name) for name in VEC_NAMES)

    @pl.when(s == 0)
    def _():
        kt_ext[:, 0:WINDOW] = jnp.zeros((KV_WIDTH, WINDOW), BF16)
        v_ext[0:WINDOW, :] = jnp.zeros((WINDOW, KV_WIDTH), BF16)
        vr_ext[0:WINDOW, :] = jnp.zeros((WINDOW, KV_WIDTH), BF16)
        u_ext[0:CONV_HALO, :] = jnp.zeros((CONV_HALO, CONV_WIDTH), F32)
        xl_ext[0:LRU_HALO, :] = jnp.zeros((LRU_HALO, LRU_WIDTH), F32)
        hcar[...] = jnp.zeros((SUBLANES, LRU_WIDTH), F32)

    qi = lax.broadcasted_iota(jnp.int32, (WINDOW, 2 * WINDOW), 0)
    kc = lax.broadcasted_iota(jnp.int32, (WINDOW, 2 * WINDOW), 1)
    band = (kc > qi) & (kc <= qi + WINDOW)
    bias_s[0] = jnp.where(band, 0.0, MASK_VALUE).astype(F32)
    bias_s[1] = jnp.where(band & (kc >= WINDOW), 0.0, MASK_VALUE).astype(F32)

    mg = mixg_ref[...]
    lane = lax.broadcasted_iota(jnp.int32, (2 * WINDOW, KV_WIDTH), 1)
    band_row = lax.broadcasted_iota(jnp.int32, (2 * WINDOW, KV_WIDTH), 0)
    keep_lo = (lane < HEAD_DIM) & (band_row != 0)
    drop_hi = (lane < HEAD_DIM) | (band_row == 0)
    lane_o = lax.broadcasted_iota(jnp.int32, (WINDOW, LANES), 1)
    zk = jnp.zeros((HEAD_DIM, 2 * WINDOW), BF16)
    row8 = lax.broadcasted_iota(jnp.int32, (SUBLANES, LANES), 0)
    L, P = SCAN_LEN, SCAN_PITCH

    def ffn_prep(r):
        rows = slice(r * SUB, (r + 1) * SUB)
        h = hmid[rows, :]
        fhn_s[rows, :] = _rms(h, n2_ref[...]).astype(BF16)
        fres_s[rows, :] = h

    def ffn_up(c, p):
        n0 = (p % 2) * 2 * MXU_COLS
        for q in range(FFN_SPLIT):
            r0 = (p // 2) * (T // 2) + q * (T // 2 // FFN_SPLIT)
            rows = slice(r0, r0 + T // 2 // FFN_SPLIT)
            u = jnp.dot(fhn_s[rows, :], wup_ref[:, c * FF_CHUNK + n0:c * FF_CHUNK + n0 + 2 * MXU_COLS],
                        preferred_element_type=F32).astype(BF16)
            u = jnp.maximum(u, 0.0)
            fu_s[c % 2, rows, n0:n0 + 2 * MXU_COLS] = u * u

    def ffn_down(c, p):
        cols = slice((p % 2) * 2 * MXU_COLS, (p % 2 + 1) * 2 * MXU_COLS)
        for q in range(FFN_SPLIT):
            r0 = (p // 2) * (T // 2) + q * (T // 2 // FFN_SPLIT)
            rows = slice(r0, r0 + T // 2 // FFN_SPLIT)
            d = jnp.dot(fu_s[c % 2, rows, :], wdn_ref[c * FF_CHUNK:(c + 1) * FF_CHUNK, cols],
                        preferred_element_type=F32)
            if c == 0:
                o_ref[rows, cols] = fres_s[rows, cols] + d
            else:
                o_ref[rows, cols] += d

    def ffn_finish(half):
        if final:
            rows = slice(half * (T // 2), (half + 1) * (T // 2))
            o_ref[rows, :] = _rms(o_ref[rows, :], fn_ref[...])

    def inproj(r):
        r0 = r * SUB
        hn = _rms(x_ref[r0:r0 + SUB, :], n1_ref[...]).astype(BF16)
        z_s[r0:r0 + SUB, :] = jnp.dot(hn, win_ref[...], preferred_element_type=F32)

    def attn_qk(r, st):
        r0 = r * SUB
        q = z_s[r0:r0 + SUB, Q_OFF:Q_OFF + ATTN_WIDTH].astype(BF16)
        kf = z_s[r0:r0 + SUB, K_OFF:K_OFF + KV_WIDTH]
        vf = z_s[r0:r0 + SUB, V_OFF:V_OFF + KV_WIDTH]
        kt_ext[:, WINDOW + r0:WINDOW + r0 + SUB] = kf.T.astype(BF16)
        v_ext[WINDOW + r0:WINDOW + r0 + SUB, :] = vf.astype(BF16)
        vr_ext[WINDOW + r0:WINDOW + r0 + SUB, :] = pltpu.roll(vf, HEAD_DIM, axis=1).astype(BF16)
        kt_band = kt_ext[:, r0:r0 + 2 * WINDOW]
        st["sc"] = []
        for h in range(N_KV_HEADS):
            kh = kt_band[h * HEAD_DIM:(h + 1) * HEAD_DIM, :]
            kt2 = jnp.concatenate(
                [jnp.concatenate([kh, zk], axis=0), jnp.concatenate([zk, kh], axis=0)], axis=1)
            c0 = h * 2 * LANES
            qs = jnp.concatenate([q[:, c0:c0 + LANES], q[:, c0 + LANES:c0 + 2 * LANES]], axis=0)
            st["sc"].append(jnp.dot(qs, kt2, preferred_element_type=F32))

    def attn_softmax(r, h, st):
        bias = bias_s[jnp.where(s == 0, 1, 0)] if r == 0 else bias_s[0]
        sc = st["sc"][h]
        p_rows = []
        dens = []
        for rr in range(2):
            p_cols = []
            den_r = []
            for e in range(2):
                sink = sinks_ref[layer, 4 * h + 2 * rr + e]
                rws = slice(rr * WINDOW, (rr + 1) * WINDOW)
                c0 = e * 2 * WINDOW
                t = jnp.concatenate(
                    [jnp.where(lane_o == 0, sink, sc[rws, c0:c0 + LANES] + bias[:, :LANES]),
                     sc[rws, c0 + LANES:c0 + 2 * LANES] + bias[:, LANES:]], axis=1)
                p = jnp.exp(t - jnp.max(t, axis=-1, keepdims=True))
                den_r.append(jnp.sum(p, axis=-1, keepdims=True))
                p_cols.append(p.astype(BF16))
            p_rows.append(jnp.concatenate(p_cols, axis=1))
            dens.append(den_r)
        st[("pm", h)] = jnp.concatenate(p_rows, axis=0)
        st[("den", h)] = dens

    def attn_pv(r, h, st):
        r0 = r * SUB
        v_band = v_ext[r0:r0 + 2 * WINDOW, :]
        vr_band = vr_ext[r0:r0 + 2 * WINDOW, :]
        if h == 0:
            va = jnp.where(keep_lo, v_band, jnp.zeros_like(v_band))
            vb = jnp.where(drop_hi, jnp.zeros_like(vr_band), vr_band)
        else:
            va = jnp.where(keep_lo, vr_band, jnp.zeros_like(vr_band))
            vb = jnp.where(drop_hi, jnp.zeros_like(v_band), v_band)
        v2 = jnp.concatenate([va, vb], axis=0)
        o = jnp.dot(st[("pm", h)], v2, preferred_element_type=F32)
        dens = st[("den", h)]
        for rr in range(2):
            den = jnp.where(lane_o < HEAD_DIM, dens[rr][0], dens[rr][1])
            st[("ya", 2 * h + rr)] = o[rr * WINDOW:(rr + 1) * WINDOW, :] / den

    def attn_norm(r, st):
        y_attn = jnp.concatenate([st[("ya", c)] for c in range(ATTN_WIDTH // LANES)], axis=1)
        st["ya_n"] = _rms(y_attn, mg[:, YA_OFF:YA_OFF + ATTN_WIDTH]).astype(BF16)

    def attn_dot(r, st):
        rows = slice(r * SUB, (r + 1) * SUB)
        hmid[rows, :] = x_ref[rows, :] + jnp.dot(st["ya_n"], wout_ref[YA_OFF:YA_OFF + ATTN_WIDTH, :],
                                                 preferred_element_type=F32)

    def conv_unit(r, lt, st):
        r0 = r * SUB
        cl = slice(lt * LANES, (lt + 1) * LANES)
        cval = z_s[r0:r0 + SUB, CV_OFF + lt * LANES:CV_OFF + (lt + 1) * LANES]
        cgate = z_s[r0:r0 + SUB, CG_OFF + lt * LANES:CG_OFF + (lt + 1) * LANES]
        u_ext[CONV_HALO + r0:CONV_HALO + r0 + SUB, cl] = cval * _sigmoid(cgate)
        base = CONV_HALO - (CONV_KERNEL - 1)
        acc = jnp.broadcast_to(cb_ref[:, cl], (SUB, LANES))
        for sh in range(SUBLANES):
            part = None
            nrows = SUB + (SUBLANES if sh else 0)
            for k in range(CONV_KERNEL):
                if (base + k) % SUBLANES != sh:
                    continue
                al = r0 + base + k - sh
                term = cw_ref[k:k + 1, cl] * u_ext[al:al + nrows, cl]
                part = term if part is None else part + term
            acc = acc + part[sh:sh + SUB, :]
        st[("yc", lt)] = acc

    def conv_post(r, st):
        uc = jnp.concatenate([st[("yc", lt)] for lt in range(CONV_WIDTH // LANES)], axis=1)
        mu = jnp.mean(uc, axis=-1, keepdims=True)
        xc_ = uc - mu
        ln = xc_ * lax.rsqrt(jnp.mean(xc_ * xc_, axis=-1, keepdims=True) + LN_EPS) * lng_ref[...] + lnb_ref[...]
        y_conv = ln * _sigmoid(ln)
        st["yc_n"] = _rms(y_conv, mg[:, YC_OFF:YC_OFF + CONV_WIDTH]).astype(BF16)

    def conv_dot(r, st):
        hmid[r * SUB:(r + 1) * SUB, :] += jnp.dot(st["yc_n"], wout_ref[YC_OFF:YC_OFF + CONV_WIDTH, :],
                                                  preferred_element_type=F32)

    def lru_pre(r, st):
        r0 = r * SUB
        xl_ext[LRU_HALO + r0:LRU_HALO + r0 + SUB, :] = z_s[r0:r0 + SUB, RX_OFF:RX_OFF + LRU_WIDTH]
        lbase = LRU_HALO - (LRU_CONV_KERNEL - 1)
        xc = jnp.broadcast_to(lcb_ref[...], (SUB, LRU_WIDTH))
        for k in range(LRU_CONV_KERNEL):
            xc = xc + lcw_ref[k:k + 1, :] * xl_ext[r0 + lbase + k:r0 + lbase + k + SUB, :]
        st["xc"] = xc
        st["xc_b"] = xc.astype(BF16)

    def lru_gate_dot(r, st):
        st["gates"] = jnp.dot(st["xc_b"], wg_ref[...], preferred_element_type=F32) + bg_ref[...]

    def lru_scan(r, st):
        r0 = r * SUB
        xc, gates = st["xc"], st["gates"]
        rgate = _sigmoid(gates[:, :LRU_WIDTH])
        igate = _sigmoid(gates[:, LRU_WIDTH:])
        log_a = (-LRU_C * rgate) * jax.nn.softplus(-lam_ref[...])
        a_full = jnp.exp(log_a)
        th = jnp.tanh(log_a)
        b_full = jnp.sqrt(2.0 * th / (th - 1.0)) * (igate * xc)
        yl_cols = []
        for lt in range(LRU_WIDTH // LANES):
            cl = slice(lt * LANES, (lt + 1) * LANES)
            for c in range(SUBLANES):
                a_s[lt, c * P:c * P + L, :] = a_full[c * L:(c + 1) * L, cl]
                b_s[lt, c * P:c * P + L, :] = b_full[c * L:(c + 1) * L, cl]
            hloc = jnp.zeros((SUBLANES, LANES), F32)
            cum = jnp.ones((SUBLANES, LANES), F32)
            for m in range(L):
                am = a_s[lt, pl.ds(m, SUBLANES, stride=P), :]
                bm = b_s[lt, pl.ds(m, SUBLANES, stride=P), :]
                hloc = am * hloc + bm
                cum = am * cum
                b_s[lt, pl.ds(m, SUBLANES, stride=P), :] = hloc
                a_s[lt, pl.ds(m, SUBLANES, stride=P), :] = cum
            ca, cbv = cum, hloc
            for d in (1, 2, 4):
                a_sh = jnp.where(row8 >= d, pltpu.roll(ca, d, axis=0), 1.0)
                b_sh = jnp.where(row8 >= d, pltpu.roll(cbv, d, axis=0), 0.0)
                cbv = ca * b_sh + cbv
                ca = ca * a_sh
            hprev = hcar[:, cl]
            ends = ca * hprev + cbv
            carry_in = jnp.where(row8 == 0, hprev, pltpu.roll(ends, 1, axis=0))
            hcar[:, cl] = jnp.broadcast_to(ends[SUBLANES - 1:SUBLANES, :], (SUBLANES, LANES))
            parts = []
            for c in range(SUBLANES):
                g = jnp.broadcast_to(carry_in[c:c + 1, :], (L, LANES))
                parts.append(b_s[lt, c * P:c * P + L, :] + a_s[lt, c * P:c * P + L, :] * g)
            yl_cols.append(jnp.concatenate(parts, axis=0))
        y_lru = jnp.concatenate(yl_cols, axis=1) * jax.nn.gelu(z_s[r0:r0 + SUB, RG_OFF:RG_OFF + LRU_WIDTH])
        st["yl_n"] = _rms(y_lru, mg[:, YL_OFF:YL_OFF + LRU_WIDTH]).astype(BF16)

    def lru_dot(r, st):
        hmid[r * SUB:(r + 1) * SUB, :] += jnp.dot(st["yl_n"], wout_ref[YL_OFF:YL_OFF + LRU_WIDTH, :],
                                                  preferred_element_type=F32)

    def emit(run_ffn, run_mixer):
        pieces = []
        if run_ffn:
            for c in range(N_SUB):
                pieces += [functools.partial(ffn_up, c, p) for p in range(4)]
                pieces += [functools.partial(ffn_down, c, p) for p in range(4)]
        n_emitted = [0]
        n_prepped = [0]

        def next_ffn(mixer_rows_done):
            k = n_emitted[0]
            if k >= len(pieces):
                return
            pieces[k]()
            n_emitted[0] += 1
            if run_mixer and k >= N_SUB * 8 - 5:
                while n_prepped[0] < mixer_rows_done:
                    ffn_prep(n_prepped[0])
                    n_prepped[0] += 1
            if k == N_SUB * 8 - 3:
                ffn_finish(0)
            if k == N_SUB * 8 - 1:
                ffn_finish(1)

        if not run_mixer:
            for _ in pieces:
                next_ffn(0)
            return

        next_ffn(0)
        inproj(0)
        for r in range(N_SUB):
            st = {}
            attn_qk(r, st)
            next_ffn(r)
            attn_softmax(r, 0, st)
            attn_softmax(r, 1, st)
            next_ffn(r)
            attn_pv(r, 0, st)
            next_ffn(r)
            attn_pv(r, 1, st)
            attn_norm(r, st)
            conv_unit(r, 0, st)
            next_ffn(r)
            attn_dot(r, st)
            conv_unit(r, 1, st)
            next_ffn(r)
            conv_post(r, st)
            lru_pre(r, st)
            next_ffn(r)
            conv_dot(r, st)
            lru_gate_dot(r, st)
            if r + 1 < N_SUB:
                inproj(r + 1)
            lru_scan(r, st)
            next_ffn(r)
            lru_dot(r, st)
            if r + 1 < N_SUB:
                next_ffn(r + 1)
        while n_prepped[0] < N_SUB:
            ffn_prep(n_prepped[0])
            n_prepped[0] += 1

        kt_ext[:, 0:WINDOW] = kt_ext[:, T:T + WINDOW]
        v_ext[0:WINDOW, :] = v_ext[T:T + WINDOW, :]
        vr_ext[0:WINDOW, :] = vr_ext[T:T + WINDOW, :]
        u_ext[0:CONV_HALO, :] = u_ext[T:T + CONV_HALO, :]
        xl_ext[0:LRU_HALO, :] = xl_ext[T:T + LRU_HALO, :]

    pl.when(i == 0)(functools.partial(emit, False, True))
    pl.when(jnp.logical_and(i > 0, i < nblk))(functools.partial(emit, True, True))
    pl.when(i == nblk)(functools.partial(emit, True, False))


def _layer_spec(arr, layer):
    nd = arr.ndim
    return pl.BlockSpec((None,) + arr.shape[1:], lambda i: (layer,) + (0,) * (nd - 1),
                        pipeline_mode=pl.Buffered(1))


def _layer_call(x, sinks, consts, seq_len, layer, final):
    M, D = x.shape
    T = MIX_T
    nblk = M // T
    return pl.pallas_call(
        functools.partial(_layer_kernel, layer=layer, final=final, blocks_per_seq=seq_len // T, nblk=nblk),
        out_shape=jax.ShapeDtypeStruct((M, D), F32),
        grid=(nblk + 1,),
        in_specs=[pl.BlockSpec(memory_space=pltpu.SMEM),
                  pl.BlockSpec((T, D), lambda i: (jnp.minimum(i, nblk - 1), 0))]
                 + [_layer_spec(c, layer) for c in consts],
        out_specs=pl.BlockSpec((T, D), lambda i: (jnp.maximum(i - 1, 0), 0)),
        scratch_shapes=[
            pltpu.VMEM((KV_WIDTH, WINDOW + T), BF16),
            pltpu.VMEM((WINDOW + T, KV_WIDTH), BF16),
            pltpu.VMEM((WINDOW + T, KV_WIDTH), BF16),
            pltpu.VMEM((CONV_HALO + T, CONV_WIDTH), F32),
            pltpu.VMEM((LRU_HALO + T, LRU_WIDTH), F32),
            pltpu.VMEM((SUBLANES, LRU_WIDTH), F32),
            pltpu.VMEM((2, WINDOW, 2 * WINDOW), F32),
            pltpu.VMEM((LRU_WIDTH // LANES, SUBLANES * SCAN_PITCH, LANES), F32),
            pltpu.VMEM((LRU_WIDTH // LANES, SUBLANES * SCAN_PITCH, LANES), F32),
            pltpu.VMEM((T, IN_WIDTH), F32),
            pltpu.VMEM((T, D), F32),
            pltpu.VMEM((T, D), BF16),
            pltpu.VMEM((T, D), F32),
            pltpu.VMEM((2, T, FF_CHUNK), BF16),
        ],
        compiler_params=pltpu.CompilerParams(
            dimension_semantics=("arbitrary",), vmem_limit_bytes=VMEM_LIMIT),
        name="layer_final" if final else "layer",
    )(sinks, x, *consts)


def _block_diag(w):
    dp, hN, di, dj = w.shape
    eye = jnp.eye(hN, dtype=w.dtype)
    return (eye[None, :, None, :, None] * w[:, :, :, None, :]).reshape(dp, hN * di, hN * dj)


def kernel(x, norm1, w_in, attn_sinks, conv_dw_w, conv_dw_b, conv_ln_g, conv_ln_b, lru_conv_w, lru_conv_b,
           lru_wa, lru_ba, lru_wx, lru_bx, lru_lambda, mix_norm, w_out, norm2, w_up, w_down, final_norm):
    B, S, D = x.shape
    depth = w_in.shape[0]
    qscale = jnp.concatenate([jnp.full((ATTN_WIDTH,), HEAD_DIM ** -0.5, F32),
                              jnp.ones((IN_WIDTH - ATTN_WIDTH,), F32)])
    win = (w_in * qscale).astype(BF16)
    wg = jnp.concatenate([_block_diag(lru_wa), _block_diag(lru_wx)], axis=2).astype(BF16)
    vec = jnp.concatenate(
        [norm1, conv_dw_b, conv_ln_g, conv_ln_b, lru_conv_b, lru_ba.reshape(depth, -1), lru_bx.reshape(depth, -1),
         lru_lambda, mix_norm, norm2, jnp.broadcast_to(final_norm[None], (depth, D))], axis=1)[:, None, :]
    consts = (vec, win, conv_dw_w, lru_conv_w, wg, w_out.astype(BF16), w_up.astype(BF16), w_down.astype(BF16))
    h = x.reshape(B * S, D)
    for l in range(depth):
        h = _layer_call(h, attn_sinks, consts, S, layer=l, final=(l == depth - 1))
    return h.reshape(B, S, D)
```

```python
import functools

import jax
import jax.numpy as jnp
from jax import lax
from jax.experimental import pallas as pl
from jax.experimental.pallas import tpu as pltpu

D_MODEL = 1024
HEAD_DIM = 64
ATTN_WIDTH = 512
N_Q_HEADS = 8
N_KV_HEADS = 2
KV_WIDTH = 128
WINDOW = 128
CONV_WIDTH = 256
CONV_KERNEL = 31
LRU_WIDTH = 256
LRU_HEADS = 4
LRU_HEAD_DIM = 64
LRU_CONV_KERNEL = 4
LRU_C = 8.0
MIX_WIDTH = 1024
IN_WIDTH = 1792
D_FF = 4096
RMS_EPS = 1e-6
LN_EPS = 1e-5
MASK_VALUE = -1e30

Q_OFF, K_OFF, V_OFF, CV_OFF, CG_OFF, RX_OFF, RG_OFF = 0, 512, 640, 768, 1024, 1280, 1536
YA_OFF, YC_OFF, YL_OFF = 0, ATTN_WIDTH, ATTN_WIDTH + CONV_WIDTH

VEC_NAMES = ("norm1", "conv_b", "ln_g", "ln_b", "lru_conv_b", "gate_b", "lambda", "mix_norm", "norm2", "final_norm")
VEC_WIDTHS = dict(zip(VEC_NAMES, (D_MODEL, CONV_WIDTH, CONV_WIDTH, CONV_WIDTH, LRU_WIDTH, 2 * LRU_WIDTH,
                                  LRU_WIDTH, MIX_WIDTH, D_MODEL, D_MODEL)))
VEC_OFF = {name: sum(VEC_WIDTHS[n] for n in VEC_NAMES[:k]) for k, name in enumerate(VEC_NAMES)}

LANES = 128
SUBLANES = 8
MXU_COLS = 256

MIX_T = 512
SUB = WINDOW
N_SUB = MIX_T // SUB
FF_CHUNK = D_FF // N_SUB
FF_PIECES = FF_CHUNK // MXU_COLS
DN_PIECES = D_MODEL // MXU_COLS
FFN_SPLIT = 1
CONV_HALO = 32
LRU_HALO = 8
SCAN_LEN = SUB // SUBLANES
SCAN_PITCH = SCAN_LEN + SUBLANES
VMEM_LIMIT = 56 * 1024 * 1024

F32 = jnp.float32
BF16 = jnp.bfloat16


def _sigmoid(x):
    return 0.5 * jnp.tanh(0.5 * x) + 0.5


def _rms(x, g):
    return x * lax.rsqrt(jnp.mean(x * x, axis=-1, keepdims=True) + RMS_EPS) * g


class _Vec:
    def __init__(self, ref, name):
        self.ref, self.off, self.width = ref, VEC_OFF[name], VEC_WIDTHS[name]

    def __getitem__(self, idx):
        if idx is Ellipsis:
            return self.ref[:, self.off:self.off + self.width]
        rows, cols = idx
        return self.ref[rows, self.off + cols.start:self.off + cols.stop]


def _layer_kernel(sinks_ref, x_ref, vec_ref, win_ref, cw_ref, lcw_ref, wg_ref, wout_ref, wup_ref, wdn_ref,
                  o_ref,
                  kt_ext, v_ext, vr_ext, u_ext, xl_ext, hcar, bias_s, a_s, b_s,
                  z_s, hmid, fhn_s, fres_s, fu_s,
                  *, layer, final, blocks_per_seq, nblk):
    T = MIX_T
    i = pl.program_id(0)
    s = lax.rem(i, blocks_per_seq)
    n1_ref, cb_ref, lng_ref, lnb_ref, lcb_ref, bg_ref, lam_ref, mixg_ref, n2_ref, fn_ref = (
        _Vec(vec_ref, name) for name in VEC_NAMES)

    @pl.when(s == 0)
    def _():
        kt_ext[:, 0:WINDOW] = jnp.zeros((KV_WIDTH, WINDOW), BF16)
        v_ext[0:WINDOW, :] = jnp.zeros((WINDOW, KV_WIDTH), BF16)
        vr_ext[0:WINDOW, :] = jnp.zeros((WINDOW, KV_WIDTH), BF16)
        u_ext[0:CONV_HALO, :] = jnp.zeros((CONV_HALO, CONV_WIDTH), F32)
        xl_ext[0:LRU_HALO, :] = jnp.zeros((LRU_HALO, LRU_WIDTH), F32)
        hcar[...] = jnp.zeros((SUBLANES, LRU_WIDTH), F32)

    qi = lax.broadcasted_iota(jnp.int32, (WINDOW, 2 * WINDOW), 0)
    kc = lax.broadcasted_iota(jnp.int32, (WINDOW, 2 * WINDOW), 1)
    band = (kc > qi) & (kc <= qi + WINDOW)
    bias_s[0] = jnp.where(band, 0.0, MASK_VALUE).astype(F32)
    bias_s[1] = jnp.where(band & (kc >= WINDOW), 0.0, MASK_VALUE).astype(F32)

    mg = mixg_ref[...]
    lane = lax.broadcasted_iota(jnp.int32, (2 * WINDOW, KV_WIDTH), 1)
    band_row = lax.broadcasted_iota(jnp.int32, (2 * WINDOW, KV_WIDTH), 0)
    keep_lo = (lane < HEAD_DIM) & (band_row != 0)
    drop_hi = (lane < HEAD_DIM) | (band_row == 0)
    lane_o = lax.broadcasted_iota(jnp.int32, (WINDOW, LANES), 1)
    zk = jnp.zeros((HEAD_DIM, 2 * WINDOW), BF16)
    row8 = lax.broadcasted_iota(jnp.int32, (SUBLANES, LANES), 0)
    L, P = SCAN_LEN, SCAN_PITCH

    def ffn_prep(r):
        rows = slice(r * SUB, (r + 1) * SUB)
        h = hmid[rows, :]
        fhn_s[rows, :] = _rms(h, n2_ref[...]).astype(BF16)
        fres_s[rows, :] = h

    def ffn_up(c, p):
        n0 = (p % 2) * 2 * MXU_COLS
        for q in range(FFN_SPLIT):
            r0 = (p // 2) * (T // 2) + q * (T // 2 // FFN_SPLIT)
            rows = slice(r0, r0 + T // 2 // FFN_SPLIT)
            u = jnp.dot(fhn_s[rows, :], wup_ref[:, c * FF_CHUNK + n0:c * FF_CHUNK + n0 + 2 * MXU_COLS],
                        preferred_element_type=F32).astype(BF16)
            u = jnp.maximum(u, 0.0)
            fu_s[c % 2, rows, n0:n0 + 2 * MXU_COLS] = u * u

    def ffn_down(c, p):
        cols = slice((p % 2) * 2 * MXU_COLS, (p % 2 + 1) * 2 * MXU_COLS)
        for q in range(FFN_SPLIT):
            r0 = (p // 2) * (T // 2) + q * (T // 2 // FFN_SPLIT)
            rows = slice(r0, r0 + T // 2 // FFN_SPLIT)
            d = jnp.dot(fu_s[c % 2, rows, :], wdn_ref[c * FF_CHUNK:(c + 1) * FF_CHUNK, cols],
                        preferred_element_type=F32)
            if c == 0:
                o_ref[rows, cols] = fres_s[rows, cols] + d
            else:
                o_ref[rows, cols] += d

    def ffn_finish(half):
        if final:
            rows = slice(half * (T // 2), (half + 1) * (T // 2))
            o_ref[rows, :] = _rms(o_ref[rows, :], fn_ref[...])

    def inproj(half):
        rows = slice(half * 2 * SUB, (half + 1) * 2 * SUB)
        hn = _rms(x_ref[rows, :], n1_ref[...]).astype(BF16)
        z_s[rows, :] = jnp.dot(hn, win_ref[...], preferred_element_type=F32)

    def attn_qk(r, st):
        r0 = r * SUB
        q = z_s[r0:r0 + SUB, Q_OFF:Q_OFF + ATTN_WIDTH].astype(BF16)
        kf = z_s[r0:r0 + SUB, K_OFF:K_OFF + KV_WIDTH]
        vf = z_s[r0:r0 + SUB, V_OFF:V_OFF + KV_WIDTH]
        kt_ext[:, WINDOW + r0:WINDOW + r0 + SUB] = kf.T.astype(BF16)
        v_ext[WINDOW + r0:WINDOW + r0 + SUB, :] = vf.astype(BF16)
        vr_ext[WINDOW + r0:WINDOW + r0 + SUB, :] = pltpu.roll(vf, HEAD_DIM, axis=1).astype(BF16)
        kt_band = kt_ext[:, r0:r0 + 2 * WINDOW]
        st["sc"] = []
        for h in range(N_KV_HEADS):
            kh = kt_band[h * HEAD_DIM:(h + 1) * HEAD_DIM, :]
            kt2 = jnp.concatenate(
                [jnp.concatenate([kh, zk], axis=0), jnp.concatenate([zk, kh], axis=0)], axis=1)
            c0 = h * 2 * LANES
            qs = jnp.concatenate([q[:, c0:c0 + LANES], q[:, c0 + LANES:c0 + 2 * LANES]], axis=0)
            st["sc"].append(jnp.dot(qs, kt2, preferred_element_type=F32))

    def attn_softmax(r, h, st):
        bias = bias_s[jnp.where(s == 0, 1, 0)] if r == 0 else bias_s[0]
        sc = st["sc"][h]
        p_rows = []
        dens = []
        for rr in range(2):
            p_cols = []
            den_r = []
            for e in range(2):
                sink = sinks_ref[layer, 4 * h + 2 * rr + e]
                rws = slice(rr * WINDOW, (rr + 1) * WINDOW)
                c0 = e * 2 * WINDOW
                t = jnp.concatenate(
                    [jnp.where(lane_o == 0, sink, sc[rws, c0:c0 + LANES] + bias[:, :LANES]),
                     sc[rws, c0 + LANES:c0 + 2 * LANES] + bias[:, LANES:]], axis=1)
                p = jnp.exp(t - jnp.max(t, axis=-1, keepdims=True))
                den_r.append(jnp.sum(p, axis=-1, keepdims=True))
                p_cols.append(p.astype(BF16))
            p_rows.append(jnp.concatenate(p_cols, axis=1))
            dens.append(den_r)
        st[("pm", h)] = jnp.concatenate(p_rows, axis=0)
        st[("den", h)] = dens

    def attn_pv(r, h, st):
        r0 = r * SUB
        v_band = v_ext[r0:r0 + 2 * WINDOW, :]
        vr_band = vr_ext[r0:r0 + 2 * WINDOW, :]
        if h == 0:
            va = jnp.where(keep_lo, v_band, jnp.zeros_like(v_band))
            vb = jnp.where(drop_hi, jnp.zeros_like(vr_band), vr_band)
        else:
            va = jnp.where(keep_lo, vr_band, jnp.zeros_like(vr_band))
            vb = jnp.where(drop_hi, jnp.zeros_like(v_band), v_band)
        v2 = jnp.concatenate([va, vb], axis=0)
        o = jnp.dot(st[("pm", h)], v2, preferred_element_type=F32)
        dens = st[("den", h)]
        for rr in range(2):
            den = jnp.where(lane_o < HEAD_DIM, dens[rr][0], dens[rr][1])
            st[("ya", 2 * h + rr)] = o[rr * WINDOW:(rr + 1) * WINDOW, :] / den

    def attn_norm(r, st):
        y_attn = jnp.concatenate([st[("ya", c)] for c in range(ATTN_WIDTH // LANES)], axis=1)
        st["ya_n"] = _rms(y_attn, mg[:, YA_OFF:YA_OFF + ATTN_WIDTH]).astype(BF16)

    def attn_dot(r, st):
        rows = slice(r * SUB, (r + 1) * SUB)
        hmid[rows, :] = x_ref[rows, :] + jnp.dot(st["ya_n"], wout_ref[YA_OFF:YA_OFF + ATTN_WIDTH, :],
                                                 preferred_element_type=F32)

    def conv_unit(r, lt, st):
        r0 = r * SUB
        cl = slice(lt * LANES, (lt + 1) * LANES)
        cval = z_s[r0:r0 + SUB, CV_OFF + lt * LANES:CV_OFF + (lt + 1) * LANES]
        cgate = z_s[r0:r0 + SUB, CG_OFF + lt * LANES:CG_OFF + (lt + 1) * LANES]
        u_ext[CONV_HALO + r0:CONV_HALO + r0 + SUB, cl] = cval * _sigmoid(cgate)
        base = CONV_HALO - (CONV_KERNEL - 1)
        acc = jnp.broadcast_to(cb_ref[:, cl], (SUB, LANES))
        for sh in range(SUBLANES):
            part = None
            nrows = SUB + (SUBLANES if sh else 0)
            for k in range(CONV_KERNEL):
                if (base + k) % SUBLANES != sh:
                    continue
                al = r0 + base + k - sh
                term = cw_ref[k:k + 1, cl] * u_ext[al:al + nrows, cl]
                part = term if part is None else part + term
            acc = acc + part[sh:sh + SUB, :]
        st[("yc", lt)] = acc

    def conv_post(r, st):
        uc = jnp.concatenate([st[("yc", lt)] for lt in range(CONV_WIDTH // LANES)], axis=1)
        mu = jnp.mean(uc, axis=-1, keepdims=True)
        xc_ = uc - mu
        ln = xc_ * lax.rsqrt(jnp.mean(xc_ * xc_, axis=-1, keepdims=True) + LN_EPS) * lng_ref[...] + lnb_ref[...]
        y_conv = ln * _sigmoid(ln)
        st["yc_n"] = _rms(y_conv, mg[:, YC_OFF:YC_OFF + CONV_WIDTH]).astype(BF16)

    def conv_dot(r, st):
        hmid[r * SUB:(r + 1) * SUB, :] += jnp.dot(st["yc_n"], wout_ref[YC_OFF:YC_OFF + CONV_WIDTH, :],
                                                  preferred_element_type=F32)

    def lru_pre(r, st):
        r0 = r * SUB
        xl_ext[LRU_HALO + r0:LRU_HALO + r0 + SUB, :] = z_s[r0:r0 + SUB, RX_OFF:RX_OFF + LRU_WIDTH]
        lbase = LRU_HALO - (LRU_CONV_KERNEL - 1)
        xc = jnp.broadcast_to(lcb_ref[...], (SUB, LRU_WIDTH))
        for k in range(LRU_CONV_KERNEL):
            xc = xc + lcw_ref[k:k + 1, :] * xl_ext[r0 + lbase + k:r0 + lbase + k + SUB, :]
        st["xc"] = xc
        st["xc_b"] = xc.astype(BF16)

    def lru_gate_dot(r, st):
        st["gates"] = jnp.dot(st["xc_b"], wg_ref[...], preferred_element_type=F32) + bg_ref[...]

    def lru_scan(r, st):
        r0 = r * SUB
        xc, gates = st["xc"], st["gates"]
        rgate = _sigmoid(gates[:, :LRU_WIDTH])
        igate = _sigmoid(gates[:, LRU_WIDTH:])
        log_a = (-LRU_C * rgate) * jax.nn.softplus(-lam_ref[...])
        a_full = jnp.exp(log_a)
        th = jnp.tanh(log_a)
        b_full = jnp.sqrt(2.0 * th / (th - 1.0)) * (igate * xc)
        yl_cols = []
        for lt in range(LRU_WIDTH // LANES):
            cl = slice(lt * LANES, (lt + 1) * LANES)
            for c in range(SUBLANES):
                a_s[lt, c * P:c * P + L, :] = a_full[c * L:(c + 1) * L, cl]
                b_s[lt, c * P:c * P + L, :] = b_full[c * L:(c + 1) * L, cl]
            hloc = jnp.zeros((SUBLANES, LANES), F32)
            cum = jnp.ones((SUBLANES, LANES), F32)
            for m in range(L):
                am = a_s[lt, pl.ds(m, SUBLANES, stride=P), :]
                bm = b_s[lt, pl.ds(m, SUBLANES, stride=P), :]
                hloc = am * hloc + bm
                cum = am * cum
                b_s[lt, pl.ds(m, SUBLANES, stride=P), :] = hloc
                a_s[lt, pl.ds(m, SUBLANES, stride=P), :] = cum
            ca, cbv = cum, hloc
            for d in (1, 2, 4):
                a_sh = jnp.where(row8 >= d, pltpu.roll(ca, d, axis=0), 1.0)
                b_sh = jnp.where(row8 >= d, pltpu.roll(cbv, d, axis=0), 0.0)
                cbv = ca * b_sh + cbv
                ca = ca * a_sh
            hprev = hcar[:, cl]
            ends = ca * hprev + cbv
            carry_in = jnp.where(row8 == 0, hprev, pltpu.roll(ends, 1, axis=0))
            hcar[:, cl] = jnp.broadcast_to(ends[SUBLANES - 1:SUBLANES, :], (SUBLANES, LANES))
            parts = []
            for c in range(SUBLANES):
                g = jnp.broadcast_to(carry_in[c:c + 1, :], (L, LANES))
                parts.append(b_s[lt, c * P:c * P + L, :] + a_s[lt, c * P:c * P + L, :] * g)
            yl_cols.append(jnp.concatenate(parts, axis=0))
        y_lru = jnp.concatenate(yl_cols, axis=1) * jax.nn.gelu(z_s[r0:r0 + SUB, RG_OFF:RG_OFF + LRU_WIDTH])
        st["yl_n"] = _rms(y_lru, mg[:, YL_OFF:YL_OFF + LRU_WIDTH]).astype(BF16)

    def lru_dot(r, st):
        hmid[r * SUB:(r + 1) * SUB, :] += jnp.dot(st["yl_n"], wout_ref[YL_OFF:YL_OFF + LRU_WIDTH, :],
                                                  preferred_element_type=F32)

    def emit(run_ffn, run_mixer):
        pieces = []
        if run_ffn:
            for c in range(N_SUB):
                pieces += [functools.partial(ffn_up, c, p) for p in range(4)]
                pieces += [functools.partial(ffn_down, c, p) for p in range(4)]
        n_emitted = [0]
        n_prepped = [0]

        def next_ffn(mixer_rows_done):
            k = n_emitted[0]
            if k >= len(pieces):
                return
            pieces[k]()
            n_emitted[0] += 1
            if run_mixer and k >= N_SUB * 8 - 5:
                while n_prepped[0] < mixer_rows_done:
                    ffn_prep(n_prepped[0])
                    n_prepped[0] += 1
            if k == N_SUB * 8 - 3:
                ffn_finish(0)
            if k == N_SUB * 8 - 1:
                ffn_finish(1)

        if not run_mixer:
            for _ in pieces:
                next_ffn(0)
            return

        next_ffn(0)
        inproj(0)
        for r in range(N_SUB):
            st = {}
            attn_qk(r, st)
            next_ffn(r)
            attn_softmax(r, 0, st)
            attn_softmax(r, 1, st)
            next_ffn(r)
            attn_pv(r, 0, st)
            next_ffn(r)
            attn_pv(r, 1, st)
            attn_norm(r, st)
            conv_unit(r, 0, st)
            next_ffn(r)
            attn_dot(r, st)
            conv_unit(r, 1, st)
            next_ffn(r)
            conv_post(r, st)
            lru_pre(r, st)
            next_ffn(r)
            conv_dot(r, st)
            lru_gate_dot(r, st)
            if r % 2 == 0 and r + 2 < N_SUB:
                inproj(r // 2 + 1)
            lru_scan(r, st)
            next_ffn(r)
            lru_dot(r, st)
            if r + 1 < N_SUB:
                next_ffn(r + 1)
        while n_prepped[0] < N_SUB:
            ffn_prep(n_prepped[0])
            n_prepped[0] += 1

        kt_ext[:, 0:WINDOW] = kt_ext[:, T:T + WINDOW]
        v_ext[0:WINDOW, :] = v_ext[T:T + WINDOW, :]
        vr_ext[0:WINDOW, :] = vr_ext[T:T + WINDOW, :]
        u_ext[0:CONV_HALO, :] = u_ext[T:T + CONV_HALO, :]
        xl_ext[0:LRU_HALO, :] = xl_ext[T:T + LRU_HALO, :]

    pl.when(i == 0)(functools.partial(emit, False, True))
    pl.when(jnp.logical_and(i > 0, i < nblk))(functools.partial(emit, True, True))
    pl.when(i == nblk)(functools.partial(emit, True, False))


def _layer_spec(arr, layer):
    nd = arr.ndim
    return pl.BlockSpec((None,) + arr.shape[1:], lambda i: (layer,) + (0,) * (nd - 1),
                        pipeline_mode=pl.Buffered(1))


def _layer_call(x, sinks, consts, seq_len, layer, final):
    M, D = x.shape
    T = MIX_T
    nblk = M // T
    return pl.pallas_call(
        functools.partial(_layer_kernel, layer=layer, final=final, blocks_per_seq=seq_len // T, nblk=nblk),
        out_shape=jax.ShapeDtypeStruct((M, D), F32),
        grid=(nblk + 1,),
        in_specs=[pl.BlockSpec(memory_space=pltpu.SMEM),
                  pl.BlockSpec((T, D), lambda i: (jnp.minimum(i, nblk - 1), 0))]
                 + [_layer_spec(c, layer) for c in consts],
        out_specs=pl.BlockSpec((T, D), lambda i: (jnp.maximum(i - 1, 0), 0)),
        scratch_shapes=[
            pltpu.VMEM((KV_WIDTH, WINDOW + T), BF16),
            pltpu.VMEM((WINDOW + T, KV_WIDTH), BF16),
            pltpu.VMEM((WINDOW + T, KV_WIDTH), BF16),
            pltpu.VMEM((CONV_HALO + T, CONV_WIDTH), F32),
            pltpu.VMEM((LRU_HALO + T, LRU_WIDTH), F32),
            pltpu.VMEM((SUBLANES, LRU_WIDTH), F32),
            pltpu.VMEM((2, WINDOW, 2 * WINDOW), F32),
            pltpu.VMEM((LRU_WIDTH // LANES, SUBLANES * SCAN_PITCH, LANES), F32),
            pltpu.VMEM((LRU_WIDTH // LANES, SUBLANES * SCAN_PITCH, LANES), F32),
            pltpu.VMEM((T, IN_WIDTH), F32),
            pltpu.VMEM((T, D), F32),
            pltpu.VMEM((T, D), BF16),
            pltpu.VMEM((T, D), F32),
            pltpu.VMEM((2, T, FF_CHUNK), BF16),
        ],
        compiler_params=pltpu.CompilerParams(
            dimension_semantics=("arbitrary",), vmem_limit_bytes=VMEM_LIMIT),
        name="layer_final" if final else "layer",
    )(sinks, x, *consts)


def _block_diag(w):
    dp, hN, di, dj = w.shape
    eye = jnp.eye(hN, dtype=w.dtype)
    return (eye[None, :, None, :, None] * w[:, :, :, None, :]).reshape(dp, hN * di, hN * dj)


def kernel(x, norm1, w_in, attn_sinks, conv_dw_w, conv_dw_b, conv_ln_g, conv_ln_b, lru_conv_w, lru_conv_b,
           lru_wa, lru_ba, lru_wx, lru_bx, lru_lambda, mix_norm, w_out, norm2, w_up, w_down, final_norm):
    B, S, D = x.shape
    depth = w_in.shape[0]
    qscale = jnp.concatenate([jnp.full((ATTN_WIDTH,), HEAD_DIM ** -0.5, F32),
                              jnp.ones((IN_WIDTH - ATTN_WIDTH,), F32)])
    win = (w_in * qscale).astype(BF16)
    wg = jnp.concatenate([_block_diag(lru_wa), _block_diag(lru_wx)], axis=2).astype(BF16)
    vec = jnp.concatenate(
        [norm1, conv_dw_b, conv_ln_g, conv_ln_b, lru_conv_b, lru_ba.reshape(depth, -1), lru_bx.reshape(depth, -1),
         lru_lambda, mix_norm, norm2, jnp.broadcast_to(final_norm[None], (depth, D))], axis=1)[:, None, :]
    consts = (vec, win, conv_dw_w, lru_conv_w, wg, w_out.astype(BF16), w_up.astype(BF16), w_down.astype(BF16))
    h = x.reshape(B * S, D)
    for l in range(depth):
        h = _layer_call(h, attn_sinks, consts, S, layer=l, final=(l == depth - 1))
    return h.reshape(B, S, D)
```

```python
import functools

import jax
import jax.numpy as jnp
from jax import lax
from jax.experimental import pallas as pl
from jax.experimental.pallas import tpu as pltpu

D_MODEL = 1024
HEAD_DIM = 64
ATTN_WIDTH = 512
N_Q_HEADS = 8
N_KV_HEADS = 2
KV_WIDTH = 128
WINDOW = 128
CONV_WIDTH = 256
CONV_KERNEL = 31
LRU_WIDTH = 256
LRU_HEADS = 4
LRU_HEAD_DIM = 64
LRU_CONV_KERNEL = 4
LRU_C = 8.0
MIX_WIDTH = 1024
IN_WIDTH = 1792
D_FF = 4096
RMS_EPS = 1e-6
LN_EPS = 1e-5
MASK_VALUE = -1e30

Q_OFF, K_OFF, V_OFF, CV_OFF, CG_OFF, RX_OFF, RG_OFF = 0, 512, 640, 768, 1024, 1280, 1536
YA_OFF, YC_OFF, YL_OFF = 0, ATTN_WIDTH, ATTN_WIDTH + CONV_WIDTH

VEC_NAMES = ("norm1", "conv_b", "ln_g", "ln_b", "lru_conv_b", "gate_b", "lambda", "mix_norm", "norm2", "final_norm")
VEC_WIDTHS = dict(zip(VEC_NAMES, (D_MODEL, CONV_WIDTH, CONV_WIDTH, CONV_WIDTH, LRU_WIDTH, 2 * LRU_WIDTH,
                                  LRU_WIDTH, MIX_WIDTH, D_MODEL, D_MODEL)))
VEC_OFF = {name: sum(VEC_WIDTHS[n] for n in VEC_NAMES[:k]) for k, name in enumerate(VEC_NAMES)}

LANES = 128
SUBLANES = 8
MXU_COLS = 256

MIX_T = 512
SUB = WINDOW
N_SUB = MIX_T // SUB
FF_CHUNK = D_FF // N_SUB
FF_PIECES = FF_CHUNK // MXU_COLS
DN_PIECES = D_MODEL // MXU_COLS
FFN_SPLIT = 1
CONV_HALO = 32
LRU_HALO = 8
SCAN_LEN = SUB // SUBLANES
SCAN_PITCH = SCAN_LEN + SUBLANES
VMEM_LIMIT = 56 * 1024 * 1024

F32 = jnp.float32
BF16 = jnp.bfloat16


def _sigmoid(x):
    return 0.5 * jnp.tanh(0.5 * x) + 0.5


def _rms(x, g):
    return x * lax.rsqrt(jnp.mean(x * x, axis=-1, keepdims=True) + RMS_EPS) * g


class _Vec:
    def __init__(self, ref, name):
        self.ref, self.off, self.width = ref, VEC_OFF[name], VEC_WIDTHS[name]

    def __getitem__(self, idx):
        if idx is Ellipsis:
            return self.ref[:, self.off:self.off + self.width]
        rows, cols = idx
        return self.ref[rows, self.off + cols.start:self.off + cols.stop]


def _layer_kernel(sinks_ref, x_ref, vec_ref, win_ref, cw_ref, lcw_ref, wg_ref, wout_ref, wup_ref, wdn_ref,
                  o_ref,
                  kt_ext, v_ext, vr_ext, u_ext, xl_ext, hcar, bias_s, a_s, b_s,
                  z_s, y_s, hmid, fhn_s, fres_s, fu_s,
                  *, layer, final, blocks_per_seq, nblk):
    T = MIX_T
    i = pl.program_id(0)
    s = lax.rem(i, blocks_per_seq)
    n1_ref, cb_ref, lng_ref, lnb_ref, lcb_ref, bg_ref, lam_ref, mixg_ref, n2_ref, fn_ref = (
        _Vec(vec_ref, name) for name in VEC_NAMES)

    @pl.when(s == 0)
    def _():
        kt_ext[:, 0:WINDOW] = jnp.zeros((KV_WIDTH, WINDOW), BF16)
        v_ext[0:WINDOW, :] = jnp.zeros((WINDOW, KV_WIDTH), BF16)
        vr_ext[0:WINDOW, :] = jnp.zeros((WINDOW, KV_WIDTH), BF16)
        u_ext[0:CONV_HALO, :] = jnp.zeros((CONV_HALO, CONV_WIDTH), F32)
        xl_ext[0:LRU_HALO, :] = jnp.zeros((LRU_HALO, LRU_WIDTH), F32)
        hcar[...] = jnp.zeros((SUBLANES, LRU_WIDTH), F32)

    qi = lax.broadcasted_iota(jnp.int32, (WINDOW, 2 * WINDOW), 0)
    kc = lax.broadcasted_iota(jnp.int32, (WINDOW, 2 * WINDOW), 1)
    band = (kc > qi) & (kc <= qi + WINDOW)
    bias_s[0] = jnp.where(band, 0.0, MASK_VALUE).astype(F32)
    bias_s[1] = jnp.where(band & (kc >= WINDOW), 0.0, MASK_VALUE).astype(F32)

    mg = mixg_ref[...]
    lane = lax.broadcasted_iota(jnp.int32, (2 * WINDOW, KV_WIDTH), 1)
    band_row = lax.broadcasted_iota(jnp.int32, (2 * WINDOW, KV_WIDTH), 0)
    keep_lo = (lane < HEAD_DIM) & (band_row != 0)
    drop_hi = (lane < HEAD_DIM) | (band_row == 0)
    lane_o = lax.broadcasted_iota(jnp.int32, (WINDOW, LANES), 1)
    zk = jnp.zeros((HEAD_DIM, 2 * WINDOW), BF16)
    row8 = lax.broadcasted_iota(jnp.int32, (SUBLANES, LANES), 0)
    L, P = SCAN_LEN, SCAN_PITCH

    def ffn_prep(r):
        rows = slice(r * SUB, (r + 1) * SUB)
        h = hmid[rows, :]
        fhn_s[rows, :] = _rms(h, n2_ref[...]).astype(BF16)
        fres_s[rows, :] = h

    def ffn_up(c, p):
        n0 = (p % 2) * 2 * MXU_COLS
        for q in range(FFN_SPLIT):
            r0 = (p // 2) * (T // 2) + q * (T // 2 // FFN_SPLIT)
            rows = slice(r0, r0 + T // 2 // FFN_SPLIT)
            u = jnp.dot(fhn_s[rows, :], wup_ref[:, c * FF_CHUNK + n0:c * FF_CHUNK + n0 + 2 * MXU_COLS],
                        preferred_element_type=F32).astype(BF16)
            u = jnp.maximum(u, 0.0)
            fu_s[c % 2, rows, n0:n0 + 2 * MXU_COLS] = u * u

    def ffn_down(c, p):
        cols = slice((p % 2) * 2 * MXU_COLS, (p % 2 + 1) * 2 * MXU_COLS)
        for q in range(FFN_SPLIT):
            r0 = (p // 2) * (T // 2) + q * (T // 2 // FFN_SPLIT)
            rows = slice(r0, r0 + T // 2 // FFN_SPLIT)
            d = jnp.dot(fu_s[c % 2, rows, :], wdn_ref[c * FF_CHUNK:(c + 1) * FF_CHUNK, cols],
                        preferred_element_type=F32)
            if c == 0:
                o_ref[rows, cols] = fres_s[rows, cols] + d
            else:
                o_ref[rows, cols] += d

    def ffn_finish(half):
        if final:
            rows = slice(half * (T // 2), (half + 1) * (T // 2))
            o_ref[rows, :] = _rms(o_ref[rows, :], fn_ref[...])

    def inproj(half):
        rows = slice(half * 2 * SUB, (half + 1) * 2 * SUB)
        hn = _rms(x_ref[rows, :], n1_ref[...]).astype(BF16)
        z_s[rows, :] = jnp.dot(hn, win_ref[...], preferred_element_type=F32)

    def attn_qk(r, st):
        r0 = r * SUB
        q = z_s[r0:r0 + SUB, Q_OFF:Q_OFF + ATTN_WIDTH].astype(BF16)
        kf = z_s[r0:r0 + SUB, K_OFF:K_OFF + KV_WIDTH]
        vf = z_s[r0:r0 + SUB, V_OFF:V_OFF + KV_WIDTH]
        kt_ext[:, WINDOW + r0:WINDOW + r0 + SUB] = kf.T.astype(BF16)
        v_ext[WINDOW + r0:WINDOW + r0 + SUB, :] = vf.astype(BF16)
        vr_ext[WINDOW + r0:WINDOW + r0 + SUB, :] = pltpu.roll(vf, HEAD_DIM, axis=1).astype(BF16)
        kt_band = kt_ext[:, r0:r0 + 2 * WINDOW]
        st["sc"] = []
        for h in range(N_KV_HEADS):
            kh = kt_band[h * HEAD_DIM:(h + 1) * HEAD_DIM, :]
            kt2 = jnp.concatenate(
                [jnp.concatenate([kh, zk], axis=0), jnp.concatenate([zk, kh], axis=0)], axis=1)
            c0 = h * 2 * LANES
            qs = jnp.concatenate([q[:, c0:c0 + LANES], q[:, c0 + LANES:c0 + 2 * LANES]], axis=0)
            st["sc"].append(jnp.dot(qs, kt2, preferred_element_type=F32))

    def attn_softmax(r, h, st):
        bias = bias_s[jnp.where(s == 0, 1, 0)] if r == 0 else bias_s[0]
        sc = st["sc"][h]
        p_rows = []
        dens = []
        for rr in range(2):
            p_cols = []
            den_r = []
            for e in range(2):
                sink = sinks_ref[layer, 4 * h + 2 * rr + e]
                rws = slice(rr * WINDOW, (rr + 1) * WINDOW)
                c0 = e * 2 * WINDOW
                t = jnp.concatenate(
                    [jnp.where(lane_o == 0, sink, sc[rws, c0:c0 + LANES] + bias[:, :LANES]),
                     sc[rws, c0 + LANES:c0 + 2 * LANES] + bias[:, LANES:]], axis=1)
                p = jnp.exp(t - jnp.max(t, axis=-1, keepdims=True))
                den_r.append(jnp.sum(p, axis=-1, keepdims=True))
                p_cols.append(p.astype(BF16))
            p_rows.append(jnp.concatenate(p_cols, axis=1))
            dens.append(den_r)
        st[("pm", h)] = jnp.concatenate(p_rows, axis=0)
        st[("den", h)] = dens

    def attn_pv(r, h, st):
        r0 = r * SUB
        v_band = v_ext[r0:r0 + 2 * WINDOW, :]
        vr_band = vr_ext[r0:r0 + 2 * WINDOW, :]
        if h == 0:
            va = jnp.where(keep_lo, v_band, jnp.zeros_like(v_band))
            vb = jnp.where(drop_hi, jnp.zeros_like(vr_band), vr_band)
        else:
            va = jnp.where(keep_lo, vr_band, jnp.zeros_like(vr_band))
            vb = jnp.where(drop_hi, jnp.zeros_like(v_band), v_band)
        v2 = jnp.concatenate([va, vb], axis=0)
        o = jnp.dot(st[("pm", h)], v2, preferred_element_type=F32)
        dens = st[("den", h)]
        for rr in range(2):
            den = jnp.where(lane_o < HEAD_DIM, dens[rr][0], dens[rr][1])
            st[("ya", 2 * h + rr)] = o[rr * WINDOW:(rr + 1) * WINDOW, :] / den

    def attn_norm(r, st):
        y_attn = jnp.concatenate([st[("ya", c)] for c in range(ATTN_WIDTH // LANES)], axis=1)
        y_s[r * SUB:(r + 1) * SUB, YA_OFF:YA_OFF + ATTN_WIDTH] = _rms(
            y_attn, mg[:, YA_OFF:YA_OFF + ATTN_WIDTH]).astype(BF16)

    def attn_dot(pair):
        rows = slice(pair * 2 * SUB, (pair + 1) * 2 * SUB)
        hmid[rows, :] = x_ref[rows, :] + jnp.dot(y_s[rows, YA_OFF:YA_OFF + ATTN_WIDTH],
                                                 wout_ref[YA_OFF:YA_OFF + ATTN_WIDTH, :],
                                                 preferred_element_type=F32)

    def conv_unit(r, lt, st):
        r0 = r * SUB
        cl = slice(lt * LANES, (lt + 1) * LANES)
        cval = z_s[r0:r0 + SUB, CV_OFF + lt * LANES:CV_OFF + (lt + 1) * LANES]
        cgate = z_s[r0:r0 + SUB, CG_OFF + lt * LANES:CG_OFF + (lt + 1) * LANES]
        u_ext[CONV_HALO + r0:CONV_HALO + r0 + SUB, cl] = cval * _sigmoid(cgate)
        base = CONV_HALO - (CONV_KERNEL - 1)
        acc = jnp.broadcast_to(cb_ref[:, cl], (SUB, LANES))
        for sh in range(SUBLANES):
            part = None
            nrows = SUB + (SUBLANES if sh else 0)
            for k in range(CONV_KERNEL):
                if (base + k) % SUBLANES != sh:
                    continue
                al = r0 + base + k - sh
                term = cw_ref[k:k + 1, cl] * u_ext[al:al + nrows, cl]
                part = term if part is None else part + term
            acc = acc + part[sh:sh + SUB, :]
        st[("yc", lt)] = acc

    def conv_post(r, st):
        uc = jnp.concatenate([st[("yc", lt)] for lt in range(CONV_WIDTH // LANES)], axis=1)
        mu = jnp.mean(uc, axis=-1, keepdims=True)
        xc_ = uc - mu
        ln = xc_ * lax.rsqrt(jnp.mean(xc_ * xc_, axis=-1, keepdims=True) + LN_EPS) * lng_ref[...] + lnb_ref[...]
        y_conv = ln * _sigmoid(ln)
        y_s[r * SUB:(r + 1) * SUB, YC_OFF:YC_OFF + CONV_WIDTH] = _rms(
            y_conv, mg[:, YC_OFF:YC_OFF + CONV_WIDTH]).astype(BF16)

    def conv_dot(pair):
        rows = slice(pair * 2 * SUB, (pair + 1) * 2 * SUB)
        hmid[rows, :] += jnp.dot(y_s[rows, YC_OFF:YC_OFF + CONV_WIDTH], wout_ref[YC_OFF:YC_OFF + CONV_WIDTH, :],
                                 preferred_element_type=F32)

    def lru_pre(r, st):
        r0 = r * SUB
        xl_ext[LRU_HALO + r0:LRU_HALO + r0 + SUB, :] = z_s[r0:r0 + SUB, RX_OFF:RX_OFF + LRU_WIDTH]
        lbase = LRU_HALO - (LRU_CONV_KERNEL - 1)
        xc = jnp.broadcast_to(lcb_ref[...], (SUB, LRU_WIDTH))
        for k in range(LRU_CONV_KERNEL):
            xc = xc + lcw_ref[k:k + 1, :] * xl_ext[r0 + lbase + k:r0 + lbase + k + SUB, :]
        st["xc"] = xc
        st["xc_b"] = xc.astype(BF16)

    def lru_gate_dot(r, st):
        st["gates"] = jnp.dot(st["xc_b"], wg_ref[...], preferred_element_type=F32) + bg_ref[...]

    def lru_scan(r, st):
        r0 = r * SUB
        xc, gates = st["xc"], st["gates"]
        rgate = _sigmoid(gates[:, :LRU_WIDTH])
        igate = _sigmoid(gates[:, LRU_WIDTH:])
        log_a = (-LRU_C * rgate) * jax.nn.softplus(-lam_ref[...])
        a_full = jnp.exp(log_a)
        th = jnp.tanh(log_a)
        b_full = jnp.sqrt(2.0 * th / (th - 1.0)) * (igate * xc)
        yl_cols = []
        for lt in range(LRU_WIDTH // LANES):
            cl = slice(lt * LANES, (lt + 1) * LANES)
            for c in range(SUBLANES):
                a_s[lt, c * P:c * P + L, :] = a_full[c * L:(c + 1) * L, cl]
                b_s[lt, c * P:c * P + L, :] = b_full[c * L:(c + 1) * L, cl]
            hloc = jnp.zeros((SUBLANES, LANES), F32)
            cum = jnp.ones((SUBLANES, LANES), F32)
            for m in range(L):
                am = a_s[lt, pl.ds(m, SUBLANES, stride=P), :]
                bm = b_s[lt, pl.ds(m, SUBLANES, stride=P), :]
                hloc = am * hloc + bm
                cum = am * cum
                b_s[lt, pl.ds(m, SUBLANES, stride=P), :] = hloc
                a_s[lt, pl.ds(m, SUBLANES, stride=P), :] = cum
            ca, cbv = cum, hloc
            for d in (1, 2, 4):
                a_sh = jnp.where(row8 >= d, pltpu.roll(ca, d, axis=0), 1.0)
                b_sh = jnp.where(row8 >= d, pltpu.roll(cbv, d, axis=0), 0.0)
                cbv = ca * b_sh + cbv
                ca = ca * a_sh
            hprev = hcar[:, cl]
            ends = ca * hprev + cbv
            carry_in = jnp.where(row8 == 0, hprev, pltpu.roll(ends, 1, axis=0))
            hcar[:, cl] = jnp.broadcast_to(ends[SUBLANES - 1:SUBLANES, :], (SUBLANES, LANES))
            parts = []
            for c in range(SUBLANES):
                g = jnp.broadcast_to(carry_in[c:c + 1, :], (L, LANES))
                parts.append(b_s[lt, c * P:c * P + L, :] + a_s[lt, c * P:c * P + L, :] * g)
            yl_cols.append(jnp.concatenate(parts, axis=0))
        y_lru = jnp.concatenate(yl_cols, axis=1) * jax.nn.gelu(z_s[r0:r0 + SUB, RG_OFF:RG_OFF + LRU_WIDTH])
        y_s[r0:r0 + SUB, YL_OFF:YL_OFF + LRU_WIDTH] = _rms(y_lru, mg[:, YL_OFF:YL_OFF + LRU_WIDTH]).astype(BF16)

    def lru_dot(pair):
        rows = slice(pair * 2 * SUB, (pair + 1) * 2 * SUB)
        hmid[rows, :] += jnp.dot(y_s[rows, YL_OFF:YL_OFF + LRU_WIDTH], wout_ref[YL_OFF:YL_OFF + LRU_WIDTH, :],
                                 preferred_element_type=F32)

    def emit(run_ffn, run_mixer):
        pieces = []
        if run_ffn:
            for c in range(N_SUB):
                pieces += [functools.partial(ffn_up, c, p) for p in range(4)]
                pieces += [functools.partial(ffn_down, c, p) for p in range(4)]
        n_emitted = [0]
        n_prepped = [0]

        def next_ffn(mixer_rows_done):
            k = n_emitted[0]
            if k >= len(pieces):
                return
            pieces[k]()
            n_emitted[0] += 1
            if run_mixer and k >= N_SUB * 8 - 5:
                while n_prepped[0] < mixer_rows_done:
                    ffn_prep(n_prepped[0])
                    n_prepped[0] += 1
            if k == N_SUB * 8 - 3:
                ffn_finish(0)
            if k == N_SUB * 8 - 1:
                ffn_finish(1)

        if not run_mixer:
            for _ in pieces:
                next_ffn(0)
            return

        next_ffn(0)
        inproj(0)
        for r in range(N_SUB):
            st = {}
            pair, second = r // 2, r % 2 == 1
            done = 2 * pair
            attn_qk(r, st)
            next_ffn(done)
            attn_softmax(r, 0, st)
            attn_softmax(r, 1, st)
            next_ffn(done)
            attn_pv(r, 0, st)
            next_ffn(done)
            attn_pv(r, 1, st)
            attn_norm(r, st)
            conv_unit(r, 0, st)
            next_ffn(done)
            if second:
                attn_dot(pair)
            conv_unit(r, 1, st)
            next_ffn(done)
            conv_post(r, st)
            lru_pre(r, st)
            next_ffn(done)
            if second:
                conv_dot(pair)
            lru_gate_dot(r, st)
            if not second and r + 2 < N_SUB:
                inproj(pair + 1)
            lru_scan(r, st)
            next_ffn(done)
            if second:
                lru_dot(pair)
                done += 2
            if r + 1 < N_SUB:
                next_ffn(done)
        while n_prepped[0] < N_SUB:
            ffn_prep(n_prepped[0])
            n_prepped[0] += 1

        kt_ext[:, 0:WINDOW] = kt_ext[:, T:T + WINDOW]
        v_ext[0:WINDOW, :] = v_ext[T:T + WINDOW, :]
        vr_ext[0:WINDOW, :] = vr_ext[T:T + WINDOW, :]
        u_ext[0:CONV_HALO, :] = u_ext[T:T + CONV_HALO, :]
        xl_ext[0:LRU_HALO, :] = xl_ext[T:T + LRU_HALO, :]

    pl.when(i == 0)(functools.partial(emit, False, True))
    pl.when(jnp.logical_and(i > 0, i < nblk))(functools.partial(emit, True, True))
    pl.when(i == nblk)(functools.partial(emit, True, False))


def _layer_spec(arr, layer):
    nd = arr.ndim
    return pl.BlockSpec((None,) + arr.shape[1:], lambda i: (layer,) + (0,) * (nd - 1),
                        pipeline_mode=pl.Buffered(1))


def _layer_call(x, sinks, consts, seq_len, layer, final):
    M, D = x.shape
    T = MIX_T
    nblk = M // T
    return pl.pallas_call(
        functools.partial(_layer_kernel, layer=layer, final=final, blocks_per_seq=seq_len // T, nblk=nblk),
        out_shape=jax.ShapeDtypeStruct((M, D), F32),
        grid=(nblk + 1,),
        in_specs=[pl.BlockSpec(memory_space=pltpu.SMEM),
                  pl.BlockSpec((T, D), lambda i: (jnp.minimum(i, nblk - 1), 0))]
                 + [_layer_spec(c, layer) for c in consts],
        out_specs=pl.BlockSpec((T, D), lambda i: (jnp.maximum(i - 1, 0), 0)),
        scratch_shapes=[
            pltpu.VMEM((KV_WIDTH, WINDOW + T), BF16),
            pltpu.VMEM((WINDOW + T, KV_WIDTH), BF16),
            pltpu.VMEM((WINDOW + T, KV_WIDTH), BF16),
            pltpu.VMEM((CONV_HALO + T, CONV_WIDTH), F32),
            pltpu.VMEM((LRU_HALO + T, LRU_WIDTH), F32),
            pltpu.VMEM((SUBLANES, LRU_WIDTH), F32),
            pltpu.VMEM((2, WINDOW, 2 * WINDOW), F32),
            pltpu.VMEM((LRU_WIDTH // LANES, SUBLANES * SCAN_PITCH, LANES), F32),
            pltpu.VMEM((LRU_WIDTH // LANES, SUBLANES * SCAN_PITCH, LANES), F32),
            pltpu.VMEM((T, IN_WIDTH), F32),
            pltpu.VMEM((T, MIX_WIDTH), BF16),
            pltpu.VMEM((T, D), F32),
            pltpu.VMEM((T, D), BF16),
            pltpu.VMEM((T, D), F32),
            pltpu.VMEM((2, T, FF_CHUNK), BF16),
        ],
        compiler_params=pltpu.CompilerParams(
            dimension_semantics=("arbitrary",), vmem_limit_bytes=VMEM_LIMIT),
        name="layer_final" if final else "layer",
    )(sinks, x, *consts)


def _block_diag(w):
    dp, hN, di, dj = w.shape
    eye = jnp.eye(hN, dtype=w.dtype)
    return (eye[None, :, None, :, None] * w[:, :, :, None, :]).reshape(dp, hN * di, hN * dj)


def kernel(x, norm1, w_in, attn_sinks, conv_dw_w, conv_dw_b, conv_ln_g, conv_ln_b, lru_conv_w, lru_conv_b,
           lru_wa, lru_ba, lru_wx, lru_bx, lru_lambda, mix_norm, w_out, norm2, w_up, w_down, final_norm):
    B, S, D = x.shape
    depth = w_in.shape[0]
    qscale = jnp.concatenate([jnp.full((ATTN_WIDTH,), HEAD_DIM ** -0.5, F32),
                              jnp.ones((IN_WIDTH - ATTN_WIDTH,), F32)])
    win = (w_in * qscale).astype(BF16)
    wg = jnp.concatenate([_block_diag(lru_wa), _block_diag(lru_wx)], axis=2).astype(BF16)
    vec = jnp.concatenate(
        [norm1, conv_dw_b, conv_ln_g, conv_ln_b, lru_conv_b, lru_ba.reshape(depth, -1), lru_bx.reshape(depth, -1),
         lru_lambda, mix_norm, norm2, jnp.broadcast_to(final_norm[None], (depth, D))], axis=1)[:, None, :]
    consts = (vec, win, conv_dw_w, lru_conv_w, wg, w_out.astype(BF16), w_up.astype(BF16), w_down.astype(BF16))
    h = x.reshape(B * S, D)
    for l in range(depth):
        h = _layer_call(h, attn_sinks, consts, S, layer=l, final=(l == depth - 1))
    return h.reshape(B, S, D)
```

```python
import functools

import jax
import jax.numpy as jnp
from jax import lax
from jax.experimental import pallas as pl
from jax.experimental.pallas import tpu as pltpu

D_MODEL = 1024
HEAD_DIM = 64
ATTN_WIDTH = 512
N_Q_HEADS = 8
N_KV_HEADS = 2
KV_WIDTH = 128
WINDOW = 128
CONV_WIDTH = 256
CONV_KERNEL = 31
LRU_WIDTH = 256
LRU_HEADS = 4
LRU_HEAD_DIM = 64
LRU_CONV_KERNEL = 4
LRU_C = 8.0
MIX_WIDTH = 1024
IN_WIDTH = 1792
D_FF = 4096
RMS_EPS = 1e-6
LN_EPS = 1e-5
MASK_VALUE = -1e30

Q_OFF, K_OFF, V_OFF, CV_OFF, CG_OFF, RX_OFF, RG_OFF = 0, 512, 640, 768, 1024, 1280, 1536
YA_OFF, YC_OFF, YL_OFF = 0, ATTN_WIDTH, ATTN_WIDTH + CONV_WIDTH

VEC_NAMES = ("norm1", "conv_b", "ln_g", "ln_b", "lru_conv_b", "gate_b", "lambda", "mix_norm", "norm2", "final_norm")
VEC_WIDTHS = dict(zip(VEC_NAMES, (D_MODEL, CONV_WIDTH, CONV_WIDTH, CONV_WIDTH, LRU_WIDTH, 2 * LRU_WIDTH,
                                  LRU_WIDTH, MIX_WIDTH, D_MODEL, D_MODEL)))
VEC_OFF = {name: sum(VEC_WIDTHS[n] for n in VEC_NAMES[:k]) for k, name in enumerate(VEC_NAMES)}

LANES = 128
SUBLANES = 8
MXU_COLS = 256

MIX_T = 512
SUB = WINDOW
N_SUB = MIX_T // SUB
FF_CHUNK = D_FF // N_SUB
FF_PIECES = FF_CHUNK // MXU_COLS
DN_PIECES = D_MODEL // MXU_COLS
FFN_SPLIT = 1
CONV_HALO = 32
LRU_HALO = 8
SCAN_LEN = SUB // SUBLANES
SCAN_PITCH = SCAN_LEN + SUBLANES
VMEM_LIMIT = 56 * 1024 * 1024

F32 = jnp.float32
BF16 = jnp.bfloat16


def _sigmoid(x):
    return 0.5 * jnp.tanh(0.5 * x) + 0.5


def _rms(x, g):
    return x * lax.rsqrt(jnp.mean(x * x, axis=-1, keepdims=True) + RMS_EPS) * g


class _Vec:
    def __init__(self, ref, name):
        self.ref, self.off, self.width = ref, VEC_OFF[name], VEC_WIDTHS[name]

    def __getitem__(self, idx):
        if idx is Ellipsis:
            return self.ref[:, self.off:self.off + self.width]
        rows, cols = idx
        return self.ref[rows, self.off + cols.start:self.off + cols.stop]


def _layer_kernel(sinks_ref, x_ref, vec_ref, win_ref, cw_ref, lcw_ref, wg_ref, wout_ref, wup_ref, wdn_ref,
                  o_ref,
                  kt_ext, v_ext, vr_ext, u_ext, xl_ext, hcar, bias_s, a_s, b_s,
                  z_s, hmid, fhn_s, fres_s, fu_s,
                  *, layer, final, blocks_per_seq, nblk):
    T = MIX_T
    i = pl.program_id(0)
    s = lax.rem(i, blocks_per_seq)
    n1_ref, cb_ref, lng_ref, lnb_ref, lcb_ref, bg_ref, lam_ref, mixg_ref, n2_ref, fn_ref = (
        _Vec(vec_ref, name) for name in VEC_NAMES)

    @pl.when(s == 0)
    def _():
        kt_ext[:, 0:WINDOW] = jnp.zeros((KV_WIDTH, WINDOW), BF16)
        v_ext[0:WINDOW, :] = jnp.zeros((WINDOW, KV_WIDTH), BF16)
        vr_ext[0:WINDOW, :] = jnp.zeros((WINDOW, KV_WIDTH), BF16)
        u_ext[0:CONV_HALO, :] = jnp.zeros((CONV_HALO, CONV_WIDTH), F32)
        xl_ext[0:LRU_HALO, :] = jnp.zeros((LRU_HALO, LRU_WIDTH), F32)
        hcar[...] = jnp.zeros((SUBLANES, LRU_WIDTH), F32)

    qi = lax.broadcasted_iota(jnp.int32, (WINDOW, 2 * WINDOW), 0)
    kc = lax.broadcasted_iota(jnp.int32, (WINDOW, 2 * WINDOW), 1)
    band = (kc > qi) & (kc <= qi + WINDOW)
    bias_s[0] = jnp.where(band, 0.0, MASK_VALUE).astype(F32)
    bias_s[1] = jnp.where(band & (kc >= WINDOW), 0.0, MASK_VALUE).astype(F32)

    mg = mixg_ref[...]
    lane = lax.broadcasted_iota(jnp.int32, (2 * WINDOW, KV_WIDTH), 1)
    band_row = lax.broadcasted_iota(jnp.int32, (2 * WINDOW, KV_WIDTH), 0)
    keep_lo = (lane < HEAD_DIM) & (band_row != 0)
    drop_hi = (lane < HEAD_DIM) | (band_row == 0)
    lane_o = lax.broadcasted_iota(jnp.int32, (WINDOW, LANES), 1)
    zk = jnp.zeros((HEAD_DIM, 2 * WINDOW), BF16)
    row8 = lax.broadcasted_iota(jnp.int32, (SUBLANES, LANES), 0)
    L, P = SCAN_LEN, SCAN_PITCH

    def ffn_prep(r):
        rows = slice(r * SUB, (r + 1) * SUB)
        h = hmid[rows, :]
        fhn_s[rows, :] = _rms(h, n2_ref[...]).astype(BF16)
        fres_s[rows, :] = h

    def ffn_up(c, p):
        n0 = (p % 2) * 2 * MXU_COLS
        for q in range(FFN_SPLIT):
            r0 = (p // 2) * (T // 2) + q * (T // 2 // FFN_SPLIT)
            rows = slice(r0, r0 + T // 2 // FFN_SPLIT)
            u = jnp.dot(fhn_s[rows, :], wup_ref[:, c * FF_CHUNK + n0:c * FF_CHUNK + n0 + 2 * MXU_COLS],
                        preferred_element_type=F32).astype(BF16)
            u = jnp.maximum(u, 0.0)
            fu_s[c % 2, rows, n0:n0 + 2 * MXU_COLS] = u * u

    def ffn_down(c, p):
        cols = slice((p % 2) * 2 * MXU_COLS, (p % 2 + 1) * 2 * MXU_COLS)
        for q in range(FFN_SPLIT):
            r0 = (p // 2) * (T // 2) + q * (T // 2 // FFN_SPLIT)
            rows = slice(r0, r0 + T // 2 // FFN_SPLIT)
            d = jnp.dot(fu_s[c % 2, rows, :], wdn_ref[c * FF_CHUNK:(c + 1) * FF_CHUNK, cols],
                        preferred_element_type=F32)
            if c == 0:
                o_ref[rows, cols] = fres_s[rows, cols] + d
            else:
                o_ref[rows, cols] += d

    def ffn_finish(half):
        if final:
            rows = slice(half * (T // 2), (half + 1) * (T // 2))
            o_ref[rows, :] = _rms(o_ref[rows, :], fn_ref[...])

    def inproj(half):
        rows = slice(half * 2 * SUB, (half + 1) * 2 * SUB)
        hn = _rms(x_ref[rows, :], n1_ref[...]).astype(BF16)
        z_s[rows, :] = jnp.dot(hn, win_ref[...], preferred_element_type=F32)

    def attn_qk(r, st):
        r0 = r * SUB
        q = z_s[r0:r0 + SUB, Q_OFF:Q_OFF + ATTN_WIDTH].astype(BF16)
        kf = z_s[r0:r0 + SUB, K_OFF:K_OFF + KV_WIDTH]
        vf = z_s[r0:r0 + SUB, V_OFF:V_OFF + KV_WIDTH]
        kt_ext[:, WINDOW + r0:WINDOW + r0 + SUB] = kf.T.astype(BF16)
        v_ext[WINDOW + r0:WINDOW + r0 + SUB, :] = vf.astype(BF16)
        vr_ext[WINDOW + r0:WINDOW + r0 + SUB, :] = pltpu.roll(vf, HEAD_DIM, axis=1).astype(BF16)
        kt_band = kt_ext[:, r0:r0 + 2 * WINDOW]
        st["sc"] = []
        for h in range(N_KV_HEADS):
            kh = kt_band[h * HEAD_DIM:(h + 1) * HEAD_DIM, :]
            kt2 = jnp.concatenate(
                [jnp.concatenate([kh, zk], axis=0), jnp.concatenate([zk, kh], axis=0)], axis=1)
            c0 = h * 2 * LANES
            qs = jnp.concatenate([q[:, c0:c0 + LANES], q[:, c0 + LANES:c0 + 2 * LANES]], axis=0)
            st["sc"].append(jnp.dot(qs, kt2, preferred_element_type=F32))

    def attn_softmax(r, h, st):
        bias = bias_s[jnp.where(s == 0, 1, 0)] if r == 0 else bias_s[0]
        sc = st["sc"][h]
        p_rows = []
        dens = []
        for rr in range(2):
            p_cols = []
            den_r = []
            for e in range(2):
                sink = sinks_ref[layer, 4 * h + 2 * rr + e]
                rws = slice(rr * WINDOW, (rr + 1) * WINDOW)
                c0 = e * 2 * WINDOW
                t = jnp.concatenate(
                    [jnp.where(lane_o == 0, sink, sc[rws, c0:c0 + LANES] + bias[:, :LANES]),
                     sc[rws, c0 + LANES:c0 + 2 * LANES] + bias[:, LANES:]], axis=1)
                p = jnp.exp(t - jnp.max(t, axis=-1, keepdims=True))
                den_r.append(jnp.sum(p, axis=-1, keepdims=True))
                p_cols.append(p.astype(BF16))
            p_rows.append(jnp.concatenate(p_cols, axis=1))
            dens.append(den_r)
        st[("pm", h)] = jnp.concatenate(p_rows, axis=0)
        st[("den", h)] = dens

    def attn_pv(r, h, st):
        r0 = r * SUB
        v_band = v_ext[r0:r0 + 2 * WINDOW, :]
        vr_band = vr_ext[r0:r0 + 2 * WINDOW, :]
        if h == 0:
            va = jnp.where(keep_lo, v_band, jnp.zeros_like(v_band))
            vb = jnp.where(drop_hi, jnp.zeros_like(vr_band), vr_band)
        else:
            va = jnp.where(keep_lo, vr_band, jnp.zeros_like(vr_band))
            vb = jnp.where(drop_hi, jnp.zeros_like(v_band), v_band)
        v2 = jnp.concatenate([va, vb], axis=0)
        o = jnp.dot(st[("pm", h)], v2, preferred_element_type=F32)
        dens = st[("den", h)]
        for rr in range(2):
            den = jnp.where(lane_o < HEAD_DIM, dens[rr][0], dens[rr][1])
            st[("ya", 2 * h + rr)] = o[rr * WINDOW:(rr + 1) * WINDOW, :] / den

    def attn_norm(r, st):
        y_attn = jnp.concatenate([st[("ya", c)] for c in range(ATTN_WIDTH // LANES)], axis=1)
        st["ya_n"] = _rms(y_attn, mg[:, YA_OFF:YA_OFF + ATTN_WIDTH]).astype(BF16)

    def attn_dot(r, st):
        rows = slice(r * SUB, (r + 1) * SUB)
        hmid[rows, :] = x_ref[rows, :] + jnp.dot(st["ya_n"], wout_ref[YA_OFF:YA_OFF + ATTN_WIDTH, :],
                                                 preferred_element_type=F32)

    def conv_unit(r, lt, st):
        r0 = r * SUB
        cl = slice(lt * LANES, (lt + 1) * LANES)
        cval = z_s[r0:r0 + SUB, CV_OFF + lt * LANES:CV_OFF + (lt + 1) * LANES]
        cgate = z_s[r0:r0 + SUB, CG_OFF + lt * LANES:CG_OFF + (lt + 1) * LANES]
        u_ext[CONV_HALO + r0:CONV_HALO + r0 + SUB, cl] = cval * _sigmoid(cgate)
        base = CONV_HALO - (CONV_KERNEL - 1)
        acc = jnp.broadcast_to(cb_ref[:, cl], (SUB, LANES))
        for sh in range(SUBLANES):
            part = None
            nrows = SUB + (SUBLANES if sh else 0)
            for k in range(CONV_KERNEL):
                if (base + k) % SUBLANES != sh:
                    continue
                al = r0 + base + k - sh
                term = cw_ref[k:k + 1, cl] * u_ext[al:al + nrows, cl]
                part = term if part is None else part + term
            acc = acc + part[sh:sh + SUB, :]
        st[("yc", lt)] = acc

    def conv_post(r, st):
        uc = jnp.concatenate([st[("yc", lt)] for lt in range(CONV_WIDTH // LANES)], axis=1)
        mu = jnp.mean(uc, axis=-1, keepdims=True)
        xc_ = uc - mu
        ln = xc_ * lax.rsqrt(jnp.mean(xc_ * xc_, axis=-1, keepdims=True) + LN_EPS) * lng_ref[...] + lnb_ref[...]
        y_conv = ln * _sigmoid(ln)
        st["yc_n"] = _rms(y_conv, mg[:, YC_OFF:YC_OFF + CONV_WIDTH]).astype(BF16)

    def conv_dot(r, st):
        hmid[r * SUB:(r + 1) * SUB, :] += jnp.dot(st["yc_n"], wout_ref[YC_OFF:YC_OFF + CONV_WIDTH, :],
                                                  preferred_element_type=F32)

    def lru_pre(r, st):
        r0 = r * SUB
        xl_ext[LRU_HALO + r0:LRU_HALO + r0 + SUB, :] = z_s[r0:r0 + SUB, RX_OFF:RX_OFF + LRU_WIDTH]
        lbase = LRU_HALO - (LRU_CONV_KERNEL - 1)
        xc = jnp.broadcast_to(lcb_ref[...], (SUB, LRU_WIDTH))
        for k in range(LRU_CONV_KERNEL):
            xc = xc + lcw_ref[k:k + 1, :] * xl_ext[r0 + lbase + k:r0 + lbase + k + SUB, :]
        st["xc"] = xc
        st["xc_b"] = xc.astype(BF16)

    def lru_gate_dot(r, st):
        st["gates"] = jnp.dot(st["xc_b"], wg_ref[...], preferred_element_type=F32) + bg_ref[...]

    def lru_scan(r, st):
        r0 = r * SUB
        xc, gates = st["xc"], st["gates"]
        rgate = _sigmoid(gates[:, :LRU_WIDTH])
        igate = _sigmoid(gates[:, LRU_WIDTH:])
        log_a = (-LRU_C * rgate) * jax.nn.softplus(-lam_ref[...])
        a_full = jnp.exp(log_a)
        th = jnp.tanh(log_a)
        b_full = jnp.sqrt(2.0 * th / (th - 1.0)) * (igate * xc)
        yl_cols = []
        for lt in range(LRU_WIDTH // LANES):
            cl = slice(lt * LANES, (lt + 1) * LANES)
            for c in range(SUBLANES):
                a_s[lt, c * P:c * P + L, :] = a_full[c * L:(c + 1) * L, cl]
                b_s[lt, c * P:c * P + L, :] = b_full[c * L:(c + 1) * L, cl]
            hloc = jnp.zeros((SUBLANES, LANES), F32)
            cum = jnp.ones((SUBLANES, LANES), F32)
            for m in range(L):
                am = a_s[lt, pl.ds(m, SUBLANES, stride=P), :]
                bm = b_s[lt, pl.ds(m, SUBLANES, stride=P), :]
                hloc = am * hloc + bm
                cum = am * cum
                b_s[lt, pl.ds(m, SUBLANES, stride=P), :] = hloc
                a_s[lt, pl.ds(m, SUBLANES, stride=P), :] = cum
            ca, cbv = cum, hloc
            for d in (1, 2, 4):
                a_sh = jnp.where(row8 >= d, pltpu.roll(ca, d, axis=0), 1.0)
                b_sh = jnp.where(row8 >= d, pltpu.roll(cbv, d, axis=0), 0.0)
                cbv = ca * b_sh + cbv
                ca = ca * a_sh
            hprev = hcar[:, cl]
            ends = ca * hprev + cbv
            carry_in = jnp.where(row8 == 0, hprev, pltpu.roll(ends, 1, axis=0))
            hcar[:, cl] = jnp.broadcast_to(ends[SUBLANES - 1:SUBLANES, :], (SUBLANES, LANES))
            parts = []
            for c in range(SUBLANES):
                g = jnp.broadcast_to(carry_in[c:c + 1, :], (L, LANES))
                parts.append(b_s[lt, c * P:c * P + L, :] + a_s[lt, c * P:c * P + L, :] * g)
            yl_cols.append(jnp.concatenate(parts, axis=0))
        y_lru = jnp.concatenate(yl_cols, axis=1) * jax.nn.gelu(z_s[r0:r0 + SUB, RG_OFF:RG_OFF + LRU_WIDTH])
        st["yl_n"] = _rms(y_lru, mg[:, YL_OFF:YL_OFF + LRU_WIDTH]).astype(BF16)

    def lru_dot(r, st):
        hmid[r * SUB:(r + 1) * SUB, :] += jnp.dot(st["yl_n"], wout_ref[YL_OFF:YL_OFF + LRU_WIDTH, :],
                                                  preferred_element_type=F32)

    def emit(run_ffn, run_mixer):
        pieces = []
        if run_ffn:
            for c in range(N_SUB):
                pieces += [functools.partial(ffn_up, c, p) for p in range(4)]
                pieces += [functools.partial(ffn_down, c, p) for p in range(4)]
        n_emitted = [0]
        n_prepped = [0]

        def next_ffn(mixer_rows_done):
            k = n_emitted[0]
            if k >= len(pieces):
                return
            pieces[k]()
            n_emitted[0] += 1
            if run_mixer and k >= N_SUB * 8 - 5:
                while n_prepped[0] < mixer_rows_done:
                    ffn_prep(n_prepped[0])
                    n_prepped[0] += 1
            if k == N_SUB * 8 - 3:
                ffn_finish(0)
            if k == N_SUB * 8 - 1:
                ffn_finish(1)

        if not run_mixer:
            for _ in pieces:
                next_ffn(0)
            return

        next_ffn(0)
        inproj(0)
        for r in range(N_SUB):
            st = {}
            attn_qk(r, st)
            next_ffn(r)
            attn_softmax(r, 0, st)
            attn_softmax(r, 1, st)
            next_ffn(r)
            attn_pv(r, 0, st)
            next_ffn(r)
            attn_pv(r, 1, st)
            attn_norm(r, st)
            conv_unit(r, 0, st)
            next_ffn(r)
            attn_dot(r, st)
            conv_unit(r, 1, st)
            next_ffn(r)
            conv_post(r, st)
            lru_pre(r, st)
            next_ffn(r)
            conv_dot(r, st)
            lru_gate_dot(r, st)
            if r % 2 == 0 and r + 2 < N_SUB:
                inproj(r // 2 + 1)
            lru_scan(r, st)
            next_ffn(r)
            lru_dot(r, st)
            if r + 1 < N_SUB:
                next_ffn(r + 1)
        while n_prepped[0] < N_SUB:
            ffn_prep(n_prepped[0])
            n_prepped[0] += 1

        kt_ext[:, 0:WINDOW] = kt_ext[:, T:T + WINDOW]
        v_ext[0:WINDOW, :] = v_ext[T:T + WINDOW, :]
        vr_ext[0:WINDOW, :] = vr_ext[T:T + WINDOW, :]
        u_ext[0:CONV_HALO, :] = u_ext[T:T + CONV_HALO, :]
        xl_ext[0:LRU_HALO, :] = xl_ext[T:T + LRU_HALO, :]

    @pl.when(i == 0)
    def _():
        fhn_s[...] = jnp.zeros((T, D_MODEL), BF16)
        fres_s[...] = jnp.zeros((T, D_MODEL), F32)

    emit(True, True)


def _layer_spec(arr, layer):
    nd = arr.ndim
    return pl.BlockSpec((None,) + arr.shape[1:], lambda i: (layer,) + (0,) * (nd - 1),
                        pipeline_mode=pl.Buffered(1))


def _layer_call(x, sinks, consts, seq_len, layer, final):
    M, D = x.shape
    T = MIX_T
    nblk = M // T
    return pl.pallas_call(
        functools.partial(_layer_kernel, layer=layer, final=final, blocks_per_seq=seq_len // T, nblk=nblk),
        out_shape=jax.ShapeDtypeStruct((M, D), F32),
        grid=(nblk + 1,),
        in_specs=[pl.BlockSpec(memory_space=pltpu.SMEM),
                  pl.BlockSpec((T, D), lambda i: (jnp.minimum(i, nblk - 1), 0))]
                 + [_layer_spec(c, layer) for c in consts],
        out_specs=pl.BlockSpec((T, D), lambda i: (jnp.maximum(i - 1, 0), 0)),
        scratch_shapes=[
            pltpu.VMEM((KV_WIDTH, WINDOW + T), BF16),
            pltpu.VMEM((WINDOW + T, KV_WIDTH), BF16),
            pltpu.VMEM((WINDOW + T, KV_WIDTH), BF16),
            pltpu.VMEM((CONV_HALO + T, CONV_WIDTH), F32),
            pltpu.VMEM((LRU_HALO + T, LRU_WIDTH), F32),
            pltpu.VMEM((SUBLANES, LRU_WIDTH), F32),
            pltpu.VMEM((2, WINDOW, 2 * WINDOW), F32),
            pltpu.VMEM((LRU_WIDTH // LANES, SUBLANES * SCAN_PITCH, LANES), F32),
            pltpu.VMEM((LRU_WIDTH // LANES, SUBLANES * SCAN_PITCH, LANES), F32),
            pltpu.VMEM((T, IN_WIDTH), F32),
            pltpu.VMEM((T, D), F32),
            pltpu.VMEM((T, D), BF16),
            pltpu.VMEM((T, D), F32),
            pltpu.VMEM((2, T, FF_CHUNK), BF16),
        ],
        compiler_params=pltpu.CompilerParams(
            dimension_semantics=("arbitrary",), vmem_limit_bytes=VMEM_LIMIT),
        name="layer_final" if final else "layer",
    )(sinks, x, *consts)


def _block_diag(w):
    dp, hN, di, dj = w.shape
    eye = jnp.eye(hN, dtype=w.dtype)
    return (eye[None, :, None, :, None] * w[:, :, :, None, :]).reshape(dp, hN * di, hN * dj)


def kernel(x, norm1, w_in, attn_sinks, conv_dw_w, conv_dw_b, conv_ln_g, conv_ln_b, lru_conv_w, lru_conv_b,
           lru_wa, lru_ba, lru_wx, lru_bx, lru_lambda, mix_norm, w_out, norm2, w_up, w_down, final_norm):
    B, S, D = x.shape
    depth = w_in.shape[0]
    qscale = jnp.concatenate([jnp.full((ATTN_WIDTH,), HEAD_DIM ** -0.5, F32),
                              jnp.ones((IN_WIDTH - ATTN_WIDTH,), F32)])
    win = (w_in * qscale).astype(BF16)
    wg = jnp.concatenate([_block_diag(lru_wa), _block_diag(lru_wx)], axis=2).astype(BF16)
    vec = jnp.concatenate(
        [norm1, conv_dw_b, conv_ln_g, conv_ln_b, lru_conv_b, lru_ba.reshape(depth, -1), lru_bx.reshape(depth, -1),
         lru_lambda, mix_norm, norm2, jnp.broadcast_to(final_norm[None], (depth, D))], axis=1)[:, None, :]
    consts = (vec, win, conv_dw_w, lru_conv_w, wg, w_out.astype(BF16), w_up.astype(BF16), w_down.astype(BF16))
    h = x.reshape(B * S, D)
    for l in range(depth):
        h = _layer_call(h, attn_sinks, consts, S, layer=l, final=(l == depth - 1))
    return h.reshape(B, S, D)
```

```python
import functools

import jax
import jax.numpy as jnp
from jax import lax
from jax.experimental import pallas as pl
from jax.experimental.pallas import tpu as pltpu

D_MODEL = 1024
HEAD_DIM = 64
ATTN_WIDTH = 512
N_Q_HEADS = 8
N_KV_HEADS = 2
KV_WIDTH = 128
WINDOW = 128
CONV_WIDTH = 256
CONV_KERNEL = 31
LRU_WIDTH = 256
LRU_HEADS = 4
LRU_HEAD_DIM = 64
LRU_CONV_KERNEL = 4
LRU_C = 8.0
MIX_WIDTH = 1024
IN_WIDTH = 1792
D_FF = 4096
RMS_EPS = 1e-6
LN_EPS = 1e-5
MASK_VALUE = -1e30

Q_OFF, K_OFF, V_OFF, CV_OFF, CG_OFF, RX_OFF, RG_OFF = 0, 512, 640, 768, 1024, 1280, 1536
YA_OFF, YC_OFF, YL_OFF = 0, ATTN_WIDTH, ATTN_WIDTH + CONV_WIDTH

VEC_NAMES = ("norm1", "conv_b", "ln_g", "ln_b", "lru_conv_b", "gate_b", "lambda", "mix_norm", "norm2", "final_norm")
VEC_WIDTHS = dict(zip(VEC_NAMES, (D_MODEL, CONV_WIDTH, CONV_WIDTH, CONV_WIDTH, LRU_WIDTH, 2 * LRU_WIDTH,
                                  LRU_WIDTH, MIX_WIDTH, D_MODEL, D_MODEL)))
VEC_OFF = {name: sum(VEC_WIDTHS[n] for n in VEC_NAMES[:k]) for k, name in enumerate(VEC_NAMES)}

LANES = 128
SUBLANES = 8
MXU_COLS = 256

MIX_T = 512
SUB = WINDOW
N_SUB = MIX_T // SUB
FF_CHUNK = D_FF // N_SUB
FF_PIECES = FF_CHUNK // MXU_COLS
DN_PIECES = D_MODEL // MXU_COLS
FFN_SPLIT = 1
CONV_HALO = 32
LRU_HALO = 8
SCAN_LEN = SUB // SUBLANES
SCAN_PITCH = SCAN_LEN + SUBLANES
VMEM_LIMIT = 56 * 1024 * 1024

F32 = jnp.float32
BF16 = jnp.bfloat16


def _sigmoid(x):
    return 0.5 * jnp.tanh(0.5 * x) + 0.5


def _rms(x, g):
    return x * lax.rsqrt(jnp.mean(x * x, axis=-1, keepdims=True) + RMS_EPS) * g


class _Vec:
    def __init__(self, ref, name):
        self.ref, self.off, self.width = ref, VEC_OFF[name], VEC_WIDTHS[name]

    def __getitem__(self, idx):
        if idx is Ellipsis:
            return self.ref[:, self.off:self.off + self.width]
        rows, cols = idx
        return self.ref[rows, self.off + cols.start:self.off + cols.stop]


def _layer_kernel(sinks_ref, x_ref, vec_ref, win_ref, cw_ref, lcw_ref, wg_ref, wout_ref, wup_ref, wdn_ref,
                  o_ref,
                  kt_ext, v_ext, vr_ext, u_ext, xl_ext, hcar, bias_s, a_s, b_s,
                  z_s, hmid, fhn_s, fres_s, fu_s,
                  *, layer, final, blocks_per_seq, nblk):
    T = MIX_T
    i = pl.program_id(0)
    s = lax.rem(i, blocks_per_seq)
    n1_ref, cb_ref, lng_ref, lnb_ref, lcb_ref, bg_ref, lam_ref, mixg_ref, n2_ref, fn_ref = (
        _Vec(vec_ref, name) for name in VEC_NAMES)

    @pl.when(s == 0)
    def _():
        kt_ext[:, 0:WINDOW] = jnp.zeros((KV_WIDTH, WINDOW), BF16)
        v_ext[0:WINDOW, :] = jnp.zeros((WINDOW, KV_WIDTH), BF16)
        vr_ext[0:WINDOW, :] = jnp.zeros((WINDOW, KV_WIDTH), BF16)
        u_ext[0:CONV_HALO, :] = jnp.zeros((CONV_HALO, CONV_WIDTH), F32)
        xl_ext[0:LRU_HALO, :] = jnp.zeros((LRU_HALO, LRU_WIDTH), F32)
        hcar[...] = jnp.zeros((SUBLANES, LRU_WIDTH), F32)

    qi = lax.broadcasted_iota(jnp.int32, (WINDOW, 2 * WINDOW), 0)
    kc = lax.broadcasted_iota(jnp.int32, (WINDOW, 2 * WINDOW), 1)
    band = (kc > qi) & (kc <= qi + WINDOW)
    bias_s[0] = jnp.where(band, 0.0, MASK_VALUE).astype(F32)
    bias_s[1] = jnp.where(band & (kc >= WINDOW), 0.0, MASK_VALUE).astype(F32)

    mg = mixg_ref[...]
    lane = lax.broadcasted_iota(jnp.int32, (2 * WINDOW, KV_WIDTH), 1)
    band_row = lax.broadcasted_iota(jnp.int32, (2 * WINDOW, KV_WIDTH), 0)
    keep_lo = (lane < HEAD_DIM) & (band_row != 0)
    drop_hi = (lane < HEAD_DIM) | (band_row == 0)
    lane_o = lax.broadcasted_iota(jnp.int32, (WINDOW, LANES), 1)
    zk = jnp.zeros((HEAD_DIM, 2 * WINDOW), BF16)
    row8 = lax.broadcasted_iota(jnp.int32, (SUBLANES, LANES), 0)
    L, P = SCAN_LEN, SCAN_PITCH

    def ffn_prep(r):
        rows = slice(r * SUB, (r + 1) * SUB)
        h = hmid[rows, :]
        fhn_s[rows, :] = _rms(h, n2_ref[...]).astype(BF16)
        fres_s[rows, :] = h

    def ffn_up(c, p):
        n0 = (p % 2) * 2 * MXU_COLS
        for q in range(FFN_SPLIT):
            r0 = (p // 2) * (T // 2) + q * (T // 2 // FFN_SPLIT)
            rows = slice(r0, r0 + T // 2 // FFN_SPLIT)
            u = jnp.dot(fhn_s[rows, :], wup_ref[:, c * FF_CHUNK + n0:c * FF_CHUNK + n0 + 2 * MXU_COLS],
                        preferred_element_type=F32).astype(BF16)
            u = jnp.maximum(u, 0.0)
            fu_s[c % 2, rows, n0:n0 + 2 * MXU_COLS] = u * u

    def ffn_down(c, p):
        cols = slice((p % 2) * 2 * MXU_COLS, (p % 2 + 1) * 2 * MXU_COLS)
        for q in range(FFN_SPLIT):
            r0 = (p // 2) * (T // 2) + q * (T // 2 // FFN_SPLIT)
            rows = slice(r0, r0 + T // 2 // FFN_SPLIT)
            d = jnp.dot(fu_s[c % 2, rows, :], wdn_ref[c * FF_CHUNK:(c + 1) * FF_CHUNK, cols],
                        preferred_element_type=F32)
            if c == 0:
                o_ref[rows, cols] = fres_s[rows, cols] + d
            else:
                o_ref[rows, cols] += d

    def ffn_finish(half):
        if final:
            rows = slice(half * (T // 2), (half + 1) * (T // 2))
            o_ref[rows, :] = _rms(o_ref[rows, :], fn_ref[...])

    def inproj(half):
        rows = slice(half * 2 * SUB, (half + 1) * 2 * SUB)
        hn = _rms(x_ref[rows, :], n1_ref[...]).astype(BF16)
        z_s[rows, :] = jnp.dot(hn, win_ref[...], preferred_element_type=F32)

    def attn_qk(r, st):
        r0 = r * SUB
        q = z_s[r0:r0 + SUB, Q_OFF:Q_OFF + ATTN_WIDTH].astype(BF16)
        kf = z_s[r0:r0 + SUB, K_OFF:K_OFF + KV_WIDTH]
        vf = z_s[r0:r0 + SUB, V_OFF:V_OFF + KV_WIDTH]
        kt_ext[:, WINDOW + r0:WINDOW + r0 + SUB] = kf.T.astype(BF16)
        v_ext[WINDOW + r0:WINDOW + r0 + SUB, :] = vf.astype(BF16)
        vr_ext[WINDOW + r0:WINDOW + r0 + SUB, :] = pltpu.roll(vf, HEAD_DIM, axis=1).astype(BF16)
        kt_band = kt_ext[:, r0:r0 + 2 * WINDOW]
        st["sc"] = []
        for h in range(N_KV_HEADS):
            kh = kt_band[h * HEAD_DIM:(h + 1) * HEAD_DIM, :]
            kt2 = jnp.concatenate(
                [jnp.concatenate([kh, zk], axis=0), jnp.concatenate([zk, kh], axis=0)], axis=1)
            c0 = h * 2 * LANES
            qs = jnp.concatenate([q[:, c0:c0 + LANES], q[:, c0 + LANES:c0 + 2 * LANES]], axis=0)
            st["sc"].append(jnp.dot(qs, kt2, preferred_element_type=F32))

    def attn_softmax(r, h, st):
        bias = bias_s[jnp.where(s == 0, 1, 0)] if r == 0 else bias_s[0]
        sc = st["sc"][h]
        p_rows = []
        dens = []
        for rr in range(2):
            p_cols = []
            den_r = []
            for e in range(2):
                sink = sinks_ref[layer, 4 * h + 2 * rr + e]
                rws = slice(rr * WINDOW, (rr + 1) * WINDOW)
                c0 = e * 2 * WINDOW
                t = jnp.concatenate(
                    [jnp.where(lane_o == 0, sink, sc[rws, c0:c0 + LANES] + bias[:, :LANES]),
                     sc[rws, c0 + LANES:c0 + 2 * LANES] + bias[:, LANES:]], axis=1)
                p = jnp.exp(t - jnp.max(t, axis=-1, keepdims=True))
                den_r.append(jnp.sum(p, axis=-1, keepdims=True))
                p_cols.append(p.astype(BF16))
            p_rows.append(jnp.concatenate(p_cols, axis=1))
            dens.append(den_r)
        st[("pm", h)] = jnp.concatenate(p_rows, axis=0)
        st[("den", h)] = dens

    def attn_pv(r, h, st):
        r0 = r * SUB
        v_band = v_ext[r0:r0 + 2 * WINDOW, :]
        vr_band = vr_ext[r0:r0 + 2 * WINDOW, :]
        if h == 0:
            va = jnp.where(keep_lo, v_band, jnp.zeros_like(v_band))
            vb = jnp.where(drop_hi, jnp.zeros_like(vr_band), vr_band)
        else:
            va = jnp.where(keep_lo, vr_band, jnp.zeros_like(vr_band))
            vb = jnp.where(drop_hi, jnp.zeros_like(v_band), v_band)
        v2 = jnp.concatenate([va, vb], axis=0)
        o = jnp.dot(st[("pm", h)], v2, preferred_element_type=F32)
        dens = st[("den", h)]
        for rr in range(2):
            den = jnp.where(lane_o < HEAD_DIM, dens[rr][0], dens[rr][1])
            st[("ya", 2 * h + rr)] = o[rr * WINDOW:(rr + 1) * WINDOW, :] / den

    def attn_norm(r, st):
        y_attn = jnp.concatenate([st[("ya", c)] for c in range(ATTN_WIDTH // LANES)], axis=1)
        st["ya_n"] = _rms(y_attn, mg[:, YA_OFF:YA_OFF + ATTN_WIDTH]).astype(BF16)

    def attn_dot(r, st):
        rows = slice(r * SUB, (r + 1) * SUB)
        hmid[rows, :] = x_ref[rows, :] + jnp.dot(st["ya_n"], wout_ref[YA_OFF:YA_OFF + ATTN_WIDTH, :],
                                                 preferred_element_type=F32)

    def conv_unit(r, lt, st):
        r0 = r * SUB
        cl = slice(lt * LANES, (lt + 1) * LANES)
        cval = z_s[r0:r0 + SUB, CV_OFF + lt * LANES:CV_OFF + (lt + 1) * LANES]
        cgate = z_s[r0:r0 + SUB, CG_OFF + lt * LANES:CG_OFF + (lt + 1) * LANES]
        u_ext[CONV_HALO + r0:CONV_HALO + r0 + SUB, cl] = cval * _sigmoid(cgate)
        base = CONV_HALO - (CONV_KERNEL - 1)
        acc = jnp.broadcast_to(cb_ref[:, cl], (SUB, LANES))
        for sh in range(SUBLANES):
            part = None
            nrows = SUB + (SUBLANES if sh else 0)
            for k in range(CONV_KERNEL):
                if (base + k) % SUBLANES != sh:
                    continue
                al = r0 + base + k - sh
                term = cw_ref[k:k + 1, cl] * u_ext[al:al + nrows, cl]
                part = term if part is None else part + term
            acc = acc + part[sh:sh + SUB, :]
        st[("yc", lt)] = acc

    def conv_post(r, st):
        uc = jnp.concatenate([st[("yc", lt)] for lt in range(CONV_WIDTH // LANES)], axis=1)
        mu = jnp.mean(uc, axis=-1, keepdims=True)
        xc_ = uc - mu
        ln = xc_ * lax.rsqrt(jnp.mean(xc_ * xc_, axis=-1, keepdims=True) + LN_EPS) * lng_ref[...] + lnb_ref[...]
        y_conv = ln * _sigmoid(ln)
        st["yc_n"] = _rms(y_conv, mg[:, YC_OFF:YC_OFF + CONV_WIDTH]).astype(BF16)

    def conv_dot(r, st):
        hmid[r * SUB:(r + 1) * SUB, :] += jnp.dot(st["yc_n"], wout_ref[YC_OFF:YC_OFF + CONV_WIDTH, :],
                                                  preferred_element_type=F32)

    def lru_pre(r, st):
        r0 = r * SUB
        xl_ext[LRU_HALO + r0:LRU_HALO + r0 + SUB, :] = z_s[r0:r0 + SUB, RX_OFF:RX_OFF + LRU_WIDTH]
        lbase = LRU_HALO - (LRU_CONV_KERNEL - 1)
        xc = jnp.broadcast_to(lcb_ref[...], (SUB, LRU_WIDTH))
        for k in range(LRU_CONV_KERNEL):
            xc = xc + lcw_ref[k:k + 1, :] * xl_ext[r0 + lbase + k:r0 + lbase + k + SUB, :]
        st["xc"] = xc
        st["xc_b"] = xc.astype(BF16)

    def lru_gate_dot(r, st):
        st["gates"] = jnp.dot(st["xc_b"], wg_ref[...], preferred_element_type=F32) + bg_ref[...]

    def lru_scan(r, st):
        r0 = r * SUB
        xc, gates = st["xc"], st["gates"]
        rgate = _sigmoid(gates[:, :LRU_WIDTH])
        igate = _sigmoid(gates[:, LRU_WIDTH:])
        log_a = (-LRU_C * rgate) * jax.nn.softplus(-lam_ref[...])
        a_full = jnp.exp(log_a)
        th = jnp.tanh(log_a)
        b_full = jnp.sqrt(2.0 * th / (th - 1.0)) * (igate * xc)
        yl_cols = []
        for lt in range(LRU_WIDTH // LANES):
            cl = slice(lt * LANES, (lt + 1) * LANES)
            for c in range(SUBLANES):
                a_s[lt, c * P:c * P + L, :] = a_full[c * L:(c + 1) * L, cl]
                b_s[lt, c * P:c * P + L, :] = b_full[c * L:(c + 1) * L, cl]
            hloc = jnp.zeros((SUBLANES, LANES), F32)
            cum = jnp.ones((SUBLANES, LANES), F32)
            for m in range(L):
                am = a_s[lt, pl.ds(m, SUBLANES, stride=P), :]
                bm = b_s[lt, pl.ds(m, SUBLANES, stride=P), :]
                hloc = am * hloc + bm
                cum = am * cum
                b_s[lt, pl.ds(m, SUBLANES, stride=P), :] = hloc
                a_s[lt, pl.ds(m, SUBLANES, stride=P), :] = cum
            ca, cbv = cum, hloc
            for d in (1, 2, 4):
                a_sh = jnp.where(row8 >= d, pltpu.roll(ca, d, axis=0), 1.0)
                b_sh = jnp.where(row8 >= d, pltpu.roll(cbv, d, axis=0), 0.0)
                cbv = ca * b_sh + cbv
                ca = ca * a_sh
            hprev = hcar[:, cl]
            ends = ca * hprev + cbv
            carry_in = jnp.where(row8 == 0, hprev, pltpu.roll(ends, 1, axis=0))
            hcar[:, cl] = jnp.broadcast_to(ends[SUBLANES - 1:SUBLANES, :], (SUBLANES, LANES))
            parts = []
            for c in range(SUBLANES):
                g = jnp.broadcast_to(carry_in[c:c + 1, :], (L, LANES))
                parts.append(b_s[lt, c * P:c * P + L, :] + a_s[lt, c * P:c * P + L, :] * g)
            yl_cols.append(jnp.concatenate(parts, axis=0))
        y_lru = jnp.concatenate(yl_cols, axis=1) * jax.nn.gelu(z_s[r0:r0 + SUB, RG_OFF:RG_OFF + LRU_WIDTH])
        st["yl_n"] = _rms(y_lru, mg[:, YL_OFF:YL_OFF + LRU_WIDTH]).astype(BF16)

    def lru_dot(r, st):
        hmid[r * SUB:(r + 1) * SUB, :] += jnp.dot(st["yl_n"], wout_ref[YL_OFF:YL_OFF + LRU_WIDTH, :],
                                                  preferred_element_type=F32)

    def emit(run_ffn, run_mixer):
        pieces = []
        if run_ffn:
            for c in range(N_SUB):
                pieces += [functools.partial(ffn_up, c, p) for p in range(4)]
                pieces += [functools.partial(ffn_down, c, p) for p in range(4)]
        n_emitted = [0]
        n_prepped = [0]

        def next_ffn(mixer_rows_done):
            k = n_emitted[0]
            if k >= len(pieces):
                return
            pieces[k]()
            n_emitted[0] += 1
            if run_mixer and k >= N_SUB * 8 - 5:
                while n_prepped[0] < mixer_rows_done:
                    ffn_prep(n_prepped[0])
                    n_prepped[0] += 1
            if k == N_SUB * 8 - 3:
                ffn_finish(0)
            if k == N_SUB * 8 - 1:
                ffn_finish(1)

        if not run_mixer:
            for _ in pieces:
                next_ffn(0)
            return

        next_ffn(0)
        inproj(0)
        for r in range(N_SUB):
            st = {}
            last = r == N_SUB - 1
            attn_qk(r, st)
            if r > 0:
                next_ffn(r)
            attn_softmax(r, 0, st)
            attn_softmax(r, 1, st)
            next_ffn(r)
            next_ffn(r)
            attn_pv(r, 0, st)
            attn_pv(r, 1, st)
            attn_norm(r, st)
            conv_unit(r, 0, st)
            next_ffn(r)
            attn_dot(r, st)
            conv_unit(r, 1, st)
            next_ffn(r)
            conv_post(r, st)
            lru_pre(r, st)
            next_ffn(r)
            conv_dot(r, st)
            lru_gate_dot(r, st)
            if r % 2 == 0 and r + 2 < N_SUB:
                inproj(r // 2 + 1)
            lru_scan(r, st)
            next_ffn(r)
            if not last:
                next_ffn(r)
            lru_dot(r, st)
            if last:
                next_ffn(r + 1)
        while n_prepped[0] < N_SUB:
            ffn_prep(n_prepped[0])
            n_prepped[0] += 1

        kt_ext[:, 0:WINDOW] = kt_ext[:, T:T + WINDOW]
        v_ext[0:WINDOW, :] = v_ext[T:T + WINDOW, :]
        vr_ext[0:WINDOW, :] = vr_ext[T:T + WINDOW, :]
        u_ext[0:CONV_HALO, :] = u_ext[T:T + CONV_HALO, :]
        xl_ext[0:LRU_HALO, :] = xl_ext[T:T + LRU_HALO, :]

    @pl.when(i == 0)
    def _():
        fhn_s[...] = jnp.zeros((T, D_MODEL), BF16)
        fres_s[...] = jnp.zeros((T, D_MODEL), F32)

    emit(True, True)


def _layer_spec(arr, layer):
    nd = arr.ndim
    return pl.BlockSpec((None,) + arr.shape[1:], lambda i: (layer,) + (0,) * (nd - 1),
                        pipeline_mode=pl.Buffered(1))


def _layer_call(x, sinks, consts, seq_len, layer, final):
    M, D = x.shape
    T = MIX_T
    nblk = M // T
    return pl.pallas_call(
        functools.partial(_layer_kernel, layer=layer, final=final, blocks_per_seq=seq_len // T, nblk=nblk),
        out_shape=jax.ShapeDtypeStruct((M, D), F32),
        grid=(nblk + 1,),
        in_specs=[pl.BlockSpec(memory_space=pltpu.SMEM),
                  pl.BlockSpec((T, D), lambda i: (jnp.minimum(i, nblk - 1), 0))]
                 + [_layer_spec(c, layer) for c in consts],
        out_specs=pl.BlockSpec((T, D), lambda i: (jnp.maximum(i - 1, 0), 0)),
        scratch_shapes=[
            pltpu.VMEM((KV_WIDTH, WINDOW + T), BF16),
            pltpu.VMEM((WINDOW + T, KV_WIDTH), BF16),
            pltpu.VMEM((WINDOW + T, KV_WIDTH), BF16),
            pltpu.VMEM((CONV_HALO + T, CONV_WIDTH), F32),
            pltpu.VMEM((LRU_HALO + T, LRU_WIDTH), F32),
            pltpu.VMEM((SUBLANES, LRU_WIDTH), F32),
            pltpu.VMEM((2, WINDOW, 2 * WINDOW), F32),
            pltpu.VMEM((LRU_WIDTH // LANES, SUBLANES * SCAN_PITCH, LANES), F32),
            pltpu.VMEM((LRU_WIDTH // LANES, SUBLANES * SCAN_PITCH, LANES), F32),
            pltpu.VMEM((T, IN_WIDTH), F32),
            pltpu.VMEM((T, D), F32),
            pltpu.VMEM((T, D), BF16),
            pltpu.VMEM((T, D), F32),
            pltpu.VMEM((2, T, FF_CHUNK), BF16),
        ],
        compiler_params=pltpu.CompilerParams(
            dimension_semantics=("arbitrary",), vmem_limit_bytes=VMEM_LIMIT),
        name="layer_final" if final else "layer",
    )(sinks, x, *consts)


def _block_diag(w):
    dp, hN, di, dj = w.shape
    eye = jnp.eye(hN, dtype=w.dtype)
    return (eye[None, :, None, :, None] * w[:, :, :, None, :]).reshape(dp, hN * di, hN * dj)


def kernel(x, norm1, w_in, attn_sinks, conv_dw_w, conv_dw_b, conv_ln_g, conv_ln_b, lru_conv_w, lru_conv_b,
           lru_wa, lru_ba, lru_wx, lru_bx, lru_lambda, mix_norm, w_out, norm2, w_up, w_down, final_norm):
    B, S, D = x.shape
    depth = w_in.shape[0]
    qscale = jnp.concatenate([jnp.full((ATTN_WIDTH,), HEAD_DIM ** -0.5, F32),
                              jnp.ones((IN_WIDTH - ATTN_WIDTH,), F32)])
    win = (w_in * qscale).astype(BF16)
    wg = jnp.concatenate([_block_diag(lru_wa), _block_diag(lru_wx)], axis=2).astype(BF16)
    vec = jnp.concatenate(
        [norm1, conv_dw_b, conv_ln_g, conv_ln_b, lru_conv_b, lru_ba.reshape(depth, -1), lru_bx.reshape(depth, -1),
         lru_lambda, mix_norm, norm2, jnp.broadcast_to(final_norm[None], (depth, D))], axis=1)[:, None, :]
    consts = (vec, win, conv_dw_w, lru_conv_w, wg, w_out.astype(BF16), w_up.astype(BF16), w_down.astype(BF16))
    h = x.reshape(B * S, D)
    for l in range(depth):
        h = _layer_call(h, attn_sinks, consts, S, layer=l, final=(l == depth - 1))
    return h.reshape(B, S, D)
```

```python
import functools

import jax
import jax.numpy as jnp
from jax import lax
from jax.experimental import pallas as pl
from jax.experimental.pallas import tpu as pltpu

D_MODEL = 1024
HEAD_DIM = 64
ATTN_WIDTH = 512
N_Q_HEADS = 8
N_KV_HEADS = 2
KV_WIDTH = 128
WINDOW = 128
CONV_WIDTH = 256
CONV_KERNEL = 31
LRU_WIDTH = 256
LRU_HEADS = 4
LRU_HEAD_DIM = 64
LRU_CONV_KERNEL = 4
LRU_C = 8.0
MIX_WIDTH = 1024
IN_WIDTH = 1792
D_FF = 4096
RMS_EPS = 1e-6
LN_EPS = 1e-5
MASK_VALUE = -1e30

Q_OFF, K_OFF, V_OFF, CV_OFF, CG_OFF, RX_OFF, RG_OFF = 0, 512, 640, 768, 1024, 1280, 1536
YA_OFF, YC_OFF, YL_OFF = 0, ATTN_WIDTH, ATTN_WIDTH + CONV_WIDTH

VEC_NAMES = ("norm1", "conv_b", "ln_g", "ln_b", "lru_conv_b", "gate_b", "lambda", "mix_norm", "norm2", "final_norm")
VEC_WIDTHS = dict(zip(VEC_NAMES, (D_MODEL, CONV_WIDTH, CONV_WIDTH, CONV_WIDTH, LRU_WIDTH, 2 * LRU_WIDTH,
                                  LRU_WIDTH, MIX_WIDTH, D_MODEL, D_MODEL)))
VEC_OFF = {name: sum(VEC_WIDTHS[n] for n in VEC_NAMES[:k]) for k, name in enumerate(VEC_NAMES)}

LANES = 128
SUBLANES = 8
MXU_COLS = 256

MIX_T = 512
SUB = WINDOW
N_SUB = MIX_T // SUB
FF_CHUNK = D_FF // N_SUB
FF_PIECES = FF_CHUNK // MXU_COLS
DN_PIECES = D_MODEL // MXU_COLS
FFN_SPLIT = 1
CONV_HALO = 32
LRU_HALO = 8
SCAN_LEN = SUB // SUBLANES
SCAN_PITCH = SCAN_LEN + SUBLANES
VMEM_LIMIT = 56 * 1024 * 1024

F32 = jnp.float32
BF16 = jnp.bfloat16


def _sigmoid(x):
    return 0.5 * jnp.tanh(0.5 * x) + 0.5


def _rms(x, g):
    return x * lax.rsqrt(jnp.mean(x * x, axis=-1, keepdims=True) + RMS_EPS) * g


class _Vec:
    def __init__(self, ref, name):
        self.ref, self.off, self.width = ref, VEC_OFF[name], VEC_WIDTHS[name]

    def __getitem__(self, idx):
        if idx is Ellipsis:
            return self.ref[:, self.off:self.off + self.width]
        rows, cols = idx
        return self.ref[rows, self.off + cols.start:self.off + cols.stop]


def _layer_kernel(sinks_ref, x_ref, vec_ref, win_ref, cw_ref, lcw_ref, wg_ref, wout_ref, wup_ref, wdn_ref,
                  o_ref,
                  kt_ext, v_ext, vr_ext, u_ext, xl_ext, hcar, bias_s, a_s, b_s,
                  z_s, hmid, fhn_s, fres_s, fu_s,
                  *, layer, final, blocks_per_seq, nblk):
    T = MIX_T
    i = pl.program_id(0)
    s = lax.rem(i, blocks_per_seq)
    n1_ref, cb_ref, lng_ref, lnb_ref, lcb_ref, bg_ref, lam_ref, mixg_ref, n2_ref, fn_ref = (
        _Vec(vec_ref, name) for name in VEC_NAMES)

    @pl.when(s == 0)
    def _():
        kt_ext[:, 0:WINDOW] = jnp.zeros((KV_WIDTH, WINDOW), BF16)
        v_ext[0:WINDOW, :] = jnp.zeros((WINDOW, KV_WIDTH), BF16)
        vr_ext[0:WINDOW, :] = jnp.zeros((WINDOW, KV_WIDTH), BF16)
        u_ext[0:CONV_HALO, :] = jnp.zeros((CONV_HALO, CONV_WIDTH), F32)
        xl_ext[0:LRU_HALO, :] = jnp.zeros((LRU_HALO, LRU_WIDTH), F32)
        hcar[...] = jnp.zeros((SUBLANES, LRU_WIDTH), F32)

    qi = lax.broadcasted_iota(jnp.int32, (WINDOW, 2 * WINDOW), 0)
    kc = lax.broadcasted_iota(jnp.int32, (WINDOW, 2 * WINDOW), 1)
    band = (kc > qi) & (kc <= qi + WINDOW)
    bias_s[0] = jnp.where(band, 0.0, MASK_VALUE).astype(F32)
    bias_s[1] = jnp.where(band & (kc >= WINDOW), 0.0, MASK_VALUE).astype(F32)

    mg = mixg_ref[...]
    lane = lax.broadcasted_iota(jnp.int32, (2 * WINDOW, KV_WIDTH), 1)
    band_row = lax.broadcasted_iota(jnp.int32, (2 * WINDOW, KV_WIDTH), 0)
    keep_lo = (lane < HEAD_DIM) & (band_row != 0)
    drop_hi = (lane < HEAD_DIM) | (band_row == 0)
    lane_o = lax.broadcasted_iota(jnp.int32, (WINDOW, LANES), 1)
    zk = jnp.zeros((HEAD_DIM, 2 * WINDOW), BF16)
    row8 = lax.broadcasted_iota(jnp.int32, (SUBLANES, LANES), 0)
    L, P = SCAN_LEN, SCAN_PITCH

    def ffn_prep(r):
        rows = slice(r * SUB, (r + 1) * SUB)
        h = hmid[rows, :]
        fhn_s[rows, :] = _rms(h, n2_ref[...]).astype(BF16)
        fres_s[rows, :] = h

    def ffn_up(c, p):
        n0 = (p % 2) * 2 * MXU_COLS
        for q in range(FFN_SPLIT):
            r0 = (p // 2) * (T // 2) + q * (T // 2 // FFN_SPLIT)
            rows = slice(r0, r0 + T // 2 // FFN_SPLIT)
            u = jnp.dot(fhn_s[rows, :], wup_ref[:, c * FF_CHUNK + n0:c * FF_CHUNK + n0 + 2 * MXU_COLS],
                        preferred_element_type=F32).astype(BF16)
            u = jnp.maximum(u, 0.0)
            fu_s[c % 2, rows, n0:n0 + 2 * MXU_COLS] = u * u

    def ffn_down(c, p):
        cols = slice((p % 2) * 2 * MXU_COLS, (p % 2 + 1) * 2 * MXU_COLS)
        for q in range(FFN_SPLIT):
            r0 = (p // 2) * (T // 2) + q * (T // 2 // FFN_SPLIT)
            rows = slice(r0, r0 + T // 2 // FFN_SPLIT)
            d = jnp.dot(fu_s[c % 2, rows, :], wdn_ref[c * FF_CHUNK:(c + 1) * FF_CHUNK, cols],
                        preferred_element_type=F32)
            if c == 0:
                o_ref[rows, cols] = fres_s[rows, cols] + d
            else:
                o_ref[rows, cols] += d

    def ffn_finish(half):
        if final:
            rows = slice(half * (T // 2), (half + 1) * (T // 2))
            o_ref[rows, :] = _rms(o_ref[rows, :], fn_ref[...])

    def inproj(half):
        rows = slice(half * 2 * SUB, (half + 1) * 2 * SUB)
        hn = _rms(x_ref[rows, :], n1_ref[...]).astype(BF16)
        z_s[rows, :] = jnp.dot(hn, win_ref[...], preferred_element_type=F32)

    def attn_qk(r, st):
        r0 = r * SUB
        q = z_s[r0:r0 + SUB, Q_OFF:Q_OFF + ATTN_WIDTH].astype(BF16)
        kf = z_s[r0:r0 + SUB, K_OFF:K_OFF + KV_WIDTH]
        vf = z_s[r0:r0 + SUB, V_OFF:V_OFF + KV_WIDTH]
        kt_ext[:, WINDOW + r0:WINDOW + r0 + SUB] = kf.T.astype(BF16)
        v_ext[WINDOW + r0:WINDOW + r0 + SUB, :] = vf.astype(BF16)
        vr_ext[WINDOW + r0:WINDOW + r0 + SUB, :] = pltpu.roll(vf, HEAD_DIM, axis=1).astype(BF16)
        kt_band = kt_ext[:, r0:r0 + 2 * WINDOW]
        st["sc"] = []
        for h in range(N_KV_HEADS):
            kh = kt_band[h * HEAD_DIM:(h + 1) * HEAD_DIM, :]
            kt2 = jnp.concatenate(
                [jnp.concatenate([kh, zk], axis=0), jnp.concatenate([zk, kh], axis=0)], axis=1)
            c0 = h * 2 * LANES
            qs = jnp.concatenate([q[:, c0:c0 + LANES], q[:, c0 + LANES:c0 + 2 * LANES]], axis=0)
            st["sc"].append(jnp.dot(qs, kt2, preferred_element_type=F32))

    def attn_softmax(r, h, st):
        bias = bias_s[jnp.where(s == 0, 1, 0)] if r == 0 else bias_s[0]
        sc = st["sc"][h]
        p_rows = []
        dens = []
        for rr in range(2):
            p_cols = []
            den_r = []
            for e in range(2):
                sink = sinks_ref[layer, 4 * h + 2 * rr + e]
                rws = slice(rr * WINDOW, (rr + 1) * WINDOW)
                c0 = e * 2 * WINDOW
                t = jnp.concatenate(
                    [jnp.where(lane_o == 0, sink, sc[rws, c0:c0 + LANES] + bias[:, :LANES]),
                     sc[rws, c0 + LANES:c0 + 2 * LANES] + bias[:, LANES:]], axis=1)
                p = jnp.exp(t - jnp.max(t, axis=-1, keepdims=True))
                den_r.append(jnp.sum(p, axis=-1, keepdims=True))
                p_cols.append(p.astype(BF16))
            p_rows.append(jnp.concatenate(p_cols, axis=1))
            dens.append(den_r)
        st[("pm", h)] = jnp.concatenate(p_rows, axis=0)
        st[("den", h)] = dens

    def attn_pv(r, h, st):
        r0 = r * SUB
        v_band = v_ext[r0:r0 + 2 * WINDOW, :]
        vr_band = vr_ext[r0:r0 + 2 * WINDOW, :]
        if h == 0:
            va = jnp.where(keep_lo, v_band, jnp.zeros_like(v_band))
            vb = jnp.where(drop_hi, jnp.zeros_like(vr_band), vr_band)
        else:
            va = jnp.where(keep_lo, vr_band, jnp.zeros_like(vr_band))
            vb = jnp.where(drop_hi, jnp.zeros_like(v_band), v_band)
        v2 = jnp.concatenate([va, vb], axis=0)
        o = jnp.dot(st[("pm", h)], v2, preferred_element_type=F32)
        dens = st[("den", h)]
        for rr in range(2):
            den = jnp.where(lane_o < HEAD_DIM, dens[rr][0], dens[rr][1])
            st[("ya", 2 * h + rr)] = o[rr * WINDOW:(rr + 1) * WINDOW, :] / den

    def attn_norm(r, st):
        y_attn = jnp.concatenate([st[("ya", c)] for c in range(ATTN_WIDTH // LANES)], axis=1)
        st["ya_n"] = _rms(y_attn, mg[:, YA_OFF:YA_OFF + ATTN_WIDTH]).astype(BF16)

    def attn_dot(r, st):
        rows = slice(r * SUB, (r + 1) * SUB)
        hmid[rows, :] = x_ref[rows, :] + jnp.dot(st["ya_n"], wout_ref[YA_OFF:YA_OFF + ATTN_WIDTH, :],
                                                 preferred_element_type=F32)

    def conv_unit(r, lt, st):
        r0 = r * SUB
        cl = slice(lt * LANES, (lt + 1) * LANES)
        cval = z_s[r0:r0 + SUB, CV_OFF + lt * LANES:CV_OFF + (lt + 1) * LANES]
        cgate = z_s[r0:r0 + SUB, CG_OFF + lt * LANES:CG_OFF + (lt + 1) * LANES]
        u_ext[CONV_HALO + r0:CONV_HALO + r0 + SUB, cl] = cval * _sigmoid(cgate)
        base = CONV_HALO - (CONV_KERNEL - 1)
        acc = jnp.broadcast_to(cb_ref[:, cl], (SUB, LANES))
        for sh in range(SUBLANES):
            part = None
            nrows = SUB + (SUBLANES if sh else 0)
            for k in range(CONV_KERNEL):
                if (base + k) % SUBLANES != sh:
                    continue
                al = r0 + base + k - sh
                term = cw_ref[k:k + 1, cl] * u_ext[al:al + nrows, cl]
                part = term if part is None else part + term
            acc = acc + part[sh:sh + SUB, :]
        st[("yc", lt)] = acc

    def conv_post(r, st):
        uc = jnp.concatenate([st[("yc", lt)] for lt in range(CONV_WIDTH // LANES)], axis=1)
        mu = jnp.mean(uc, axis=-1, keepdims=True)
        xc_ = uc - mu
        ln = xc_ * lax.rsqrt(jnp.mean(xc_ * xc_, axis=-1, keepdims=True) + LN_EPS) * lng_ref[...] + lnb_ref[...]
        y_conv = ln * _sigmoid(ln)
        st["yc_n"] = _rms(y_conv, mg[:, YC_OFF:YC_OFF + CONV_WIDTH]).astype(BF16)

    def conv_dot(r, st):
        hmid[r * SUB:(r + 1) * SUB, :] += jnp.dot(st["yc_n"], wout_ref[YC_OFF:YC_OFF + CONV_WIDTH, :],
                                                  preferred_element_type=F32)

    def lru_pre(r, st):
        r0 = r * SUB
        xl_ext[LRU_HALO + r0:LRU_HALO + r0 + SUB, :] = z_s[r0:r0 + SUB, RX_OFF:RX_OFF + LRU_WIDTH]
        lbase = LRU_HALO - (LRU_CONV_KERNEL - 1)
        xc = jnp.broadcast_to(lcb_ref[...], (SUB, LRU_WIDTH))
        for k in range(LRU_CONV_KERNEL):
            xc = xc + lcw_ref[k:k + 1, :] * xl_ext[r0 + lbase + k:r0 + lbase + k + SUB, :]
        st["xc"] = xc
        st["xc_b"] = xc.astype(BF16)

    def lru_gate_dot(r, st):
        st["gates"] = jnp.dot(st["xc_b"], wg_ref[...], preferred_element_type=F32) + bg_ref[...]

    def lru_scan(r, st):
        r0 = r * SUB
        xc, gates = st["xc"], st["gates"]
        rgate = _sigmoid(gates[:, :LRU_WIDTH])
        igate = _sigmoid(gates[:, LRU_WIDTH:])
        log_a = (-LRU_C * rgate) * jax.nn.softplus(-lam_ref[...])
        a_full = jnp.exp(log_a)
        th = jnp.tanh(log_a)
        b_full = jnp.sqrt(2.0 * th / (th - 1.0)) * (igate * xc)
        yl_cols = []
        for lt in range(LRU_WIDTH // LANES):
            cl = slice(lt * LANES, (lt + 1) * LANES)
            for c in range(SUBLANES):
                a_s[lt, c * P:c * P + L, :] = a_full[c * L:(c + 1) * L, cl]
                b_s[lt, c * P:c * P + L, :] = b_full[c * L:(c + 1) * L, cl]
            hloc = jnp.zeros((SUBLANES, LANES), F32)
            cum = jnp.ones((SUBLANES, LANES), F32)
            for m in range(L):
                am = a_s[lt, pl.ds(m, SUBLANES, stride=P), :]
                bm = b_s[lt, pl.ds(m, SUBLANES, stride=P), :]
                hloc = am * hloc + bm
                cum = am * cum
                b_s[lt, pl.ds(m, SUBLANES, stride=P), :] = hloc
                a_s[lt, pl.ds(m, SUBLANES, stride=P), :] = cum
            ca, cbv = cum, hloc
            for d in (1, 2, 4):
                a_sh = jnp.where(row8 >= d, pltpu.roll(ca, d, axis=0), 1.0)
                b_sh = jnp.where(row8 >= d, pltpu.roll(cbv, d, axis=0), 0.0)
                cbv = ca * b_sh + cbv
                ca = ca * a_sh
            hprev = hcar[:, cl]
            ends = ca * hprev + cbv
            carry_in = jnp.where(row8 == 0, hprev, pltpu.roll(ends, 1, axis=0))
            hcar[:, cl] = jnp.broadcast_to(ends[SUBLANES - 1:SUBLANES, :], (SUBLANES, LANES))
            parts = []
            for c in range(SUBLANES):
                g = jnp.broadcast_to(carry_in[c:c + 1, :], (L, LANES))
                parts.append(b_s[lt, c * P:c * P + L, :] + a_s[lt, c * P:c * P + L, :] * g)
            yl_cols.append(jnp.concatenate(parts, axis=0))
        y_lru = jnp.concatenate(yl_cols, axis=1) * jax.nn.gelu(z_s[r0:r0 + SUB, RG_OFF:RG_OFF + LRU_WIDTH])
        st["yl_n"] = _rms(y_lru, mg[:, YL_OFF:YL_OFF + LRU_WIDTH]).astype(BF16)

    def lru_dot(r, st):
        hmid[r * SUB:(r + 1) * SUB, :] += jnp.dot(st["yl_n"], wout_ref[YL_OFF:YL_OFF + LRU_WIDTH, :],
                                                  preferred_element_type=F32)

    def emit(run_ffn, run_mixer):
        pieces = []
        if run_ffn:
            pieces += [functools.partial(ffn_up, 0, p) for p in range(4)]
            for c in range(N_SUB):
                for p in range(4):
                    if c + 1 < N_SUB:
                        pieces.append(functools.partial(ffn_up, c + 1, p))
                    pieces.append(functools.partial(ffn_down, c, p))
        last_up = N_SUB * 8 - 6
        n_emitted = [0]
        n_prepped = [0]

        def next_ffn(mixer_rows_done):
            k = n_emitted[0]
            if k >= len(pieces):
                return
            pieces[k]()
            n_emitted[0] += 1
            if run_mixer and k >= last_up:
                while n_prepped[0] < mixer_rows_done:
                    ffn_prep(n_prepped[0])
                    n_prepped[0] += 1
            if k == N_SUB * 8 - 3:
                ffn_finish(0)
            if k == N_SUB * 8 - 1:
                ffn_finish(1)

        if not run_mixer:
            for _ in pieces:
                next_ffn(0)
            return

        next_ffn(0)
        inproj(0)
        for r in range(N_SUB):
            st = {}
            last = r == N_SUB - 1
            attn_qk(r, st)
            if r > 0:
                next_ffn(r)
            attn_softmax(r, 0, st)
            attn_softmax(r, 1, st)
            next_ffn(r)
            next_ffn(r)
            attn_pv(r, 0, st)
            attn_pv(r, 1, st)
            attn_norm(r, st)
            conv_unit(r, 0, st)
            next_ffn(r)
            attn_dot(r, st)
            conv_unit(r, 1, st)
            next_ffn(r)
            conv_post(r, st)
            lru_pre(r, st)
            next_ffn(r)
            conv_dot(r, st)
            lru_gate_dot(r, st)
            if r % 2 == 0 and r + 2 < N_SUB:
                inproj(r // 2 + 1)
            lru_scan(r, st)
            next_ffn(r)
            if not last:
                next_ffn(r)
            lru_dot(r, st)
            if last:
                next_ffn(r + 1)
        while n_prepped[0] < N_SUB:
            ffn_prep(n_prepped[0])
            n_prepped[0] += 1

        kt_ext[:, 0:WINDOW] = kt_ext[:, T:T + WINDOW]
        v_ext[0:WINDOW, :] = v_ext[T:T + WINDOW, :]
        vr_ext[0:WINDOW, :] = vr_ext[T:T + WINDOW, :]
        u_ext[0:CONV_HALO, :] = u_ext[T:T + CONV_HALO, :]
        xl_ext[0:LRU_HALO, :] = xl_ext[T:T + LRU_HALO, :]

    @pl.when(i == 0)
    def _():
        fhn_s[...] = jnp.zeros((T, D_MODEL), BF16)
        fres_s[...] = jnp.zeros((T, D_MODEL), F32)

    emit(True, True)


def _layer_spec(arr, layer):
    nd = arr.ndim
    return pl.BlockSpec((None,) + arr.shape[1:], lambda i: (layer,) + (0,) * (nd - 1),
                        pipeline_mode=pl.Buffered(1))


def _layer_call(x, sinks, consts, seq_len, layer, final):
    M, D = x.shape
    T = MIX_T
    nblk = M // T
    return pl.pallas_call(
        functools.partial(_layer_kernel, layer=layer, final=final, blocks_per_seq=seq_len // T, nblk=nblk),
        out_shape=jax.ShapeDtypeStruct((M, D), F32),
        grid=(nblk + 1,),
        in_specs=[pl.BlockSpec(memory_space=pltpu.SMEM),
                  pl.BlockSpec((T, D), lambda i: (jnp.minimum(i, nblk - 1), 0))]
                 + [_layer_spec(c, layer) for c in consts],
        out_specs=pl.BlockSpec((T, D), lambda i: (jnp.maximum(i - 1, 0), 0)),
        scratch_shapes=[
            pltpu.VMEM((KV_WIDTH, WINDOW + T), BF16),
            pltpu.VMEM((WINDOW + T, KV_WIDTH), BF16),
            pltpu.VMEM((WINDOW + T, KV_WIDTH), BF16),
            pltpu.VMEM((CONV_HALO + T, CONV_WIDTH), F32),
            pltpu.VMEM((LRU_HALO + T, LRU_WIDTH), F32),
            pltpu.VMEM((SUBLANES, LRU_WIDTH), F32),
            pltpu.VMEM((2, WINDOW, 2 * WINDOW), F32),
            pltpu.VMEM((LRU_WIDTH // LANES, SUBLANES * SCAN_PITCH, LANES), F32),
            pltpu.VMEM((LRU_WIDTH // LANES, SUBLANES * SCAN_PITCH, LANES), F32),
            pltpu.VMEM((T, IN_WIDTH), F32),
            pltpu.VMEM((T, D), F32),
            pltpu.VMEM((T, D), BF16),
            pltpu.VMEM((T, D), F32),
            pltpu.VMEM((2, T, FF_CHUNK), BF16),
        ],
        compiler_params=pltpu.CompilerParams(
            dimension_semantics=("arbitrary",), vmem_limit_bytes=VMEM_LIMIT),
        name="layer_final" if final else "layer",
    )(sinks, x, *consts)


def _block_diag(w):
    dp, hN, di, dj = w.shape
    eye = jnp.eye(hN, dtype=w.dtype)
    return (eye[None, :, None, :, None] * w[:, :, :, None, :]).reshape(dp, hN * di, hN * dj)


def kernel(x, norm1, w_in, attn_sinks, conv_dw_w, conv_dw_b, conv_ln_g, conv_ln_b, lru_conv_w, lru_conv_b,
           lru_wa, lru_ba, lru_wx, lru_bx, lru_lambda, mix_norm, w_out, norm2, w_up, w_down, final_norm):
    B, S, D = x.shape
    depth = w_in.shape[0]
    qscale = jnp.concatenate([jnp.full((ATTN_WIDTH,), HEAD_DIM ** -0.5, F32),
                              jnp.ones((IN_WIDTH - ATTN_WIDTH,), F32)])
    win = (w_in * qscale).astype(BF16)
    wg = jnp.concatenate([_block_diag(lru_wa), _block_diag(lru_wx)], axis=2).astype(BF16)
    vec = jnp.concatenate(
        [norm1, conv_dw_b, conv_ln_g, conv_ln_b, lru_conv_b, lru_ba.reshape(depth, -1), lru_bx.reshape(depth, -1),
         lru_lambda, mix_norm, norm2, jnp.broadcast_to(final_norm[None], (depth, D))], axis=1)[:, None, :]
    consts = (vec, win, conv_dw_w, lru_conv_w, wg, w_out.astype(BF16), w_up.astype(BF16), w_down.astype(BF16))
    h = x.reshape(B * S, D)
    for l in range(depth):
        h = _layer_call(h, attn_sinks, consts, S, layer=l, final=(l == depth - 1))
    return h.reshape(B, S, D)
```

```python
import functools

import jax
import jax.numpy as jnp
from jax import lax
from jax.experimental import pallas as pl
from jax.experimental.pallas import tpu as pltpu

D_MODEL = 1024
HEAD_DIM = 64
ATTN_WIDTH = 512
N_Q_HEADS = 8
N_KV_HEADS = 2
KV_WIDTH = 128
WINDOW = 128
CONV_WIDTH = 256
CONV_KERNEL = 31
LRU_WIDTH = 256
LRU_HEADS = 4
LRU_HEAD_DIM = 64
LRU_CONV_KERNEL = 4
LRU_C = 8.0
MIX_WIDTH = 1024
IN_WIDTH = 1792
D_FF = 4096
RMS_EPS = 1e-6
LN_EPS = 1e-5
MASK_VALUE = -1e30

Q_OFF, K_OFF, V_OFF, CV_OFF, CG_OFF, RX_OFF, RG_OFF = 0, 512, 640, 768, 1024, 1280, 1536
YA_OFF, YC_OFF, YL_OFF = 0, ATTN_WIDTH, ATTN_WIDTH + CONV_WIDTH

VEC_NAMES = ("norm1", "conv_b", "ln_g", "ln_b", "lru_conv_b", "gate_b", "lambda", "mix_norm", "norm2", "final_norm")
VEC_WIDTHS = dict(zip(VEC_NAMES, (D_MODEL, CONV_WIDTH, CONV_WIDTH, CONV_WIDTH, LRU_WIDTH, 2 * LRU_WIDTH,
                                  LRU_WIDTH, MIX_WIDTH, D_MODEL, D_MODEL)))
VEC_OFF = {name: sum(VEC_WIDTHS[n] for n in VEC_NAMES[:k]) for k, name in enumerate(VEC_NAMES)}

LANES = 128
SUBLANES = 8
MXU_COLS = 256

MIX_T = 512
SUB = WINDOW
N_SUB = MIX_T // SUB
FF_CHUNK = D_FF // N_SUB
PIECE_ROWS = MIX_T // 2
PIECE_COLS = 2 * MXU_COLS
PIECES_PER_CHUNK = (MIX_T // PIECE_ROWS) * (FF_CHUNK // PIECE_COLS)
CONV_HALO = 32
LRU_HALO = 8
SCAN_LEN = SUB // SUBLANES
SCAN_PITCH = SCAN_LEN + SUBLANES
VMEM_LIMIT = 56 * 1024 * 1024

F32 = jnp.float32
BF16 = jnp.bfloat16


def _sigmoid(x):
    return 0.5 * jnp.tanh(0.5 * x) + 0.5


def _rms(x, g):
    return x * lax.rsqrt(jnp.mean(x * x, axis=-1, keepdims=True) + RMS_EPS) * g


class _Vec:
    def __init__(self, ref, name):
        self.ref, self.off, self.width = ref, VEC_OFF[name], VEC_WIDTHS[name]

    def __getitem__(self, idx):
        if idx is Ellipsis:
            return self.ref[:, self.off:self.off + self.width]
        rows, cols = idx
        return self.ref[rows, self.off + cols.start:self.off + cols.stop]


def _layer_kernel(sinks_ref, x_ref, vec_ref, win_ref, cw_ref, lcw_ref, wg_ref, wout_ref, wup_ref, wdn_ref,
                  o_ref,
                  kt_ext, v_ext, vr_ext, u_ext, xl_ext, hcar, bias_s, a_s, b_s,
                  z_s, hmid, fhn_s, fres_s, fu_s,
                  *, layer, final, blocks_per_seq):
    T = MIX_T
    i = pl.program_id(0)
    s = lax.rem(i, blocks_per_seq)
    n1_ref, cb_ref, lng_ref, lnb_ref, lcb_ref, bg_ref, lam_ref, mixg_ref, n2_ref, fn_ref = (
        _Vec(vec_ref, name) for name in VEC_NAMES)

    @pl.when(s == 0)
    def _():
        kt_ext[:, 0:WINDOW] = jnp.zeros((KV_WIDTH, WINDOW), BF16)
        v_ext[0:WINDOW, :] = jnp.zeros((WINDOW, KV_WIDTH), BF16)
        vr_ext[0:WINDOW, :] = jnp.zeros((WINDOW, KV_WIDTH), BF16)
        u_ext[0:CONV_HALO, :] = jnp.zeros((CONV_HALO, CONV_WIDTH), F32)
        xl_ext[0:LRU_HALO, :] = jnp.zeros((LRU_HALO, LRU_WIDTH), F32)
        hcar[...] = jnp.zeros((SUBLANES, LRU_WIDTH), F32)

    qi = lax.broadcasted_iota(jnp.int32, (WINDOW, 2 * WINDOW), 0)
    kc = lax.broadcasted_iota(jnp.int32, (WINDOW, 2 * WINDOW), 1)
    band = (kc > qi) & (kc <= qi + WINDOW)
    bias_s[0] = jnp.where(band, 0.0, MASK_VALUE).astype(F32)
    bias_s[1] = jnp.where(band & (kc >= WINDOW), 0.0, MASK_VALUE).astype(F32)

    mg = mixg_ref[...]
    lane = lax.broadcasted_iota(jnp.int32, (2 * WINDOW, KV_WIDTH), 1)
    band_row = lax.broadcasted_iota(jnp.int32, (2 * WINDOW, KV_WIDTH), 0)
    keep_lo = (lane < HEAD_DIM) & (band_row != 0)
    drop_hi = (lane < HEAD_DIM) | (band_row == 0)
    lane_o = lax.broadcasted_iota(jnp.int32, (WINDOW, LANES), 1)
    zk = jnp.zeros((HEAD_DIM, 2 * WINDOW), BF16)
    row8 = lax.broadcasted_iota(jnp.int32, (SUBLANES, LANES), 0)
    L, P = SCAN_LEN, SCAN_PITCH

    def ffn_prep(r):
        rows = slice(r * SUB, (r + 1) * SUB)
        h = hmid[rows, :]
        fhn_s[rows, :] = _rms(h, n2_ref[...]).astype(BF16)
        fres_s[rows, :] = h

    def piece(p):
        n_col = FF_CHUNK // PIECE_COLS
        rows = slice((p // n_col) * PIECE_ROWS, (p // n_col + 1) * PIECE_ROWS)
        cols = slice((p % n_col) * PIECE_COLS, (p % n_col + 1) * PIECE_COLS)
        return rows, cols

    def ffn_up(c, p):
        rows, cols = piece(p)
        u = jnp.dot(fhn_s[rows, :], wup_ref[:, c * FF_CHUNK + cols.start:c * FF_CHUNK + cols.stop],
                    preferred_element_type=F32).astype(BF16)
        u = jnp.maximum(u, 0.0)
        fu_s[c % 2, rows, cols] = u * u

    def ffn_down(c, p):
        rows, cols = piece(p)
        d = jnp.dot(fu_s[c % 2, rows, :], wdn_ref[c * FF_CHUNK:(c + 1) * FF_CHUNK, cols], preferred_element_type=F32)
        if c == 0:
            o_ref[rows, cols] = fres_s[rows, cols] + d
        else:
            o_ref[rows, cols] += d

    def ffn_finish(half):
        if final:
            rows = slice(half * PIECE_ROWS, (half + 1) * PIECE_ROWS)
            o_ref[rows, :] = _rms(o_ref[rows, :], fn_ref[...])

    def inproj(half):
        rows = slice(half * 2 * SUB, (half + 1) * 2 * SUB)
        hn = _rms(x_ref[rows, :], n1_ref[...]).astype(BF16)
        z_s[rows, :] = jnp.dot(hn, win_ref[...], preferred_element_type=F32)

    def attn_qk(r, st):
        r0 = r * SUB
        q = z_s[r0:r0 + SUB, Q_OFF:Q_OFF + ATTN_WIDTH].astype(BF16)
        kf = z_s[r0:r0 + SUB, K_OFF:K_OFF + KV_WIDTH]
        vf = z_s[r0:r0 + SUB, V_OFF:V_OFF + KV_WIDTH]
        kt_ext[:, WINDOW + r0:WINDOW + r0 + SUB] = kf.T.astype(BF16)
        v_ext[WINDOW + r0:WINDOW + r0 + SUB, :] = vf.astype(BF16)
        vr_ext[WINDOW + r0:WINDOW + r0 + SUB, :] = pltpu.roll(vf, HEAD_DIM, axis=1).astype(BF16)
        kt_band = kt_ext[:, r0:r0 + 2 * WINDOW]
        st["sc"] = []
        for h in range(N_KV_HEADS):
            kh = kt_band[h * HEAD_DIM:(h + 1) * HEAD_DIM, :]
            kt2 = jnp.concatenate(
                [jnp.concatenate([kh, zk], axis=0), jnp.concatenate([zk, kh], axis=0)], axis=1)
            c0 = h * 2 * LANES
            qs = jnp.concatenate([q[:, c0:c0 + LANES], q[:, c0 + LANES:c0 + 2 * LANES]], axis=0)
            st["sc"].append(jnp.dot(qs, kt2, preferred_element_type=F32))

    def attn_softmax(r, h, st):
        bias = bias_s[jnp.where(s == 0, 1, 0)] if r == 0 else bias_s[0]
        sc = st["sc"][h]
        p_rows = []
        dens = []
        for rr in range(2):
            p_cols = []
            den_r = []
            for e in range(2):
                sink = sinks_ref[layer, 4 * h + 2 * rr + e]
                rws = slice(rr * WINDOW, (rr + 1) * WINDOW)
                c0 = e * 2 * WINDOW
                t = jnp.concatenate(
                    [jnp.where(lane_o == 0, sink, sc[rws, c0:c0 + LANES] + bias[:, :LANES]),
                     sc[rws, c0 + LANES:c0 + 2 * LANES] + bias[:, LANES:]], axis=1)
                p = jnp.exp(t - jnp.max(t, axis=-1, keepdims=True))
                den_r.append(jnp.sum(p, axis=-1, keepdims=True))
                p_cols.append(p.astype(BF16))
            p_rows.append(jnp.concatenate(p_cols, axis=1))
            dens.append(den_r)
        st[("pm", h)] = jnp.concatenate(p_rows, axis=0)
        st[("den", h)] = dens

    def attn_pv(r, h, st):
        r0 = r * SUB
        v_band = v_ext[r0:r0 + 2 * WINDOW, :]
        vr_band = vr_ext[r0:r0 + 2 * WINDOW, :]
        if h == 0:
            va = jnp.where(keep_lo, v_band, jnp.zeros_like(v_band))
            vb = jnp.where(drop_hi, jnp.zeros_like(vr_band), vr_band)
        else:
            va = jnp.where(keep_lo, vr_band, jnp.zeros_like(vr_band))
            vb = jnp.where(drop_hi, jnp.zeros_like(v_band), v_band)
        v2 = jnp.concatenate([va, vb], axis=0)
        o = jnp.dot(st[("pm", h)], v2, preferred_element_type=F32)
        dens = st[("den", h)]
        for rr in range(2):
            den = jnp.where(lane_o < HEAD_DIM, dens[rr][0], dens[rr][1])
            st[("ya", 2 * h + rr)] = o[rr * WINDOW:(rr + 1) * WINDOW, :] / den

    def attn_norm(r, st):
        y_attn = jnp.concatenate([st[("ya", c)] for c in range(ATTN_WIDTH // LANES)], axis=1)
        st["ya_n"] = _rms(y_attn, mg[:, YA_OFF:YA_OFF + ATTN_WIDTH]).astype(BF16)

    def attn_dot(r, st):
        rows = slice(r * SUB, (r + 1) * SUB)
        hmid[rows, :] = x_ref[rows, :] + jnp.dot(st["ya_n"], wout_ref[YA_OFF:YA_OFF + ATTN_WIDTH, :],
                                                 preferred_element_type=F32)

    def conv_unit(r, lt, st):
        r0 = r * SUB
        cl = slice(lt * LANES, (lt + 1) * LANES)
        cval = z_s[r0:r0 + SUB, CV_OFF + lt * LANES:CV_OFF + (lt + 1) * LANES]
        cgate = z_s[r0:r0 + SUB, CG_OFF + lt * LANES:CG_OFF + (lt + 1) * LANES]
        u_ext[CONV_HALO + r0:CONV_HALO + r0 + SUB, cl] = cval * _sigmoid(cgate)
        base = CONV_HALO - (CONV_KERNEL - 1)
        acc = jnp.broadcast_to(cb_ref[:, cl], (SUB, LANES))
        for sh in range(SUBLANES):
            part = None
            nrows = SUB + (SUBLANES if sh else 0)
            for k in range(CONV_KERNEL):
                if (base + k) % SUBLANES != sh:
                    continue
                al = r0 + base + k - sh
                term = cw_ref[k:k + 1, cl] * u_ext[al:al + nrows, cl]
                part = term if part is None else part + term
            acc = acc + part[sh:sh + SUB, :]
        st[("yc", lt)] = acc

    def conv_post(r, st):
        uc = jnp.concatenate([st[("yc", lt)] for lt in range(CONV_WIDTH // LANES)], axis=1)
        mu = jnp.mean(uc, axis=-1, keepdims=True)
        xc_ = uc - mu
        ln = xc_ * lax.rsqrt(jnp.mean(xc_ * xc_, axis=-1, keepdims=True) + LN_EPS) * lng_ref[...] + lnb_ref[...]
        y_conv = ln * _sigmoid(ln)
        st["yc_n"] = _rms(y_conv, mg[:, YC_OFF:YC_OFF + CONV_WIDTH]).astype(BF16)

    def conv_dot(r, st):
        hmid[r * SUB:(r + 1) * SUB, :] += jnp.dot(st["yc_n"], wout_ref[YC_OFF:YC_OFF + CONV_WIDTH, :],
                                                  preferred_element_type=F32)

    def lru_pre(r, st):
        r0 = r * SUB
        xl_ext[LRU_HALO + r0:LRU_HALO + r0 + SUB, :] = z_s[r0:r0 + SUB, RX_OFF:RX_OFF + LRU_WIDTH]
        lbase = LRU_HALO - (LRU_CONV_KERNEL - 1)
        xc = jnp.broadcast_to(lcb_ref[...], (SUB, LRU_WIDTH))
        for k in range(LRU_CONV_KERNEL):
            xc = xc + lcw_ref[k:k + 1, :] * xl_ext[r0 + lbase + k:r0 + lbase + k + SUB, :]
        st["xc"] = xc
        st["xc_b"] = xc.astype(BF16)

    def lru_gate_dot(r, st):
        st["gates"] = jnp.dot(st["xc_b"], wg_ref[...], preferred_element_type=F32) + bg_ref[...]

    def lru_scan(r, st):
        r0 = r * SUB
        xc, gates = st["xc"], st["gates"]
        rgate = _sigmoid(gates[:, :LRU_WIDTH])
        igate = _sigmoid(gates[:, LRU_WIDTH:])
        log_a = (-LRU_C * rgate) * jax.nn.softplus(-lam_ref[...])
        a_full = jnp.exp(log_a)
        th = jnp.tanh(log_a)
        b_full = jnp.sqrt(2.0 * th / (th - 1.0)) * (igate * xc)
        yl_cols = []
        for lt in range(LRU_WIDTH // LANES):
            cl = slice(lt * LANES, (lt + 1) * LANES)
            for c in range(SUBLANES):
                a_s[lt, c * P:c * P + L, :] = a_full[c * L:(c + 1) * L, cl]
                b_s[lt, c * P:c * P + L, :] = b_full[c * L:(c + 1) * L, cl]
            hloc = jnp.zeros((SUBLANES, LANES), F32)
            cum = jnp.ones((SUBLANES, LANES), F32)
            for m in range(L):
                am = a_s[lt, pl.ds(m, SUBLANES, stride=P), :]
                bm = b_s[lt, pl.ds(m, SUBLANES, stride=P), :]
                hloc = am * hloc + bm
                cum = am * cum
                b_s[lt, pl.ds(m, SUBLANES, stride=P), :] = hloc
                a_s[lt, pl.ds(m, SUBLANES, stride=P), :] = cum
            ca, cbv = cum, hloc
            for d in (1, 2, 4):
                a_sh = jnp.where(row8 >= d, pltpu.roll(ca, d, axis=0), 1.0)
                b_sh = jnp.where(row8 >= d, pltpu.roll(cbv, d, axis=0), 0.0)
                cbv = ca * b_sh + cbv
                ca = ca * a_sh
            hprev = hcar[:, cl]
            ends = ca * hprev + cbv
            carry_in = jnp.where(row8 == 0, hprev, pltpu.roll(ends, 1, axis=0))
            hcar[:, cl] = jnp.broadcast_to(ends[SUBLANES - 1:SUBLANES, :], (SUBLANES, LANES))
            parts = []
            for c in range(SUBLANES):
                g = jnp.broadcast_to(carry_in[c:c + 1, :], (L, LANES))
                parts.append(b_s[lt, c * P:c * P + L, :] + a_s[lt, c * P:c * P + L, :] * g)
            yl_cols.append(jnp.concatenate(parts, axis=0))
        y_lru = jnp.concatenate(yl_cols, axis=1) * jax.nn.gelu(z_s[r0:r0 + SUB, RG_OFF:RG_OFF + LRU_WIDTH])
        st["yl_n"] = _rms(y_lru, mg[:, YL_OFF:YL_OFF + LRU_WIDTH]).astype(BF16)

    def lru_dot(r, st):
        hmid[r * SUB:(r + 1) * SUB, :] += jnp.dot(st["yl_n"], wout_ref[YL_OFF:YL_OFF + LRU_WIDTH, :],
                                                  preferred_element_type=F32)

    def emit():
        pieces = []
        for c in range(N_SUB):
            pieces += [functools.partial(ffn_up, c, p) for p in range(PIECES_PER_CHUNK)]
            pieces += [functools.partial(ffn_down, c, p) for p in range(PIECES_PER_CHUNK)]
        n_pieces = len(pieces)
        last_up = n_pieces - PIECES_PER_CHUNK - 1
        n_emitted = [0]
        n_prepped = [0]

        def next_ffn(mixer_rows_done):
            k = n_emitted[0]
            pieces[k]()
            n_emitted[0] += 1
            if k >= last_up:
                while n_prepped[0] < mixer_rows_done:
                    ffn_prep(n_prepped[0])
                    n_prepped[0] += 1
            if k == n_pieces - PIECES_PER_CHUNK // 2 - 1:
                ffn_finish(0)
            if k == n_pieces - 1:
                ffn_finish(1)

        next_ffn(0)
        inproj(0)
        for r in range(N_SUB):
            st = {}
            last = r == N_SUB - 1
            attn_qk(r, st)
            if r > 0:
                next_ffn(r)
            attn_softmax(r, 0, st)
            attn_softmax(r, 1, st)
            next_ffn(r)
            next_ffn(r)
            attn_pv(r, 0, st)
            attn_pv(r, 1, st)
            attn_norm(r, st)
            conv_unit(r, 0, st)
            next_ffn(r)
            attn_dot(r, st)
            conv_unit(r, 1, st)
            next_ffn(r)
            conv_post(r, st)
            lru_pre(r, st)
            next_ffn(r)
            conv_dot(r, st)
            lru_gate_dot(r, st)
            if r % 2 == 0 and r + 2 < N_SUB:
                inproj(r // 2 + 1)
            lru_scan(r, st)
            next_ffn(r)
            if not last:
                next_ffn(r)
            lru_dot(r, st)
            if last:
                next_ffn(r + 1)
        while n_prepped[0] < N_SUB:
            ffn_prep(n_prepped[0])
            n_prepped[0] += 1

        kt_ext[:, 0:WINDOW] = kt_ext[:, T:T + WINDOW]
        v_ext[0:WINDOW, :] = v_ext[T:T + WINDOW, :]
        vr_ext[0:WINDOW, :] = vr_ext[T:T + WINDOW, :]
        u_ext[0:CONV_HALO, :] = u_ext[T:T + CONV_HALO, :]
        xl_ext[0:LRU_HALO, :] = xl_ext[T:T + LRU_HALO, :]

    @pl.when(i == 0)
    def _():
        fhn_s[...] = jnp.zeros((T, D_MODEL), BF16)
        fres_s[...] = jnp.zeros((T, D_MODEL), F32)

    emit()


def _layer_spec(arr, layer):
    nd = arr.ndim
    return pl.BlockSpec((None,) + arr.shape[1:], lambda i: (layer,) + (0,) * (nd - 1),
                        pipeline_mode=pl.Buffered(1))


def _layer_call(x, sinks, consts, seq_len, layer, final):
    M, D = x.shape
    T = MIX_T
    nblk = M // T
    return pl.pallas_call(
        functools.partial(_layer_kernel, layer=layer, final=final, blocks_per_seq=seq_len // T),
        out_shape=jax.ShapeDtypeStruct((M, D), F32),
        grid=(nblk + 1,),
        in_specs=[pl.BlockSpec(memory_space=pltpu.SMEM),
                  pl.BlockSpec((T, D), lambda i: (jnp.minimum(i, nblk - 1), 0))]
                 + [_layer_spec(c, layer) for c in consts],
        out_specs=pl.BlockSpec((T, D), lambda i: (jnp.maximum(i - 1, 0), 0)),
        scratch_shapes=[
            pltpu.VMEM((KV_WIDTH, WINDOW + T), BF16),
            pltpu.VMEM((WINDOW + T, KV_WIDTH), BF16),
            pltpu.VMEM((WINDOW + T, KV_WIDTH), BF16),
            pltpu.VMEM((CONV_HALO + T, CONV_WIDTH), F32),
            pltpu.VMEM((LRU_HALO + T, LRU_WIDTH), F32),
            pltpu.VMEM((SUBLANES, LRU_WIDTH), F32),
            pltpu.VMEM((2, WINDOW, 2 * WINDOW), F32),
            pltpu.VMEM((LRU_WIDTH // LANES, SUBLANES * SCAN_PITCH, LANES), F32),
            pltpu.VMEM((LRU_WIDTH // LANES, SUBLANES * SCAN_PITCH, LANES), F32),
            pltpu.VMEM((T, IN_WIDTH), F32),
            pltpu.VMEM((T, D), F32),
            pltpu.VMEM((T, D), BF16),
            pltpu.VMEM((T, D), F32),
            pltpu.VMEM((2, T, FF_CHUNK), BF16),
        ],
        compiler_params=pltpu.CompilerParams(
            dimension_semantics=("arbitrary",), vmem_limit_bytes=VMEM_LIMIT),
        name="layer_final" if final else "layer",
    )(sinks, x, *consts)


def _block_diag(w):
    dp, hN, di, dj = w.shape
    eye = jnp.eye(hN, dtype=w.dtype)
    return (eye[None, :, None, :, None] * w[:, :, :, None, :]).reshape(dp, hN * di, hN * dj)


def kernel(x, norm1, w_in, attn_sinks, conv_dw_w, conv_dw_b, conv_ln_g, conv_ln_b, lru_conv_w, lru_conv_b,
           lru_wa, lru_ba, lru_wx, lru_bx, lru_lambda, mix_norm, w_out, norm2, w_up, w_down, final_norm):
    B, S, D = x.shape
    depth = w_in.shape[0]
    qscale = jnp.concatenate([jnp.full((ATTN_WIDTH,), HEAD_DIM ** -0.5, F32),
                              jnp.ones((IN_WIDTH - ATTN_WIDTH,), F32)])
    win = (w_in * qscale).astype(BF16)
    wg = jnp.concatenate([_block_diag(lru_wa), _block_diag(lru_wx)], axis=2).astype(BF16)
    vec = jnp.concatenate(
        [norm1, conv_dw_b, conv_ln_g, conv_ln_b, lru_conv_b, lru_ba.reshape(depth, -1), lru_bx.reshape(depth, -1),
         lru_lambda, mix_norm, norm2, jnp.broadcast_to(final_norm[None], (depth, D))], axis=1)[:, None, :]
    consts = (vec, win, conv_dw_w, lru_conv_w, wg, w_out.astype(BF16), w_up.astype(BF16), w_down.astype(BF16))
    h = x.reshape(B * S, D)
    for l in range(depth):
        h = _layer_call(h, attn_sinks, consts, S, layer=l, final=(l == depth - 1))
    return h.reshape(B, S, D)
```

```python
import functools

import jax
import jax.numpy as jnp
from jax import lax
from jax.experimental import pallas as pl
from jax.experimental.pallas import tpu as pltpu

D_MODEL = 1024
HEAD_DIM = 64
ATTN_WIDTH = 512
N_Q_HEADS = 8
N_KV_HEADS = 2
KV_WIDTH = 128
WINDOW = 128
CONV_WIDTH = 256
CONV_KERNEL = 31
LRU_WIDTH = 256
LRU_HEADS = 4
LRU_HEAD_DIM = 64
LRU_CONV_KERNEL = 4
LRU_C = 8.0
MIX_WIDTH = 1024
IN_WIDTH = 1792
D_FF = 4096
RMS_EPS = 1e-6
LN_EPS = 1e-5
MASK_VALUE = -1e30

Q_OFF, K_OFF, V_OFF, CV_OFF, CG_OFF, RX_OFF, RG_OFF = 0, 512, 640, 768, 1024, 1280, 1536
YA_OFF, YC_OFF, YL_OFF = 0, ATTN_WIDTH, ATTN_WIDTH + CONV_WIDTH

VEC_NAMES = ("norm1", "conv_b", "ln_g", "ln_b", "lru_conv_b", "gate_b", "lambda", "mix_norm", "norm2", "final_norm")
VEC_WIDTHS = dict(zip(VEC_NAMES, (D_MODEL, CONV_WIDTH, CONV_WIDTH, CONV_WIDTH, LRU_WIDTH, 2 * LRU_WIDTH,
                                  LRU_WIDTH, MIX_WIDTH, D_MODEL, D_MODEL)))
VEC_OFF = {name: sum(VEC_WIDTHS[n] for n in VEC_NAMES[:k]) for k, name in enumerate(VEC_NAMES)}

LANES = 128
SUBLANES = 8
MXU_COLS = 256

MIX_T = 512
SUB = WINDOW
N_SUB = MIX_T // SUB
FF_CHUNK = D_FF // N_SUB
PIECE_ROWS = MIX_T // 2
PIECE_COLS = 2 * MXU_COLS
PIECES_PER_CHUNK = (MIX_T // PIECE_ROWS) * (FF_CHUNK // PIECE_COLS)
CONV_HALO = 32
LRU_HALO = 8
SCAN_LEN = SUB // SUBLANES
SCAN_PITCH = SCAN_LEN + SUBLANES
VMEM_LIMIT = 56 * 1024 * 1024

F32 = jnp.float32
BF16 = jnp.bfloat16


def _sigmoid(x):
    return 0.5 * jnp.tanh(0.5 * x) + 0.5


def _rms(x, g):
    return x * lax.rsqrt(jnp.mean(x * x, axis=-1, keepdims=True) + RMS_EPS) * g


class _Vec:
    def __init__(self, ref, name):
        self.ref, self.off, self.width = ref, VEC_OFF[name], VEC_WIDTHS[name]

    def __getitem__(self, idx):
        if idx is Ellipsis:
            return self.ref[:, self.off:self.off + self.width]
        rows, cols = idx
        return self.ref[rows, self.off + cols.start:self.off + cols.stop]


def _layer_kernel(sinks_ref, x_ref, vec_ref, win_ref, cw_ref, lcw_ref, wg_ref, wout_ref, wup_ref, wdn_ref,
                  o_ref,
                  kt_ext, v_ext, vr_ext, u_ext, xl_ext, hcar, bias_s, a_s, b_s,
                  z_s, hmid, fhn_s, fres_s, fu_s,
                  *, layer, final, blocks_per_seq):
    T = MIX_T
    i = pl.program_id(0)
    s = lax.rem(i, blocks_per_seq)
    n1_ref, cb_ref, lng_ref, lnb_ref, lcb_ref, bg_ref, lam_ref, mixg_ref, n2_ref, fn_ref = (
        _Vec(vec_ref, name) for name in VEC_NAMES)

    @pl.when(s == 0)
    def _():
        kt_ext[:, 0:WINDOW] = jnp.zeros((KV_WIDTH, WINDOW), BF16)
        v_ext[0:WINDOW, :] = jnp.zeros((WINDOW, KV_WIDTH), BF16)
        vr_ext[0:WINDOW, :] = jnp.zeros((WINDOW, KV_WIDTH), BF16)
        u_ext[0:CONV_HALO, :] = jnp.zeros((CONV_HALO, CONV_WIDTH), F32)
        xl_ext[0:LRU_HALO, :] = jnp.zeros((LRU_HALO, LRU_WIDTH), F32)
        hcar[...] = jnp.zeros((SUBLANES, LRU_WIDTH), F32)

    qi = lax.broadcasted_iota(jnp.int32, (WINDOW, 2 * WINDOW), 0)
    kc = lax.broadcasted_iota(jnp.int32, (WINDOW, 2 * WINDOW), 1)
    band = (kc > qi) & (kc <= qi + WINDOW)
    bias_s[0] = jnp.where(band, 0.0, MASK_VALUE).astype(F32)
    bias_s[1] = jnp.where(band & (kc >= WINDOW), 0.0, MASK_VALUE).astype(F32)

    mg = mixg_ref[...]
    lane = lax.broadcasted_iota(jnp.int32, (2 * WINDOW, KV_WIDTH), 1)
    band_row = lax.broadcasted_iota(jnp.int32, (2 * WINDOW, KV_WIDTH), 0)
    keep_lo = (lane < HEAD_DIM) & (band_row != 0)
    drop_hi = (lane < HEAD_DIM) | (band_row == 0)
    lane_o = lax.broadcasted_iota(jnp.int32, (WINDOW, LANES), 1)
    zk = jnp.zeros((HEAD_DIM, 2 * WINDOW), BF16)
    row8 = lax.broadcasted_iota(jnp.int32, (SUBLANES, LANES), 0)
    L, P = SCAN_LEN, SCAN_PITCH

    def ffn_prep(r):
        rows = slice(r * SUB, (r + 1) * SUB)
        h = hmid[rows, :]
        fhn_s[rows, :] = _rms(h, n2_ref[...]).astype(BF16)
        fres_s[rows, :] = h

    def piece(p):
        n_col = FF_CHUNK // PIECE_COLS
        rows = slice((p // n_col) * PIECE_ROWS, (p // n_col + 1) * PIECE_ROWS)
        cols = slice((p % n_col) * PIECE_COLS, (p % n_col + 1) * PIECE_COLS)
        return rows, cols

    def ffn_up(c, p):
        rows, cols = piece(p)
        u = jnp.dot(fhn_s[rows, :], wup_ref[:, c * FF_CHUNK + cols.start:c * FF_CHUNK + cols.stop],
                    preferred_element_type=F32).astype(BF16)
        u = jnp.maximum(u, 0.0)
        fu_s[c % 2, rows, cols] = u * u

    def ffn_down(c, p):
        rows, cols = piece(p)
        d = jnp.dot(fu_s[c % 2, rows, :], wdn_ref[c * FF_CHUNK:(c + 1) * FF_CHUNK, cols], preferred_element_type=F32)
        if c == 0:
            o_ref[rows, cols] = fres_s[rows, cols] + d
        else:
            o_ref[rows, cols] += d

    def ffn_finish(half):
        if final:
            rows = slice(half * PIECE_ROWS, (half + 1) * PIECE_ROWS)
            o_ref[rows, :] = _rms(o_ref[rows, :], fn_ref[...])

    def inproj(half):
        rows = slice(half * 2 * SUB, (half + 1) * 2 * SUB)
        hn = _rms(x_ref[rows, :], n1_ref[...]).astype(BF16)
        z_s[rows, :] = jnp.dot(hn, win_ref[...], preferred_element_type=F32)

    def attn_qk(r, st):
        r0 = r * SUB
        q = z_s[r0:r0 + SUB, Q_OFF:Q_OFF + ATTN_WIDTH].astype(BF16)
        kf = z_s[r0:r0 + SUB, K_OFF:K_OFF + KV_WIDTH]
        vf = z_s[r0:r0 + SUB, V_OFF:V_OFF + KV_WIDTH]
        kt_ext[:, WINDOW + r0:WINDOW + r0 + SUB] = kf.T.astype(BF16)
        v_ext[WINDOW + r0:WINDOW + r0 + SUB, :] = vf.astype(BF16)
        vr_ext[WINDOW + r0:WINDOW + r0 + SUB, :] = pltpu.roll(vf, HEAD_DIM, axis=1).astype(BF16)
        kt_band = kt_ext[:, r0:r0 + 2 * WINDOW]
        st["sc"] = []
        for h in range(N_KV_HEADS):
            kh = kt_band[h * HEAD_DIM:(h + 1) * HEAD_DIM, :]
            kt2 = jnp.concatenate(
                [jnp.concatenate([kh, zk], axis=0), jnp.concatenate([zk, kh], axis=0)], axis=1)
            c0 = h * 2 * LANES
            qs = jnp.concatenate([q[:, c0:c0 + LANES], q[:, c0 + LANES:c0 + 2 * LANES]], axis=0)
            st["sc"].append(jnp.dot(qs, kt2, preferred_element_type=F32))

    def attn_softmax(r, h, st):
        bias = bias_s[jnp.where(s == 0, 1, 0)] if r == 0 else bias_s[0]
        sc = st["sc"][h]
        p_rows = []
        dens = []
        for rr in range(2):
            p_cols = []
            den_r = []
            for e in range(2):
                sink = sinks_ref[layer, 4 * h + 2 * rr + e]
                rws = slice(rr * WINDOW, (rr + 1) * WINDOW)
                c0 = e * 2 * WINDOW
                t = jnp.concatenate(
                    [jnp.where(lane_o == 0, sink, sc[rws, c0:c0 + LANES] + bias[:, :LANES]),
                     sc[rws, c0 + LANES:c0 + 2 * LANES] + bias[:, LANES:]], axis=1)
                p = jnp.exp(t - jnp.max(t, axis=-1, keepdims=True))
                den_r.append(jnp.sum(p, axis=-1, keepdims=True))
                p_cols.append(p.astype(BF16))
            p_rows.append(jnp.concatenate(p_cols, axis=1))
            dens.append(den_r)
        st[("pm", h)] = jnp.concatenate(p_rows, axis=0)
        st[("den", h)] = dens

    def attn_pv(r, h, st):
        r0 = r * SUB
        v_band = v_ext[r0:r0 + 2 * WINDOW, :]
        vr_band = vr_ext[r0:r0 + 2 * WINDOW, :]
        if h == 0:
            va = jnp.where(keep_lo, v_band, jnp.zeros_like(v_band))
            vb = jnp.where(drop_hi, jnp.zeros_like(vr_band), vr_band)
        else:
            va = jnp.where(keep_lo, vr_band, jnp.zeros_like(vr_band))
            vb = jnp.where(drop_hi, jnp.zeros_like(v_band), v_band)
        v2 = jnp.concatenate([va, vb], axis=0)
        o = jnp.dot(st[("pm", h)], v2, preferred_element_type=F32)
        dens = st[("den", h)]
        for rr in range(2):
            den = jnp.where(lane_o < HEAD_DIM, dens[rr][0], dens[rr][1])
            st[("ya", 2 * h + rr)] = o[rr * WINDOW:(rr + 1) * WINDOW, :] / den

    def attn_norm(r, st):
        y_attn = jnp.concatenate([st[("ya", c)] for c in range(ATTN_WIDTH // LANES)], axis=1)
        st["ya_n"] = _rms(y_attn, mg[:, YA_OFF:YA_OFF + ATTN_WIDTH]).astype(BF16)

    def attn_dot(r, st):
        rows = slice(r * SUB, (r + 1) * SUB)
        hmid[rows, :] = x_ref[rows, :] + jnp.dot(st["ya_n"], wout_ref[YA_OFF:YA_OFF + ATTN_WIDTH, :],
                                                 preferred_element_type=F32)

    def conv_unit(r, lt, st):
        r0 = r * SUB
        cl = slice(lt * LANES, (lt + 1) * LANES)
        cval = z_s[r0:r0 + SUB, CV_OFF + lt * LANES:CV_OFF + (lt + 1) * LANES]
        cgate = z_s[r0:r0 + SUB, CG_OFF + lt * LANES:CG_OFF + (lt + 1) * LANES]
        u_ext[CONV_HALO + r0:CONV_HALO + r0 + SUB, cl] = cval * _sigmoid(cgate)
        base = CONV_HALO - (CONV_KERNEL - 1)
        acc = jnp.broadcast_to(cb_ref[:, cl], (SUB, LANES))
        for sh in range(SUBLANES):
            part = None
            nrows = SUB + (SUBLANES if sh else 0)
            for k in range(CONV_KERNEL):
                if (base + k) % SUBLANES != sh:
                    continue
                al = r0 + base + k - sh
                term = cw_ref[k:k + 1, cl] * u_ext[al:al + nrows, cl]
                part = term if part is None else part + term
            acc = acc + part[sh:sh + SUB, :]
        st[("yc", lt)] = acc

    def conv_post(r, st):
        uc = jnp.concatenate([st[("yc", lt)] for lt in range(CONV_WIDTH // LANES)], axis=1)
        mu = jnp.mean(uc, axis=-1, keepdims=True)
        xc_ = uc - mu
        ln = xc_ * lax.rsqrt(jnp.mean(xc_ * xc_, axis=-1, keepdims=True) + LN_EPS) * lng_ref[...] + lnb_ref[...]
        y_conv = ln * _sigmoid(ln)
        st["yc_n"] = _rms(y_conv, mg[:, YC_OFF:YC_OFF + CONV_WIDTH]).astype(BF16)

    def conv_dot(r, st):
        hmid[r * SUB:(r + 1) * SUB, :] += jnp.dot(st["yc_n"], wout_ref[YC_OFF:YC_OFF + CONV_WIDTH, :],
                                                  preferred_element_type=F32)

    def lru_pre(r, st):
        r0 = r * SUB
        xl_ext[LRU_HALO + r0:LRU_HALO + r0 + SUB, :] = z_s[r0:r0 + SUB, RX_OFF:RX_OFF + LRU_WIDTH]
        lbase = LRU_HALO - (LRU_CONV_KERNEL - 1)
        xc = jnp.broadcast_to(lcb_ref[...], (SUB, LRU_WIDTH))
        for k in range(LRU_CONV_KERNEL):
            xc = xc + lcw_ref[k:k + 1, :] * xl_ext[r0 + lbase + k:r0 + lbase + k + SUB, :]
        st["xc"] = xc
        st["xc_b"] = xc.astype(BF16)

    def lru_gate_dot(r, st):
        st["gates"] = jnp.dot(st["xc_b"], wg_ref[...], preferred_element_type=F32) + bg_ref[...]

    def lru_scan(r, st):
        r0 = r * SUB
        xc, gates = st["xc"], st["gates"]
        rgate = _sigmoid(gates[:, :LRU_WIDTH])
        igate = _sigmoid(gates[:, LRU_WIDTH:])
        log_a = (-LRU_C * rgate) * jax.nn.softplus(-lam_ref[...])
        a_full = jnp.exp(log_a)
        th = jnp.tanh(log_a)
        b_full = jnp.sqrt(2.0 * th / (th - 1.0)) * (igate * xc)
        yl_cols = []
        for lt in range(LRU_WIDTH // LANES):
            cl = slice(lt * LANES, (lt + 1) * LANES)
            for c in range(SUBLANES):
                a_s[lt, c * P:c * P + L, :] = a_full[c * L:(c + 1) * L, cl]
                b_s[lt, c * P:c * P + L, :] = b_full[c * L:(c + 1) * L, cl]
            hloc = jnp.zeros((SUBLANES, LANES), F32)
            cum = jnp.ones((SUBLANES, LANES), F32)
            for m in range(L):
                am = a_s[lt, pl.ds(m, SUBLANES, stride=P), :]
                bm = b_s[lt, pl.ds(m, SUBLANES, stride=P), :]
                hloc = am * hloc + bm
                cum = am * cum
                b_s[lt, pl.ds(m, SUBLANES, stride=P), :] = hloc
                a_s[lt, pl.ds(m, SUBLANES, stride=P), :] = cum
            ca, cbv = cum, hloc
            for d in (1, 2, 4):
                a_sh = jnp.where(row8 >= d, pltpu.roll(ca, d, axis=0), 1.0)
                b_sh = jnp.where(row8 >= d, pltpu.roll(cbv, d, axis=0), 0.0)
                cbv = ca * b_sh + cbv
                ca = ca * a_sh
            hprev = hcar[:, cl]
            ends = ca * hprev + cbv
            carry_in = jnp.where(row8 == 0, hprev, pltpu.roll(ends, 1, axis=0))
            hcar[:, cl] = jnp.broadcast_to(ends[SUBLANES - 1:SUBLANES, :], (SUBLANES, LANES))
            parts = []
            for c in range(SUBLANES):
                g = jnp.broadcast_to(carry_in[c:c + 1, :], (L, LANES))
                parts.append(b_s[lt, c * P:c * P + L, :] + a_s[lt, c * P:c * P + L, :] * g)
            yl_cols.append(jnp.concatenate(parts, axis=0))
        y_lru = jnp.concatenate(yl_cols, axis=1) * jax.nn.gelu(z_s[r0:r0 + SUB, RG_OFF:RG_OFF + LRU_WIDTH])
        st["yl_n"] = _rms(y_lru, mg[:, YL_OFF:YL_OFF + LRU_WIDTH]).astype(BF16)

    def lru_dot(r, st):
        hmid[r * SUB:(r + 1) * SUB, :] += jnp.dot(st["yl_n"], wout_ref[YL_OFF:YL_OFF + LRU_WIDTH, :],
                                                  preferred_element_type=F32)

    def emit():
        pieces = []
        for c in range(N_SUB):
            pieces += [functools.partial(ffn_up, c, p) for p in range(PIECES_PER_CHUNK)]
            pieces += [functools.partial(ffn_down, c, p) for p in range(PIECES_PER_CHUNK)]
        n_pieces = len(pieces)
        last_up = n_pieces - PIECES_PER_CHUNK - 1
        n_emitted = [0]
        n_prepped = [0]

        def next_ffn(mixer_rows_done):
            k = n_emitted[0]
            pieces[k]()
            n_emitted[0] += 1
            if k >= last_up:
                while n_prepped[0] < mixer_rows_done:
                    ffn_prep(n_prepped[0])
                    n_prepped[0] += 1
            if k == n_pieces - PIECES_PER_CHUNK // 2 - 1:
                ffn_finish(0)
            if k == n_pieces - 1:
                ffn_finish(1)

        next_ffn(0)
        inproj(0)
        for r in range(N_SUB):
            st = {}
            last = r == N_SUB - 1
            attn_qk(r, st)
            if r > 0:
                next_ffn(r)
            attn_softmax(r, 0, st)
            attn_softmax(r, 1, st)
            next_ffn(r)
            next_ffn(r)
            attn_pv(r, 0, st)
            attn_pv(r, 1, st)
            attn_norm(r, st)
            lru_pre(r, st)
            next_ffn(r)
            attn_dot(r, st)
            lru_gate_dot(r, st)
            if r % 2 == 0 and r + 2 < N_SUB:
                inproj(r // 2 + 1)
            lru_scan(r, st)
            next_ffn(r)
            conv_unit(r, 0, st)
            next_ffn(r)
            lru_dot(r, st)
            conv_unit(r, 1, st)
            next_ffn(r)
            conv_post(r, st)
            if not last:
                next_ffn(r)
            conv_dot(r, st)
            if last:
                next_ffn(r + 1)
        while n_prepped[0] < N_SUB:
            ffn_prep(n_prepped[0])
            n_prepped[0] += 1

        kt_ext[:, 0:WINDOW] = kt_ext[:, T:T + WINDOW]
        v_ext[0:WINDOW, :] = v_ext[T:T + WINDOW, :]
        vr_ext[0:WINDOW, :] = vr_ext[T:T + WINDOW, :]
        u_ext[0:CONV_HALO, :] = u_ext[T:T + CONV_HALO, :]
        xl_ext[0:LRU_HALO, :] = xl_ext[T:T + LRU_HALO, :]

    @pl.when(i == 0)
    def _():
        fhn_s[...] = jnp.zeros((T, D_MODEL), BF16)
        fres_s[...] = jnp.zeros((T, D_MODEL), F32)

    emit()


def _layer_spec(arr, layer):
    nd = arr.ndim
    return pl.BlockSpec((None,) + arr.shape[1:], lambda i: (layer,) + (0,) * (nd - 1),
                        pipeline_mode=pl.Buffered(1))


def _layer_call(x, sinks, consts, seq_len, layer, final):
    M, D = x.shape
    T = MIX_T
    nblk = M // T
    return pl.pallas_call(
        functools.partial(_layer_kernel, layer=layer, final=final, blocks_per_seq=seq_len // T),
        out_shape=jax.ShapeDtypeStruct((M, D), F32),
        grid=(nblk + 1,),
        in_specs=[pl.BlockSpec(memory_space=pltpu.SMEM),
                  pl.BlockSpec((T, D), lambda i: (jnp.minimum(i, nblk - 1), 0))]
                 + [_layer_spec(c, layer) for c in consts],
        out_specs=pl.BlockSpec((T, D), lambda i: (jnp.maximum(i - 1, 0), 0)),
        scratch_shapes=[
            pltpu.VMEM((KV_WIDTH, WINDOW + T), BF16),
            pltpu.VMEM((WINDOW + T, KV_WIDTH), BF16),
            pltpu.VMEM((WINDOW + T, KV_WIDTH), BF16),
            pltpu.VMEM((CONV_HALO + T, CONV_WIDTH), F32),
            pltpu.VMEM((LRU_HALO + T, LRU_WIDTH), F32),
            pltpu.VMEM((SUBLANES, LRU_WIDTH), F32),
            pltpu.VMEM((2, WINDOW, 2 * WINDOW), F32),
            pltpu.VMEM((LRU_WIDTH // LANES, SUBLANES * SCAN_PITCH, LANES), F32),
            pltpu.VMEM((LRU_WIDTH // LANES, SUBLANES * SCAN_PITCH, LANES), F32),
            pltpu.VMEM((T, IN_WIDTH), F32),
            pltpu.VMEM((T, D), F32),
            pltpu.VMEM((T, D), BF16),
            pltpu.VMEM((T, D), F32),
            pltpu.VMEM((2, T, FF_CHUNK), BF16),
        ],
        compiler_params=pltpu.CompilerParams(
            dimension_semantics=("arbitrary",), vmem_limit_bytes=VMEM_LIMIT),
        name="layer_final" if final else "layer",
    )(sinks, x, *consts)


def _block_diag(w):
    dp, hN, di, dj = w.shape
    eye = jnp.eye(hN, dtype=w.dtype)
    return (eye[None, :, None, :, None] * w[:, :, :, None, :]).reshape(dp, hN * di, hN * dj)


def kernel(x, norm1, w_in, attn_sinks, conv_dw_w, conv_dw_b, conv_ln_g, conv_ln_b, lru_conv_w, lru_conv_b,
           lru_wa, lru_ba, lru_wx, lru_bx, lru_lambda, mix_norm, w_out, norm2, w_up, w_down, final_norm):
    B, S, D = x.shape
    depth = w_in.shape[0]
    qscale = jnp.concatenate([jnp.full((ATTN_WIDTH,), HEAD_DIM ** -0.5, F32),
                              jnp.ones((IN_WIDTH - ATTN_WIDTH,), F32)])
    win = (w_in * qscale).astype(BF16)
    wg = jnp.concatenate([_block_diag(lru_wa), _block_diag(lru_wx)], axis=2).astype(BF16)
    vec = jnp.concatenate(
        [norm1, conv_dw_b, conv_ln_g, conv_ln_b, lru_conv_b, lru_ba.reshape(depth, -1), lru_bx.reshape(depth, -1),
         lru_lambda, mix_norm, norm2, jnp.broadcast_to(final_norm[None], (depth, D))], axis=1)[:, None, :]
    consts = (vec, win, conv_dw_w, lru_conv_w, wg, w_out.astype(BF16), w_up.astype(BF16), w_down.astype(BF16))
    h = x.reshape(B * S, D)
    for l in range(depth):
        h = _layer_call(h, attn_sinks, consts, S, layer=l, final=(l == depth - 1))
    return h.reshape(B, S, D)
```

```python
import functools

import jax
import jax.numpy as jnp
from jax import lax
from jax.experimental import pallas as pl
from jax.experimental.pallas import tpu as pltpu

D_MODEL = 1024
HEAD_DIM = 64
ATTN_WIDTH = 512
N_Q_HEADS = 8
N_KV_HEADS = 2
KV_WIDTH = 128
WINDOW = 128
CONV_WIDTH = 256
CONV_KERNEL = 31
LRU_WIDTH = 256
LRU_HEADS = 4
LRU_HEAD_DIM = 64
LRU_CONV_KERNEL = 4
LRU_C = 8.0
MIX_WIDTH = 1024
IN_WIDTH = 1792
D_FF = 4096
RMS_EPS = 1e-6
LN_EPS = 1e-5
MASK_VALUE = -1e30

Q_OFF, K_OFF, V_OFF, CV_OFF, CG_OFF, RX_OFF, RG_OFF = 0, 512, 640, 768, 1024, 1280, 1536
YA_OFF, YC_OFF, YL_OFF = 0, ATTN_WIDTH, ATTN_WIDTH + CONV_WIDTH

VEC_NAMES = ("norm1", "conv_b", "ln_g", "ln_b", "lru_conv_b", "gate_b", "lambda", "mix_norm", "norm2", "final_norm")
VEC_WIDTHS = dict(zip(VEC_NAMES, (D_MODEL, CONV_WIDTH, CONV_WIDTH, CONV_WIDTH, LRU_WIDTH, 2 * LRU_WIDTH,
                                  LRU_WIDTH, MIX_WIDTH, D_MODEL, D_MODEL)))
VEC_OFF = {name: sum(VEC_WIDTHS[n] for n in VEC_NAMES[:k]) for k, name in enumerate(VEC_NAMES)}

LANES = 128
SUBLANES = 8
MXU_COLS = 256

MIX_T = 512
SUB = WINDOW
N_SUB = MIX_T // SUB
FF_CHUNK = D_FF // N_SUB
PIECE_ROWS = MIX_T // 2
PIECE_COLS = 2 * MXU_COLS
PIECES_PER_CHUNK = (MIX_T // PIECE_ROWS) * (FF_CHUNK // PIECE_COLS)
CONV_HALO = 32
LRU_HALO = 8
SCAN_LEN = SUB // SUBLANES
SCAN_PITCH = SCAN_LEN + SUBLANES
VMEM_LIMIT = 56 * 1024 * 1024

F32 = jnp.float32
BF16 = jnp.bfloat16


def _sigmoid(x):
    return 0.5 * jnp.tanh(0.5 * x) + 0.5


def _rms(x, g):
    return x * lax.rsqrt(jnp.mean(x * x, axis=-1, keepdims=True) + RMS_EPS) * g


class _Vec:
    def __init__(self, ref, name):
        self.ref, self.off, self.width = ref, VEC_OFF[name], VEC_WIDTHS[name]

    def __getitem__(self, idx):
        if idx is Ellipsis:
            return self.ref[:, self.off:self.off + self.width]
        rows, cols = idx
        return self.ref[rows, self.off + cols.start:self.off + cols.stop]


def _layer_kernel(sinks_ref, x_ref, vec_ref, win_ref, cw_ref, lcw_ref, wg_ref, wout_ref, wup_ref, wdn_ref,
                  o_ref,
                  kt_ext, v_ext, vr_ext, u_ext, xl_ext, hcar, bias_s, a_s, b_s,
                  z_s, hmid, fhn_s, fres_s, fu_s,
                  *, layer, final, blocks_per_seq):
    T = MIX_T
    i = pl.program_id(0)
    s = lax.rem(i, blocks_per_seq)
    n1_ref, cb_ref, lng_ref, lnb_ref, lcb_ref, bg_ref, lam_ref, mixg_ref, n2_ref, fn_ref = (
        _Vec(vec_ref, name) for name in VEC_NAMES)

    @pl.when(s == 0)
    def _():
        kt_ext[:, 0:WINDOW] = jnp.zeros((KV_WIDTH, WINDOW), BF16)
        v_ext[0:WINDOW, :] = jnp.zeros((WINDOW, KV_WIDTH), BF16)
        vr_ext[0:WINDOW, :] = jnp.zeros((WINDOW, KV_WIDTH), BF16)
        u_ext[0:CONV_HALO, :] = jnp.zeros((CONV_HALO, CONV_WIDTH), F32)
        xl_ext[0:LRU_HALO, :] = jnp.zeros((LRU_HALO, LRU_WIDTH), F32)
        hcar[...] = jnp.zeros((SUBLANES, LRU_WIDTH), F32)

    qi = lax.broadcasted_iota(jnp.int32, (WINDOW, 2 * WINDOW), 0)
    kc = lax.broadcasted_iota(jnp.int32, (WINDOW, 2 * WINDOW), 1)
    band = (kc > qi) & (kc <= qi + WINDOW)
    bias_s[0] = jnp.where(band, 0.0, MASK_VALUE).astype(F32)
    bias_s[1] = jnp.where(band & (kc >= WINDOW), 0.0, MASK_VALUE).astype(F32)

    mg = mixg_ref[...]
    lane = lax.broadcasted_iota(jnp.int32, (2 * WINDOW, KV_WIDTH), 1)
    band_row = lax.broadcasted_iota(jnp.int32, (2 * WINDOW, KV_WIDTH), 0)
    keep_lo = (lane < HEAD_DIM) & (band_row != 0)
    drop_hi = (lane < HEAD_DIM) | (band_row == 0)
    lane_o = lax.broadcasted_iota(jnp.int32, (WINDOW, LANES), 1)
    zk = jnp.zeros((HEAD_DIM, 2 * WINDOW), BF16)
    row8 = lax.broadcasted_iota(jnp.int32, (SUBLANES, LANES), 0)
    L, P = SCAN_LEN, SCAN_PITCH

    def ffn_prep(r):
        rows = slice(r * SUB, (r + 1) * SUB)
        h = hmid[rows, :]
        fhn_s[rows, :] = _rms(h, n2_ref[...]).astype(BF16)
        fres_s[rows, :] = h

    def piece(p):
        n_col = FF_CHUNK // PIECE_COLS
        rows = slice((p // n_col) * PIECE_ROWS, (p // n_col + 1) * PIECE_ROWS)
        cols = slice((p % n_col) * PIECE_COLS, (p % n_col + 1) * PIECE_COLS)
        return rows, cols

    def ffn_up(c, p):
        rows, cols = piece(p)
        u = jnp.dot(fhn_s[rows, :], wup_ref[:, c * FF_CHUNK + cols.start:c * FF_CHUNK + cols.stop],
                    preferred_element_type=F32).astype(BF16)
        u = jnp.maximum(u, 0.0)
        fu_s[c % 2, rows, cols] = u * u

    def ffn_down(c, p):
        rows, cols = piece(p)
        d = jnp.dot(fu_s[c % 2, rows, :], wdn_ref[c * FF_CHUNK:(c + 1) * FF_CHUNK, cols], preferred_element_type=F32)
        if c == 0:
            o_ref[rows, cols] = fres_s[rows, cols] + d
        else:
            o_ref[rows, cols] += d

    def ffn_finish(half):
        if final:
            rows = slice(half * PIECE_ROWS, (half + 1) * PIECE_ROWS)
            o_ref[rows, :] = _rms(o_ref[rows, :], fn_ref[...])

    def inproj(half):
        rows = slice(half * 2 * SUB, (half + 1) * 2 * SUB)
        hn = _rms(x_ref[rows, :], n1_ref[...]).astype(BF16)
        z_s[rows, :] = jnp.dot(hn, win_ref[...], preferred_element_type=F32)

    def attn_qk(r, st):
        r0 = r * SUB
        q = z_s[r0:r0 + SUB, Q_OFF:Q_OFF + ATTN_WIDTH].astype(BF16)
        kf = z_s[r0:r0 + SUB, K_OFF:K_OFF + KV_WIDTH]
        vf = z_s[r0:r0 + SUB, V_OFF:V_OFF + KV_WIDTH]
        kt_ext[:, WINDOW + r0:WINDOW + r0 + SUB] = kf.T.astype(BF16)
        v_ext[WINDOW + r0:WINDOW + r0 + SUB, :] = vf.astype(BF16)
        vr_ext[WINDOW + r0:WINDOW + r0 + SUB, :] = pltpu.roll(vf, HEAD_DIM, axis=1).astype(BF16)
        kt_band = kt_ext[:, r0:r0 + 2 * WINDOW]
        st["sc"] = []
        for h in range(N_KV_HEADS):
            kh = kt_band[h * HEAD_DIM:(h + 1) * HEAD_DIM, :]
            kt2 = jnp.concatenate(
                [jnp.concatenate([kh, zk], axis=0), jnp.concatenate([zk, kh], axis=0)], axis=1)
            c0 = h * 2 * LANES
            qs = jnp.concatenate([q[:, c0:c0 + LANES], q[:, c0 + LANES:c0 + 2 * LANES]], axis=0)
            st["sc"].append(jnp.dot(qs, kt2, preferred_element_type=F32))

    def attn_softmax(r, h, st):
        bias = bias_s[jnp.where(s == 0, 1, 0)] if r == 0 else bias_s[0]
        sc = st["sc"][h]
        p_rows = []
        dens = []
        for rr in range(2):
            p_cols = []
            den_r = []
            for e in range(2):
                sink = sinks_ref[layer, 4 * h + 2 * rr + e]
                rws = slice(rr * WINDOW, (rr + 1) * WINDOW)
                c0 = e * 2 * WINDOW
                t = jnp.concatenate(
                    [jnp.where(lane_o == 0, sink, sc[rws, c0:c0 + LANES] + bias[:, :LANES]),
                     sc[rws, c0 + LANES:c0 + 2 * LANES] + bias[:, LANES:]], axis=1)
                p = jnp.exp(t - jnp.max(t, axis=-1, keepdims=True))
                den_r.append(jnp.sum(p, axis=-1, keepdims=True))
                p_cols.append(p.astype(BF16))
            p_rows.append(jnp.concatenate(p_cols, axis=1))
            dens.append(den_r)
        st[("pm", h)] = jnp.concatenate(p_rows, axis=0)
        st[("den", h)] = dens

    def attn_pv(r, h, st):
        r0 = r * SUB
        v_band = v_ext[r0:r0 + 2 * WINDOW, :]
        vr_band = vr_ext[r0:r0 + 2 * WINDOW, :]
        if h == 0:
            va = jnp.where(keep_lo, v_band, jnp.zeros_like(v_band))
            vb = jnp.where(drop_hi, jnp.zeros_like(vr_band), vr_band)
        else:
            va = jnp.where(keep_lo, vr_band, jnp.zeros_like(vr_band))
            vb = jnp.where(drop_hi, jnp.zeros_like(v_band), v_band)
        v2 = jnp.concatenate([va, vb], axis=0)
        o = jnp.dot(st[("pm", h)], v2, preferred_element_type=F32)
        dens = st[("den", h)]
        for rr in range(2):
            den = jnp.where(lane_o < HEAD_DIM, dens[rr][0], dens[rr][1])
            st[("ya", 2 * h + rr)] = o[rr * WINDOW:(rr + 1) * WINDOW, :] / den

    def attn_norm(r, st):
        y_attn = jnp.concatenate([st[("ya", c)] for c in range(ATTN_WIDTH // LANES)], axis=1)
        st["ya_n"] = _rms(y_attn, mg[:, YA_OFF:YA_OFF + ATTN_WIDTH]).astype(BF16)

    def attn_dot(r, st):
        rows = slice(r * SUB, (r + 1) * SUB)
        hmid[rows, :] = x_ref[rows, :] + jnp.dot(st["ya_n"], wout_ref[YA_OFF:YA_OFF + ATTN_WIDTH, :],
                                                 preferred_element_type=F32)

    def conv_unit(r, lt, st):
        r0 = r * SUB
        cl = slice(lt * LANES, (lt + 1) * LANES)
        cval = z_s[r0:r0 + SUB, CV_OFF + lt * LANES:CV_OFF + (lt + 1) * LANES]
        cgate = z_s[r0:r0 + SUB, CG_OFF + lt * LANES:CG_OFF + (lt + 1) * LANES]
        u_ext[CONV_HALO + r0:CONV_HALO + r0 + SUB, cl] = cval * _sigmoid(cgate)
        base = CONV_HALO - (CONV_KERNEL - 1)
        acc = jnp.broadcast_to(cb_ref[:, cl], (SUB, LANES))
        for sh in range(SUBLANES):
            part = None
            nrows = SUB + (SUBLANES if sh else 0)
            for k in range(CONV_KERNEL):
                if (base + k) % SUBLANES != sh:
                    continue
                al = r0 + base + k - sh
                term = cw_ref[k:k + 1, cl] * u_ext[al:al + nrows, cl]
                part = term if part is None else part + term
            acc = acc + part[sh:sh + SUB, :]
        st[("yc", lt)] = acc

    def conv_post(r, st):
        uc = jnp.concatenate([st[("yc", lt)] for lt in range(CONV_WIDTH // LANES)], axis=1)
        mu = jnp.mean(uc, axis=-1, keepdims=True)
        xc_ = uc - mu
        ln = xc_ * lax.rsqrt(jnp.mean(xc_ * xc_, axis=-1, keepdims=True) + LN_EPS) * lng_ref[...] + lnb_ref[...]
        y_conv = ln * _sigmoid(ln)
        st["yc_n"] = _rms(y_conv, mg[:, YC_OFF:YC_OFF + CONV_WIDTH]).astype(BF16)

    def conv_dot(r, st):
        hmid[r * SUB:(r + 1) * SUB, :] += jnp.dot(st["yc_n"], wout_ref[YC_OFF:YC_OFF + CONV_WIDTH, :],
                                                  preferred_element_type=F32)

    def lru_pre(r, st):
        r0 = r * SUB
        xl_ext[LRU_HALO + r0:LRU_HALO + r0 + SUB, :] = z_s[r0:r0 + SUB, RX_OFF:RX_OFF + LRU_WIDTH]
        lbase = LRU_HALO - (LRU_CONV_KERNEL - 1)
        xc = jnp.broadcast_to(lcb_ref[...], (SUB, LRU_WIDTH))
        for k in range(LRU_CONV_KERNEL):
            xc = xc + lcw_ref[k:k + 1, :] * xl_ext[r0 + lbase + k:r0 + lbase + k + SUB, :]
        st["xc"] = xc
        st["xc_b"] = xc.astype(BF16)

    def lru_gate_dot(r, st):
        st["gates"] = jnp.dot(st["xc_b"], wg_ref[...], preferred_element_type=F32) + bg_ref[...]

    def lru_scan(r, st):
        r0 = r * SUB
        xc, gates = st["xc"], st["gates"]
        rgate = _sigmoid(gates[:, :LRU_WIDTH])
        igate = _sigmoid(gates[:, LRU_WIDTH:])
        log_a = (-LRU_C * rgate) * jax.nn.softplus(-lam_ref[...])
        a_full = jnp.exp(log_a)
        th = jnp.tanh(log_a)
        b_full = jnp.sqrt(2.0 * th / (th - 1.0)) * (igate * xc)
        yl_cols = []
        for lt in range(LRU_WIDTH // LANES):
            cl = slice(lt * LANES, (lt + 1) * LANES)
            for c in range(SUBLANES):
                a_s[lt, c * P:c * P + L, :] = a_full[c * L:(c + 1) * L, cl]
                b_s[lt, c * P:c * P + L, :] = b_full[c * L:(c + 1) * L, cl]
            hloc = jnp.zeros((SUBLANES, LANES), F32)
            cum = jnp.ones((SUBLANES, LANES), F32)
            for m in range(L):
                am = a_s[lt, pl.ds(m, SUBLANES, stride=P), :]
                bm = b_s[lt, pl.ds(m, SUBLANES, stride=P), :]
                hloc = am * hloc + bm
                cum = am * cum
                b_s[lt, pl.ds(m, SUBLANES, stride=P), :] = hloc
                a_s[lt, pl.ds(m, SUBLANES, stride=P), :] = cum
            ca, cbv = cum, hloc
            for d in (1, 2, 4):
                a_sh = jnp.where(row8 >= d, pltpu.roll(ca, d, axis=0), 1.0)
                b_sh = jnp.where(row8 >= d, pltpu.roll(cbv, d, axis=0), 0.0)
                cbv = ca * b_sh + cbv
                ca = ca * a_sh
            hprev = hcar[:, cl]
            ends = ca * hprev + cbv
            carry_in = jnp.where(row8 == 0, hprev, pltpu.roll(ends, 1, axis=0))
            hcar[:, cl] = jnp.broadcast_to(ends[SUBLANES - 1:SUBLANES, :], (SUBLANES, LANES))
            parts = []
            for c in range(SUBLANES):
                g = jnp.broadcast_to(carry_in[c:c + 1, :], (L, LANES))
                parts.append(b_s[lt, c * P:c * P + L, :] + a_s[lt, c * P:c * P + L, :] * g)
            yl_cols.append(jnp.concatenate(parts, axis=0))
        y_lru = jnp.concatenate(yl_cols, axis=1) * jax.nn.gelu(z_s[r0:r0 + SUB, RG_OFF:RG_OFF + LRU_WIDTH])
        st["yl_n"] = _rms(y_lru, mg[:, YL_OFF:YL_OFF + LRU_WIDTH]).astype(BF16)

    def lru_dot(r, st):
        hmid[r * SUB:(r + 1) * SUB, :] += jnp.dot(st["yl_n"], wout_ref[YL_OFF:YL_OFF + LRU_WIDTH, :],
                                                  preferred_element_type=F32)

    def emit():
        pieces = []
        for c in range(N_SUB):
            pieces += [functools.partial(ffn_up, c, p) for p in range(PIECES_PER_CHUNK)]
            pieces += [functools.partial(ffn_down, c, p) for p in range(PIECES_PER_CHUNK)]
        n_pieces = len(pieces)
        last_up = n_pieces - PIECES_PER_CHUNK - 1
        n_emitted = [0]
        n_prepped = [0]

        def next_ffn(mixer_rows_done):
            k = n_emitted[0]
            pieces[k]()
            n_emitted[0] += 1
            if k >= last_up:
                while n_prepped[0] < min(mixer_rows_done, N_SUB - 1):
                    ffn_prep(n_prepped[0])
                    n_prepped[0] += 1
            if k == n_pieces - PIECES_PER_CHUNK // 2 - 1:
                ffn_finish(0)
            if k == n_pieces - 1:
                ffn_finish(1)

        next_ffn(0)
        inproj(0)
        ffn_prep(N_SUB - 1)
        for r in range(N_SUB):
            st = {}
            last = r == N_SUB - 1
            attn_qk(r, st)
            if r > 0:
                next_ffn(r)
            attn_softmax(r, 0, st)
            attn_softmax(r, 1, st)
            next_ffn(r)
            next_ffn(r)
            attn_pv(r, 0, st)
            attn_pv(r, 1, st)
            attn_norm(r, st)
            lru_pre(r, st)
            next_ffn(r)
            attn_dot(r, st)
            lru_gate_dot(r, st)
            if r % 2 == 0 and r + 2 < N_SUB:
                inproj(r // 2 + 1)
            lru_scan(r, st)
            next_ffn(r)
            conv_unit(r, 0, st)
            next_ffn(r)
            lru_dot(r, st)
            conv_unit(r, 1, st)
            next_ffn(r)
            conv_post(r, st)
            if not last:
                next_ffn(r)
            conv_dot(r, st)
            if last:
                next_ffn(r + 1)
        while n_prepped[0] < N_SUB - 1:
            ffn_prep(n_prepped[0])
            n_prepped[0] += 1

        kt_ext[:, 0:WINDOW] = kt_ext[:, T:T + WINDOW]
        v_ext[0:WINDOW, :] = v_ext[T:T + WINDOW, :]
        vr_ext[0:WINDOW, :] = vr_ext[T:T + WINDOW, :]
        u_ext[0:CONV_HALO, :] = u_ext[T:T + CONV_HALO, :]
        xl_ext[0:LRU_HALO, :] = xl_ext[T:T + LRU_HALO, :]

    @pl.when(i == 0)
    def _():
        fhn_s[...] = jnp.zeros((T, D_MODEL), BF16)
        fres_s[...] = jnp.zeros((T, D_MODEL), F32)
        hmid[(N_SUB - 1) * SUB:, :] = jnp.zeros((SUB, D_MODEL), F32)

    emit()


def _layer_spec(arr, layer):
    nd = arr.ndim
    return pl.BlockSpec((None,) + arr.shape[1:], lambda i: (layer,) + (0,) * (nd - 1),
                        pipeline_mode=pl.Buffered(1))


def _layer_call(x, sinks, consts, seq_len, layer, final):
    M, D = x.shape
    T = MIX_T
    nblk = M // T
    return pl.pallas_call(
        functools.partial(_layer_kernel, layer=layer, final=final, blocks_per_seq=seq_len // T),
        out_shape=jax.ShapeDtypeStruct((M, D), F32),
        grid=(nblk + 1,),
        in_specs=[pl.BlockSpec(memory_space=pltpu.SMEM),
                  pl.BlockSpec((T, D), lambda i: (jnp.minimum(i, nblk - 1), 0))]
                 + [_layer_spec(c, layer) for c in consts],
        out_specs=pl.BlockSpec((T, D), lambda i: (jnp.maximum(i - 1, 0), 0)),
        scratch_shapes=[
            pltpu.VMEM((KV_WIDTH, WINDOW + T), BF16),
            pltpu.VMEM((WINDOW + T, KV_WIDTH), BF16),
            pltpu.VMEM((WINDOW + T, KV_WIDTH), BF16),
            pltpu.VMEM((CONV_HALO + T, CONV_WIDTH), F32),
            pltpu.VMEM((LRU_HALO + T, LRU_WIDTH), F32),
            pltpu.VMEM((SUBLANES, LRU_WIDTH), F32),
            pltpu.VMEM((2, WINDOW, 2 * WINDOW), F32),
            pltpu.VMEM((LRU_WIDTH // LANES, SUBLANES * SCAN_PITCH, LANES), F32),
            pltpu.VMEM((LRU_WIDTH // LANES, SUBLANES * SCAN_PITCH, LANES), F32),
            pltpu.VMEM((T, IN_WIDTH), F32),
            pltpu.VMEM((T, D), F32),
            pltpu.VMEM((T, D), BF16),
            pltpu.VMEM((T, D), F32),
            pltpu.VMEM((2, T, FF_CHUNK), BF16),
        ],
        compiler_params=pltpu.CompilerParams(
            dimension_semantics=("arbitrary",), vmem_limit_bytes=VMEM_LIMIT),
        name="layer_final" if final else "layer",
    )(sinks, x, *consts)


def _block_diag(w):
    dp, hN, di, dj = w.shape
    eye = jnp.eye(hN, dtype=w.dtype)
    return (eye[None, :, None, :, None] * w[:, :, :, None, :]).reshape(dp, hN * di, hN * dj)


def kernel(x, norm1, w_in, attn_sinks, conv_dw_w, conv_dw_b, conv_ln_g, conv_ln_b, lru_conv_w, lru_conv_b,
           lru_wa, lru_ba, lru_wx, lru_bx, lru_lambda, mix_norm, w_out, norm2, w_up, w_down, final_norm):
    B, S, D = x.shape
    depth = w_in.shape[0]
    qscale = jnp.concatenate([jnp.full((ATTN_WIDTH,), HEAD_DIM ** -0.5, F32),
                              jnp.ones((IN_WIDTH - ATTN_WIDTH,), F32)])
    win = (w_in * qscale).astype(BF16)
    wg = jnp.concatenate([_block_diag(lru_wa), _block_diag(lru_wx)], axis=2).astype(BF16)
    vec = jnp.concatenate(
        [norm1, conv_dw_b, conv_ln_g, conv_ln_b, lru_conv_b, lru_ba.reshape(depth, -1), lru_bx.reshape(depth, -1),
         lru_lambda, mix_norm, norm2, jnp.broadcast_to(final_norm[None], (depth, D))], axis=1)[:, None, :]
    consts = (vec, win, conv_dw_w, lru_conv_w, wg, w_out.astype(BF16), w_up.astype(BF16), w_down.astype(BF16))
    h = x.reshape(B * S, D)
    for l in range(depth):
        h = _layer_call(h, attn_sinks, consts, S, layer=l, final=(l == depth - 1))
    return h.reshape(B, S, D)
```

```python
import functools

import jax
import jax.numpy as jnp
from jax import lax
from jax.experimental import pallas as pl
from jax.experimental.pallas import tpu as pltpu

D_MODEL = 1024
HEAD_DIM = 64
ATTN_WIDTH = 512
N_Q_HEADS = 8
N_KV_HEADS = 2
KV_WIDTH = 128
WINDOW = 128
CONV_WIDTH = 256
CONV_KERNEL = 31
LRU_WIDTH = 256
LRU_HEADS = 4
LRU_HEAD_DIM = 64
LRU_CONV_KERNEL = 4
LRU_C = 8.0
MIX_WIDTH = 1024
IN_WIDTH = 1792
D_FF = 4096
RMS_EPS = 1e-6
LN_EPS = 1e-5
MASK_VALUE = -1e30

Q_OFF, K_OFF, V_OFF, CV_OFF, CG_OFF, RX_OFF, RG_OFF = 0, 512, 640, 768, 1024, 1280, 1536
YA_OFF, YC_OFF, YL_OFF = 0, ATTN_WIDTH, ATTN_WIDTH + CONV_WIDTH

VEC_NAMES = ("norm1", "conv_b", "ln_g", "ln_b", "lru_conv_b", "gate_b", "lambda", "mix_norm", "norm2", "final_norm")
VEC_WIDTHS = dict(zip(VEC_NAMES, (D_MODEL, CONV_WIDTH, CONV_WIDTH, CONV_WIDTH, LRU_WIDTH, 2 * LRU_WIDTH,
                                  LRU_WIDTH, MIX_WIDTH, D_MODEL, D_MODEL)))
VEC_OFF = {name: sum(VEC_WIDTHS[n] for n in VEC_NAMES[:k]) for k, name in enumerate(VEC_NAMES)}

LANES = 128
SUBLANES = 8
MXU_COLS = 256

MIX_T = 512
SUB = WINDOW
N_SUB = MIX_T // SUB
FF_CHUNK = D_FF // N_SUB
PIECE_ROWS = MIX_T // 2
PIECE_COLS = 2 * MXU_COLS
PIECES_PER_CHUNK = (MIX_T // PIECE_ROWS) * (FF_CHUNK // PIECE_COLS)
CONV_HALO = 32
LRU_HALO = 8
SCAN_LEN = SUB // SUBLANES
SCAN_PITCH = SCAN_LEN + SUBLANES
VMEM_LIMIT = 56 * 1024 * 1024

F32 = jnp.float32
BF16 = jnp.bfloat16


def _sigmoid(x):
    return 0.5 * jnp.tanh(0.5 * x) + 0.5


def _unit_rms(x):
    return x * lax.rsqrt(jnp.mean(x * x, axis=-1, keepdims=True) + RMS_EPS)


def _rms(x, g):
    return _unit_rms(x) * g


class _Vec:
    def __init__(self, ref, name):
        self.ref, self.off, self.width = ref, VEC_OFF[name], VEC_WIDTHS[name]

    def __getitem__(self, idx):
        if idx is Ellipsis:
            return self.ref[:, self.off:self.off + self.width]
        rows, cols = idx
        return self.ref[rows, self.off + cols.start:self.off + cols.stop]


def _layer_kernel(sinks_ref, x_ref, vec_ref, win_ref, cw_ref, lcw_ref, wg_ref, wout_ref, wup_ref, wdn_ref,
                  o_ref,
                  kt_ext, v_ext, vr_ext, u_ext, xl_ext, hcar, bias_s, a_s, b_s,
                  z_s, hmid, fhn_s, fres_s, fu_s,
                  *, layer, final, blocks_per_seq):
    T = MIX_T
    i = pl.program_id(0)
    s = lax.rem(i, blocks_per_seq)
    n1_ref, cb_ref, lng_ref, lnb_ref, lcb_ref, bg_ref, lam_ref, mixg_ref, n2_ref, fn_ref = (
        _Vec(vec_ref, name) for name in VEC_NAMES)

    @pl.when(s == 0)
    def _():
        kt_ext[:, 0:WINDOW] = jnp.zeros((KV_WIDTH, WINDOW), BF16)
        v_ext[0:WINDOW, :] = jnp.zeros((WINDOW, KV_WIDTH), BF16)
        vr_ext[0:WINDOW, :] = jnp.zeros((WINDOW, KV_WIDTH), BF16)
        u_ext[0:CONV_HALO, :] = jnp.zeros((CONV_HALO, CONV_WIDTH), F32)
        xl_ext[0:LRU_HALO, :] = jnp.zeros((LRU_HALO, LRU_WIDTH), F32)
        hcar[...] = jnp.zeros((SUBLANES, LRU_WIDTH), F32)

    qi = lax.broadcasted_iota(jnp.int32, (WINDOW, 2 * WINDOW), 0)
    kc = lax.broadcasted_iota(jnp.int32, (WINDOW, 2 * WINDOW), 1)
    band = (kc > qi) & (kc <= qi + WINDOW)
    bias_s[0] = jnp.where(band, 0.0, MASK_VALUE).astype(F32)
    bias_s[1] = jnp.where(band & (kc >= WINDOW), 0.0, MASK_VALUE).astype(F32)

    lane = lax.broadcasted_iota(jnp.int32, (2 * WINDOW, KV_WIDTH), 1)
    band_row = lax.broadcasted_iota(jnp.int32, (2 * WINDOW, KV_WIDTH), 0)
    keep_lo = (lane < HEAD_DIM) & (band_row != 0)
    drop_hi = (lane < HEAD_DIM) | (band_row == 0)
    lane_o = lax.broadcasted_iota(jnp.int32, (WINDOW, LANES), 1)
    zk = jnp.zeros((HEAD_DIM, 2 * WINDOW), BF16)
    row8 = lax.broadcasted_iota(jnp.int32, (SUBLANES, LANES), 0)
    L, P = SCAN_LEN, SCAN_PITCH

    def ffn_prep(r):
        rows = slice(r * SUB, (r + 1) * SUB)
        h = hmid[rows, :]
        fhn_s[rows, :] = _unit_rms(h).astype(BF16)
        fres_s[rows, :] = h

    def piece(p):
        n_col = FF_CHUNK // PIECE_COLS
        rows = slice((p // n_col) * PIECE_ROWS, (p // n_col + 1) * PIECE_ROWS)
        cols = slice((p % n_col) * PIECE_COLS, (p % n_col + 1) * PIECE_COLS)
        return rows, cols

    def ffn_up(c, p):
        rows, cols = piece(p)
        u = jnp.dot(fhn_s[rows, :], wup_ref[:, c * FF_CHUNK + cols.start:c * FF_CHUNK + cols.stop],
                    preferred_element_type=F32).astype(BF16)
        u = jnp.maximum(u, 0.0)
        fu_s[c % 2, rows, cols] = u * u

    def ffn_down(c, p):
        rows, cols = piece(p)
        d = jnp.dot(fu_s[c % 2, rows, :], wdn_ref[c * FF_CHUNK:(c + 1) * FF_CHUNK, cols], preferred_element_type=F32)
        if c == 0:
            o_ref[rows, cols] = fres_s[rows, cols] + d
        else:
            o_ref[rows, cols] += d

    def ffn_finish(half):
        if final:
            rows = slice(half * PIECE_ROWS, (half + 1) * PIECE_ROWS)
            o_ref[rows, :] = _rms(o_ref[rows, :], fn_ref[...])

    def inproj(half):
        rows = slice(half * 2 * SUB, (half + 1) * 2 * SUB)
        hn = _unit_rms(x_ref[rows, :]).astype(BF16)
        z_s[rows, :] = jnp.dot(hn, win_ref[...], preferred_element_type=F32)

    def attn_qk(r, st):
        r0 = r * SUB
        q = z_s[r0:r0 + SUB, Q_OFF:Q_OFF + ATTN_WIDTH].astype(BF16)
        kf = z_s[r0:r0 + SUB, K_OFF:K_OFF + KV_WIDTH]
        vf = z_s[r0:r0 + SUB, V_OFF:V_OFF + KV_WIDTH]
        kt_ext[:, WINDOW + r0:WINDOW + r0 + SUB] = kf.T.astype(BF16)
        v_ext[WINDOW + r0:WINDOW + r0 + SUB, :] = vf.astype(BF16)
        vr_ext[WINDOW + r0:WINDOW + r0 + SUB, :] = pltpu.roll(vf, HEAD_DIM, axis=1).astype(BF16)
        kt_band = kt_ext[:, r0:r0 + 2 * WINDOW]
        st["sc"] = []
        for h in range(N_KV_HEADS):
            kh = kt_band[h * HEAD_DIM:(h + 1) * HEAD_DIM, :]
            kt2 = jnp.concatenate(
                [jnp.concatenate([kh, zk], axis=0), jnp.concatenate([zk, kh], axis=0)], axis=1)
            c0 = h * 2 * LANES
            qs = jnp.concatenate([q[:, c0:c0 + LANES], q[:, c0 + LANES:c0 + 2 * LANES]], axis=0)
            st["sc"].append(jnp.dot(qs, kt2, preferred_element_type=F32))

    def attn_softmax(r, h, st):
        bias = bias_s[jnp.where(s == 0, 1, 0)] if r == 0 else bias_s[0]
        sc = st["sc"][h]
        p_rows = []
        dens = []
        for rr in range(2):
            p_cols = []
            den_r = []
            for e in range(2):
                sink = sinks_ref[layer, 4 * h + 2 * rr + e]
                rws = slice(rr * WINDOW, (rr + 1) * WINDOW)
                c0 = e * 2 * WINDOW
                t = jnp.concatenate(
                    [jnp.where(lane_o == 0, sink, sc[rws, c0:c0 + LANES] + bias[:, :LANES]),
                     sc[rws, c0 + LANES:c0 + 2 * LANES] + bias[:, LANES:]], axis=1)
                p = jnp.exp(t - jnp.max(t, axis=-1, keepdims=True))
                den_r.append(jnp.sum(p, axis=-1, keepdims=True))
                p_cols.append(p.astype(BF16))
            p_rows.append(jnp.concatenate(p_cols, axis=1))
            dens.append(den_r)
        st[("pm", h)] = jnp.concatenate(p_rows, axis=0)
        st[("den", h)] = dens

    def attn_pv(r, h, st):
        r0 = r * SUB
        v_band = v_ext[r0:r0 + 2 * WINDOW, :]
        vr_band = vr_ext[r0:r0 + 2 * WINDOW, :]
        if h == 0:
            va = jnp.where(keep_lo, v_band, jnp.zeros_like(v_band))
            vb = jnp.where(drop_hi, jnp.zeros_like(vr_band), vr_band)
        else:
            va = jnp.where(keep_lo, vr_band, jnp.zeros_like(vr_band))
            vb = jnp.where(drop_hi, jnp.zeros_like(v_band), v_band)
        v2 = jnp.concatenate([va, vb], axis=0)
        o = jnp.dot(st[("pm", h)], v2, preferred_element_type=F32)
        dens = st[("den", h)]
        for rr in range(2):
            den = jnp.where(lane_o < HEAD_DIM, dens[rr][0], dens[rr][1])
            st[("ya", 2 * h + rr)] = o[rr * WINDOW:(rr + 1) * WINDOW, :] / den

    def attn_norm(r, st):
        y_attn = jnp.concatenate([st[("ya", c)] for c in range(ATTN_WIDTH // LANES)], axis=1)
        st["ya_n"] = _unit_rms(y_attn).astype(BF16)

    def attn_dot(r, st):
        rows = slice(r * SUB, (r + 1) * SUB)
        hmid[rows, :] = x_ref[rows, :] + jnp.dot(st["ya_n"], wout_ref[YA_OFF:YA_OFF + ATTN_WIDTH, :],
                                                 preferred_element_type=F32)

    def conv_unit(r, lt, st):
        r0 = r * SUB
        cl = slice(lt * LANES, (lt + 1) * LANES)
        cval = z_s[r0:r0 + SUB, CV_OFF + lt * LANES:CV_OFF + (lt + 1) * LANES]
        cgate = z_s[r0:r0 + SUB, CG_OFF + lt * LANES:CG_OFF + (lt + 1) * LANES]
        u_ext[CONV_HALO + r0:CONV_HALO + r0 + SUB, cl] = cval * _sigmoid(cgate)
        base = CONV_HALO - (CONV_KERNEL - 1)
        acc = jnp.broadcast_to(cb_ref[:, cl], (SUB, LANES))
        for sh in range(SUBLANES):
            part = None
            nrows = SUB + (SUBLANES if sh else 0)
            for k in range(CONV_KERNEL):
                if (base + k) % SUBLANES != sh:
                    continue
                al = r0 + base + k - sh
                term = cw_ref[k:k + 1, cl] * u_ext[al:al + nrows, cl]
                part = term if part is None else part + term
            acc = acc + part[sh:sh + SUB, :]
        st[("yc", lt)] = acc

    def conv_post(r, st):
        uc = jnp.concatenate([st[("yc", lt)] for lt in range(CONV_WIDTH // LANES)], axis=1)
        mu = jnp.mean(uc, axis=-1, keepdims=True)
        xc_ = uc - mu
        ln = xc_ * lax.rsqrt(jnp.mean(xc_ * xc_, axis=-1, keepdims=True) + LN_EPS) * lng_ref[...] + lnb_ref[...]
        y_conv = ln * _sigmoid(ln)
        st["yc_n"] = _unit_rms(y_conv).astype(BF16)

    def conv_dot(r, st):
        hmid[r * SUB:(r + 1) * SUB, :] += jnp.dot(st["yc_n"], wout_ref[YC_OFF:YC_OFF + CONV_WIDTH, :],
                                                  preferred_element_type=F32)

    def lru_pre(r, st):
        r0 = r * SUB
        xl_ext[LRU_HALO + r0:LRU_HALO + r0 + SUB, :] = z_s[r0:r0 + SUB, RX_OFF:RX_OFF + LRU_WIDTH]
        lbase = LRU_HALO - (LRU_CONV_KERNEL - 1)
        xc = jnp.broadcast_to(lcb_ref[...], (SUB, LRU_WIDTH))
        for k in range(LRU_CONV_KERNEL):
            xc = xc + lcw_ref[k:k + 1, :] * xl_ext[r0 + lbase + k:r0 + lbase + k + SUB, :]
        st["xc"] = xc
        st["xc_b"] = xc.astype(BF16)

    def lru_gate_dot(r, st):
        st["gates"] = jnp.dot(st["xc_b"], wg_ref[...], preferred_element_type=F32) + bg_ref[...]

    def lru_scan(r, st):
        r0 = r * SUB
        xc, gates = st["xc"], st["gates"]
        rgate = _sigmoid(gates[:, :LRU_WIDTH])
        igate = _sigmoid(gates[:, LRU_WIDTH:])
        log_a = (-LRU_C * rgate) * jax.nn.softplus(-lam_ref[...])
        a_full = jnp.exp(log_a)
        th = jnp.tanh(log_a)
        b_full = jnp.sqrt(2.0 * th / (th - 1.0)) * (igate * xc)
        yl_cols = []
        for lt in range(LRU_WIDTH // LANES):
            cl = slice(lt * LANES, (lt + 1) * LANES)
            for c in range(SUBLANES):
                a_s[lt, c * P:c * P + L, :] = a_full[c * L:(c + 1) * L, cl]
                b_s[lt, c * P:c * P + L, :] = b_full[c * L:(c + 1) * L, cl]
            hloc = jnp.zeros((SUBLANES, LANES), F32)
            cum = jnp.ones((SUBLANES, LANES), F32)
            for m in range(L):
                am = a_s[lt, pl.ds(m, SUBLANES, stride=P), :]
                bm = b_s[lt, pl.ds(m, SUBLANES, stride=P), :]
                hloc = am * hloc + bm
                cum = am * cum
                b_s[lt, pl.ds(m, SUBLANES, stride=P), :] = hloc
                a_s[lt, pl.ds(m, SUBLANES, stride=P), :] = cum
            ca, cbv = cum, hloc
            for d in (1, 2, 4):
                a_sh = jnp.where(row8 >= d, pltpu.roll(ca, d, axis=0), 1.0)
                b_sh = jnp.where(row8 >= d, pltpu.roll(cbv, d, axis=0), 0.0)
                cbv = ca * b_sh + cbv
                ca = ca * a_sh
            hprev = hcar[:, cl]
            ends = ca * hprev + cbv
            carry_in = jnp.where(row8 == 0, hprev, pltpu.roll(ends, 1, axis=0))
            hcar[:, cl] = jnp.broadcast_to(ends[SUBLANES - 1:SUBLANES, :], (SUBLANES, LANES))
            parts = []
            for c in range(SUBLANES):
                g = jnp.broadcast_to(carry_in[c:c + 1, :], (L, LANES))
                parts.append(b_s[lt, c * P:c * P + L, :] + a_s[lt, c * P:c * P + L, :] * g)
            yl_cols.append(jnp.concatenate(parts, axis=0))
        y_lru = jnp.concatenate(yl_cols, axis=1) * jax.nn.gelu(z_s[r0:r0 + SUB, RG_OFF:RG_OFF + LRU_WIDTH])
        st["yl_n"] = _unit_rms(y_lru).astype(BF16)

    def lru_dot(r, st):
        hmid[r * SUB:(r + 1) * SUB, :] += jnp.dot(st["yl_n"], wout_ref[YL_OFF:YL_OFF + LRU_WIDTH, :],
                                                  preferred_element_type=F32)

    def emit():
        pieces = []
        for c in range(N_SUB):
            pieces += [functools.partial(ffn_up, c, p) for p in range(PIECES_PER_CHUNK)]
            pieces += [functools.partial(ffn_down, c, p) for p in range(PIECES_PER_CHUNK)]
        n_pieces = len(pieces)
        last_up = n_pieces - PIECES_PER_CHUNK - 1
        n_emitted = [0]
        n_prepped = [0]

        def next_ffn(mixer_rows_done):
            k = n_emitted[0]
            pieces[k]()
            n_emitted[0] += 1
            if k >= last_up:
                while n_prepped[0] < mixer_rows_done:
                    ffn_prep(n_prepped[0])
                    n_prepped[0] += 1
            if k == n_pieces - PIECES_PER_CHUNK // 2 - 1:
                ffn_finish(0)
            if k == n_pieces - 1:
                ffn_finish(1)

        next_ffn(0)
        inproj(0)
        for r in range(N_SUB):
            st = {}
            last = r == N_SUB - 1
            attn_qk(r, st)
            if r > 0:
                next_ffn(r)
            attn_softmax(r, 0, st)
            attn_softmax(r, 1, st)
            next_ffn(r)
            next_ffn(r)
            attn_pv(r, 0, st)
            attn_pv(r, 1, st)
            attn_norm(r, st)
            lru_pre(r, st)
            next_ffn(r)
            attn_dot(r, st)
            lru_gate_dot(r, st)
            if r % 2 == 0 and r + 2 < N_SUB:
                inproj(r // 2 + 1)
            lru_scan(r, st)
            next_ffn(r)
            conv_unit(r, 0, st)
            next_ffn(r)
            lru_dot(r, st)
            conv_unit(r, 1, st)
            next_ffn(r)
            conv_post(r, st)
            if not last:
                next_ffn(r)
            conv_dot(r, st)
            if last:
                next_ffn(r + 1)
        while n_prepped[0] < N_SUB:
            ffn_prep(n_prepped[0])
            n_prepped[0] += 1

        kt_ext[:, 0:WINDOW] = kt_ext[:, T:T + WINDOW]
        v_ext[0:WINDOW, :] = v_ext[T:T + WINDOW, :]
        vr_ext[0:WINDOW, :] = vr_ext[T:T + WINDOW, :]
        u_ext[0:CONV_HALO, :] = u_ext[T:T + CONV_HALO, :]
        xl_ext[0:LRU_HALO, :] = xl_ext[T:T + LRU_HALO, :]

    @pl.when(i == 0)
    def _():
        fhn_s[...] = jnp.zeros((T, D_MODEL), BF16)
        fres_s[...] = jnp.zeros((T, D_MODEL), F32)

    emit()


def _layer_spec(arr, layer):
    nd = arr.ndim
    return pl.BlockSpec((None,) + arr.shape[1:], lambda i: (layer,) + (0,) * (nd - 1),
                        pipeline_mode=pl.Buffered(1))


def _layer_call(x, sinks, consts, seq_len, layer, final):
    M, D = x.shape
    T = MIX_T
    nblk = M // T
    return pl.pallas_call(
        functools.partial(_layer_kernel, layer=layer, final=final, blocks_per_seq=seq_len // T),
        out_shape=jax.ShapeDtypeStruct((M, D), F32),
        grid=(nblk + 1,),
        in_specs=[pl.BlockSpec(memory_space=pltpu.SMEM),
                  pl.BlockSpec((T, D), lambda i: (jnp.minimum(i, nblk - 1), 0))]
                 + [_layer_spec(c, layer) for c in consts],
        out_specs=pl.BlockSpec((T, D), lambda i: (jnp.maximum(i - 1, 0), 0)),
        scratch_shapes=[
            pltpu.VMEM((KV_WIDTH, WINDOW + T), BF16),
            pltpu.VMEM((WINDOW + T, KV_WIDTH), BF16),
            pltpu.VMEM((WINDOW + T, KV_WIDTH), BF16),
            pltpu.VMEM((CONV_HALO + T, CONV_WIDTH), F32),
            pltpu.VMEM((LRU_HALO + T, LRU_WIDTH), F32),
            pltpu.VMEM((SUBLANES, LRU_WIDTH), F32),
            pltpu.VMEM((2, WINDOW, 2 * WINDOW), F32),
            pltpu.VMEM((LRU_WIDTH // LANES, SUBLANES * SCAN_PITCH, LANES), F32),
            pltpu.VMEM((LRU_WIDTH // LANES, SUBLANES * SCAN_PITCH, LANES), F32),
            pltpu.VMEM((T, IN_WIDTH), F32),
            pltpu.VMEM((T, D), F32),
            pltpu.VMEM((T, D), BF16),
            pltpu.VMEM((T, D), F32),
            pltpu.VMEM((2, T, FF_CHUNK), BF16),
        ],
        compiler_params=pltpu.CompilerParams(
            dimension_semantics=("arbitrary",), vmem_limit_bytes=VMEM_LIMIT),
        name="layer_final" if final else "layer",
    )(sinks, x, *consts)


def _block_diag(w):
    dp, hN, di, dj = w.shape
    eye = jnp.eye(hN, dtype=w.dtype)
    return (eye[None, :, None, :, None] * w[:, :, :, None, :]).reshape(dp, hN * di, hN * dj)


def kernel(x, norm1, w_in, attn_sinks, conv_dw_w, conv_dw_b, conv_ln_g, conv_ln_b, lru_conv_w, lru_conv_b,
           lru_wa, lru_ba, lru_wx, lru_bx, lru_lambda, mix_norm, w_out, norm2, w_up, w_down, final_norm):
    B, S, D = x.shape
    depth = w_in.shape[0]
    qscale = jnp.concatenate([jnp.full((ATTN_WIDTH,), HEAD_DIM ** -0.5, F32),
                              jnp.ones((IN_WIDTH - ATTN_WIDTH,), F32)])
    win = (norm1[:, :, None] * w_in * qscale).astype(BF16)
    wg = jnp.concatenate([_block_diag(lru_wa), _block_diag(lru_wx)], axis=2).astype(BF16)
    vec = jnp.concatenate(
        [norm1, conv_dw_b, conv_ln_g, conv_ln_b, lru_conv_b, lru_ba.reshape(depth, -1), lru_bx.reshape(depth, -1),
         lru_lambda, mix_norm, norm2, jnp.broadcast_to(final_norm[None], (depth, D))], axis=1)[:, None, :]
    consts = (vec, win, conv_dw_w, lru_conv_w, wg, (mix_norm[:, :, None] * w_out).astype(BF16),
              (norm2[:, :, None] * w_up).astype(BF16), w_down.astype(BF16))
    h = x.reshape(B * S, D)
    for l in range(depth):
        h = _layer_call(h, attn_sinks, consts, S, layer=l, final=(l == depth - 1))
    return h.reshape(B, S, D)
```
